```python
import math
import jax, jax.numpy as jnp
from jax import lax
import numpy as np

D_MODEL = 1024
BATCH = 16
SEQ = 4096
DEPTH = 4

GRID_W = 64
CTX_LEN = 256
EPS = 1e-6
CONV_W = 4
N_BRANCH = 3
S5_WIDTH = D_MODEL // 2
S5_GROUP = 16
S5_GROUPS = S5_WIDTH // S5_GROUP
S5_STATE = 64
M2_HEAD_DIM = 64
M2_INNER = D_MODEL
M2_HEADS = M2_INNER // M2_HEAD_DIM
M2_GROUPS = 4
M2_HPG = M2_HEADS // M2_GROUPS
M2_STATE = 64
M2_CHUNK = 128
M2_CONV_DIM = M2_INNER + 2 * M2_GROUPS * M2_STATE
LRU_WIDTH = D_MODEL // 2
LRU_BLOCKS = 8
LRU_BLOCK = LRU_WIDTH // LRU_BLOCKS
LRU_C = 8.0
N_EXPERTS = 16
CAPACITY = 2
D_EXPERT = D_MODEL
D_IN = S5_WIDTH + M2_INNER + M2_CONV_DIM + 2 * M2_HEADS + 2 * LRU_WIDTH + N_BRANCH * D_MODEL

kernel_name = 'hybrid_s5_ssd_rglru_ecmoe_diffusion'


def _rmsnorm(x, g):
    x32 = x.astype(jnp.float32)
    y = x32 * lax.rsqrt(jnp.mean(x32 * x32, axis=-1, keepdims=True) + EPS)
    return (y * g.astype(jnp.float32)).astype(x.dtype)


def _modulate(x, g, shift, scale):
    return _rmsnorm(x, g) * (1 + scale) + shift


def _rev(t):
    return jnp.flip(t, axis=1)


def _dwconv(x, w, b):
    k = w.shape[0]
    y = lax.conv_general_dilated(x, w.astype(x.dtype)[:, None, :], window_strides=(1,), padding=[(k // 2, k - 1 - k // 2)], dimension_numbers=('NWC', 'WIO', 'NWC'), feature_group_count=x.shape[-1])
    return y + b.astype(x.dtype)


def _to_col_major(t, rows):
    b, l, ch = t.shape
    return t.reshape(b, rows, GRID_W, ch).transpose(0, 2, 1, 3).reshape(b, l, ch)


def _to_row_major(t, rows):
    b, l, ch = t.shape
    return t.reshape(b, GRID_W, rows, ch).transpose(0, 2, 1, 3).reshape(b, l, ch)


def _segsum(a):
    t = a.shape[-1]
    cs = jnp.cumsum(a, axis=-1)
    diff = cs[..., :, None] - cs[..., None, :]
    mask = jnp.tril(jnp.ones((t, t), dtype=bool))
    return jnp.where(mask, diff, -jnp.inf)


def _linear_scan(a, v, h0):
    if h0 is not None:
        v = v.at[:, 0].add(a[:, 0] * h0)
    def comb(e1, e2):
        a1, b1 = e1
        a2, b2 = e2
        return a1 * a2, a2 * b1 + b2
    _, h = lax.associative_scan(comb, (a, v), axis=1)
    return h, h[:, -1]


def _s5_scan(u, lam_re, lam_im, log_step, b_re, b_im, c_re, c_im, h0):
    f32 = jnp.float32
    lam_re, lam_im = lam_re.astype(f32), lam_im.astype(f32)
    step = jnp.exp(log_step.astype(f32))[:, None]
    ar, ai = lam_re * step, lam_im * step
    mag = jnp.exp(ar)
    bar_re, bar_im = mag * jnp.cos(ai), mag * jnp.sin(ai)
    den = lam_re * lam_re + lam_im * lam_im
    nr, ni = bar_re - 1.0, bar_im
    coef_re = (nr * lam_re + ni * lam_im) / den
    coef_im = (ni * lam_re - nr * lam_im) / den
    bu_re = jnp.einsum('blgi,gpi->blgp', u, b_re.astype(f32))
    bu_im = jnp.einsum('blgi,gpi->blgp', u, b_im.astype(f32))
    v_re = coef_re * bu_re - coef_im * bu_im
    v_im = coef_re * bu_im + coef_im * bu_re
    if h0 is not None:
        h_re, h_im = h0
        v_re = v_re.at[:, 0].add(bar_re * h_re - bar_im * h_im)
        v_im = v_im.at[:, 0].add(bar_re * h_im + bar_im * h_re)
    cnt = jnp.ones((1, u.shape[1], 1, 1), f32)
    def comb(e1, e2):
        n1, r1, i1 = e1
        n2, r2, i2 = e2
        m = jnp.exp(n2 * ar)
        pr, pi = m * jnp.cos(n2 * ai), m * jnp.sin(n2 * ai)
        return n1 + n2, pr * r1 - pi * i1 + r2, pr * i1 + pi * r1 + i2
    _, s_re, s_im = lax.associative_scan(comb, (cnt, v_re, v_im), axis=1)
    y = jnp.einsum('blgp,gip->blgi', s_re, c_re.astype(f32)) - jnp.einsum('blgp,gip->blgi', s_im, c_im.astype(f32))
    return y, (s_re[:, -1], s_im[:, -1])


def _s5_branch(u, p, h0, rows):
    bsz, l, _ = u.shape
    u32 = u.astype(jnp.float32)
    us = u32 if rows is None else _to_col_major(u32, rows)
    ug = us.reshape(bsz, l, S5_GROUPS, S5_GROUP)
    y = jnp.zeros_like(ug)
    finals = []
    for d in range(2):
        inp = ug if d == 0 else _rev(ug)
        yd, fin = _s5_scan(inp, p['s5_lam_re'][d], p['s5_lam_im'][d], p['s5_log_step'][d], p['s5_b_re'][d], p['s5_b_im'][d], p['s5_c_re'][d], p['s5_c_im'][d], None if h0 is None else h0[d])
        y = y + (yd if d == 0 else _rev(yd))
        finals.append(fin)
    y = y.reshape(bsz, l, S5_WIDTH)
    if rows is not None:
        y = _to_row_major(y, rows)
    y = jax.nn.gelu(y + p['s5_d'].astype(jnp.float32) * u32).astype(u.dtype)
    val, gate = jnp.split(y @ p['s5_w_glu'], 2, axis=-1)
    return val * jax.nn.sigmoid(gate), finals


def _ssd_scan(xs, la, bm, cm, h0):
    bsz, l, g, j, hp = xs.shape
    n = bm.shape[-1]
    nc = l // M2_CHUNK
    xs = xs.reshape(bsz, nc, M2_CHUNK, g, j, hp)
    bm = bm.reshape(bsz, nc, M2_CHUNK, g, n)
    cm = cm.reshape(bsz, nc, M2_CHUNK, g, n)
    a = la.reshape(bsz, nc, M2_CHUNK, g, j).transpose(0, 3, 4, 1, 2)
    a_cum = jnp.cumsum(a, axis=-1)
    cb = jnp.einsum('bcqgn,bcsgn->bgcqs', cm, bm)
    lmat = jnp.exp(_segsum(a))
    y_diag = jnp.einsum('bgjcqs,bcsgjp->bcqgjp', cb[:, :, None] * lmat, xs)
    decay_in = jnp.exp(a_cum[..., -1:] - a_cum).transpose(0, 3, 4, 1, 2)
    st = jnp.einsum('bcsgn,bcsgjp->bcgjpn', bm, xs * decay_in[..., None])
    if h0 is None:
        h0 = jnp.zeros((bsz, g, j, hp, n), xs.dtype)
    st = jnp.concatenate([h0[:, None], st], axis=1)
    chunk_a = jnp.pad(a_cum[..., -1], ((0, 0), (0, 0), (0, 0), (1, 0)))
    decay_chunk = jnp.exp(_segsum(chunk_a))
    st = jnp.einsum('bgjzc,bcgjpn->bzgjpn', decay_chunk, st)
    entering, final = st[:, :-1], st[:, -1]
    decay_out = jnp.exp(a_cum).transpose(0, 3, 4, 1, 2)
    y_off = jnp.einsum('bcqgn,bcgjpn->bcqgjp', cm, entering) * decay_out[..., None]
    return (y_diag + y_off).reshape(bsz, l, g, j, hp), final


def _mamba2_branch(z, xbc, dt_raw, p, h0):
    bsz, l, _ = z.shape
    f32 = jnp.float32
    xbc = jax.nn.silu(_dwconv(xbc, p['m2_conv_w'], p['m2_conv_b']).astype(f32))
    gn = M2_GROUPS * M2_STATE
    xm = xbc[..., :M2_INNER].reshape(bsz, l, M2_GROUPS, M2_HPG, M2_HEAD_DIM)
    bm = xbc[..., M2_INNER:M2_INNER + gn].reshape(bsz, l, M2_GROUPS, M2_STATE)
    cm = xbc[..., M2_INNER + gn:].reshape(bsz, l, M2_GROUPS, M2_STATE)
    dtv = jax.nn.softplus(dt_raw.astype(f32).reshape(bsz, l, 2, M2_GROUPS, M2_HPG) + p['m2_dt_bias'].astype(f32).reshape(2, M2_GROUPS, M2_HPG))
    a = -jnp.exp(p['m2_a_log'].astype(f32)).reshape(2, M2_GROUPS, M2_HPG)
    y = p['m2_d'].astype(f32).reshape(M2_GROUPS, M2_HPG, 1) * xm
    finals = []
    for d in range(2):
        dtd = dtv[:, :, d]
        xs, la, bs, cs = xm * dtd[..., None], dtd * a[d], bm, cm
        if d == 1:
            xs, la, bs, cs = _rev(xs), _rev(la), _rev(bs), _rev(cs)
        yd, fin = _ssd_scan(xs, la, bs, cs, None if h0 is None else h0[d])
        y = y + (yd if d == 0 else _rev(yd))
        finals.append(fin)
    y = y.reshape(bsz, l, M2_INNER) * jax.nn.silu(z.astype(f32))
    y = _rmsnorm(y, p['m2_norm_g']).astype(z.dtype)
    return y @ p['m2_w_out'], finals


def _rglru_branch(xc, gc, p, h0):
    bsz, l, _ = xc.shape
    f32 = jnp.float32
    xc32 = _dwconv(xc, p['lru_conv_w'], p['lru_conv_b']).astype(f32)
    xb = xc32.reshape(bsz, l, LRU_BLOCKS, LRU_BLOCK)
    h = jnp.zeros_like(xc32)
    finals = []
    for d in range(2):
        r = jax.nn.sigmoid(jnp.einsum('blhi,hij->blhj', xb, p['lru_w_a'][d].astype(f32)).reshape(bsz, l, LRU_WIDTH) + p['lru_b_a'][d].astype(f32))
        ig = jax.nn.sigmoid(jnp.einsum('blhi,hij->blhj', xb, p['lru_w_x'][d].astype(f32)).reshape(bsz, l, LRU_WIDTH) + p['lru_b_x'][d].astype(f32))
        log_a = -LRU_C * r * jax.nn.softplus(-p['lru_lam'][d].astype(f32))
        a_t = jnp.exp(log_a)
        v = jnp.sqrt(jnp.maximum(-jnp.expm1(2.0 * log_a), EPS)) * (ig * xc32)
        if d == 1:
            a_t, v = _rev(a_t), _rev(v)
        hd, fin = _linear_scan(a_t, v, None if h0 is None else h0[d])
        h = h + (hd if d == 0 else _rev(hd))
        finals.append(fin)
    y = (h * jax.nn.gelu(gc.astype(f32))).astype(xc.dtype)
    return y @ p['lru_w_out'], finals


def _token_mixer(h, p, h0, rows):
    sizes = (S5_WIDTH, M2_INNER, M2_CONV_DIM, 2 * M2_HEADS, LRU_WIDTH, LRU_WIDTH)
    idx = np.cumsum(sizes).tolist()
    proj = h @ p['w_in']
    u_s5, z, xbc, dt_raw, x_lru, g_lru, gates = jnp.split(proj, idx, axis=-1)
    h0a, h0b, h0c = (None, None, None) if h0 is None else h0
    ya, sa = _s5_branch(u_s5, p, h0a, rows)
    yb, sb = _mamba2_branch(z, xbc, dt_raw, p, h0b)
    yc, sc = _rglru_branch(x_lru, g_lru, p, h0c)
    f32 = jnp.float32
    g = jax.nn.sigmoid(gates.astype(f32)).reshape(gates.shape[0], gates.shape[1], N_BRANCH, D_MODEL)
    merged = g[:, :, 0] * ya.astype(f32) + g[:, :, 1] * yb.astype(f32) + g[:, :, 2] * yc.astype(f32)
    return merged.astype(h.dtype) @ p['w_o'], (sa, sb, sc)


def _expert_choice_moe(h, p):
    bsz, n, d = h.shape
    cap = CAPACITY * n // N_EXPERTS
    aff = jax.nn.softmax((h @ p['moe_w_router']).astype(jnp.float32), axis=-1)
    wts, idx = lax.top_k(aff.transpose(0, 2, 1), cap)
    xs = jax.vmap(lambda hb, ib: hb[ib])(h, idx)
    hid = jax.nn.silu(jnp.einsum('becd,edf->becf', xs, p['moe_w1'])) * jnp.einsum('becd,edf->becf', xs, p['moe_w3'])
    ys = jnp.einsum('becf,efd->becd', hid, p['moe_w2']) * wts[..., None].astype(h.dtype)
    return jax.vmap(lambda yb, ib: jnp.zeros((n, d), yb.dtype).at[ib.reshape(-1)].add(yb.reshape(-1, d)))(ys, idx)


def setup_inputs(seed: int = 0) -> dict:
    key = jax.random.key(seed)
    keys = iter(jax.random.split(key, 64))
    f32 = jnp.float32
    def nrm(shape, scale):
        return jax.random.normal(next(keys), shape, f32) * scale
    def uni(shape, lo, hi):
        return jax.random.uniform(next(keys), shape, f32, lo, hi)
    L, D = DEPTH, D_MODEL
    G, P, I = S5_GROUPS, S5_STATE, S5_GROUP
    x = nrm((BATCH, SEQ, D), 1.0)
    c = nrm((BATCH, D), 1.0)
    ctx = nrm((BATCH, CTX_LEN, D), 1.0)
    c_ctx = nrm((D,), 1.0)
    w_mod = nrm((L, D, 6 * D), 0.5 * D ** -0.5)
    b_mod = nrm((L, 6 * D), 0.01)
    norm1_g = 1.0 + nrm((L, D), 0.01)
    norm2_g = 1.0 + nrm((L, D), 0.01)
    w_in = nrm((L, D, D_IN), D ** -0.5)
    s5_lam_re = -0.5 + nrm((L, 2, G, P), 0.01)
    s5_lam_im = math.pi * jnp.arange(P, dtype=f32) + nrm((L, 2, G, P), 0.01)
    s5_log_step = uni((L, 2, G), math.log(1e-3), math.log(1e-1))
    s5_b_re = nrm((L, 2, G, P, I), (2 * I) ** -0.5)
    s5_b_im = nrm((L, 2, G, P, I), (2 * I) ** -0.5)
    s5_c_re = nrm((L, 2, G, I, P), (2 * P) ** -0.5)
    s5_c_im = nrm((L, 2, G, I, P), (2 * P) ** -0.5)
    s5_d = nrm((L, S5_WIDTH), 0.5)
    s5_w_glu = nrm((L, S5_WIDTH, 2 * D), S5_WIDTH ** -0.5)
    m2_conv_w = nrm((L, CONV_W, M2_CONV_DIM), CONV_W ** -0.5)
    m2_conv_b = nrm((L, M2_CONV_DIM), 0.01)
    dt0 = jnp.exp(uni((L, 2, M2_HEADS), math.log(1e-3), math.log(1e-1)))
    m2_dt_bias = dt0 + jnp.log(-jnp.expm1(-dt0))
    m2_a_log = jnp.log(uni((L, 2, M2_HEADS), 1.0, 16.0))
    m2_d = 1.0 + nrm((L, M2_HEADS), 0.1)
    m2_norm_g = 1.0 + nrm((L, M2_INNER), 0.01)
    m2_w_out = nrm((L, M2_INNER, D), M2_INNER ** -0.5)
    lru_conv_w = nrm((L, CONV_W, LRU_WIDTH), CONV_W ** -0.5)
    lru_conv_b = nrm((L, LRU_WIDTH), 0.01)
    lru_w_a = nrm((L, 2, LRU_BLOCKS, LRU_BLOCK, LRU_BLOCK), LRU_BLOCK ** -0.5)
    lru_b_a = nrm((L, 2, LRU_WIDTH), 0.01)
    lru_w_x = nrm((L, 2, LRU_BLOCKS, LRU_BLOCK, LRU_BLOCK), LRU_BLOCK ** -0.5)
    lru_b_x = nrm((L, 2, LRU_WIDTH), 0.01)
    a0 = uni((L, 2, LRU_WIDTH), 0.9, 0.999)
    s = a0 ** (1.0 / LRU_C)
    lru_lam = jnp.log(s) - jnp.log1p(-s)
    lru_w_out = nrm((L, LRU_WIDTH, D), LRU_WIDTH ** -0.5)
    w_o = nrm((L, D, D), D ** -0.5)
    moe_w_router = nrm((L, D, N_EXPERTS), D ** -0.5)
    moe_w1 = nrm((L, N_EXPERTS, D, D_EXPERT), D ** -0.5)
    moe_w3 = nrm((L, N_EXPERTS, D, D_EXPERT), D ** -0.5)
    moe_w2 = nrm((L, N_EXPERTS, D_EXPERT, D), D_EXPERT ** -0.5)
    final_norm_g = 1.0 + nrm((D,), 0.01)
    return {'x': x, 'c': c, 'ctx': ctx, 'c_ctx': c_ctx, 'w_mod': w_mod, 'b_mod': b_mod, 'norm1_g': norm1_g, 'norm2_g': norm2_g, 'w_in': w_in, 's5_lam_re': s5_lam_re, 's5_lam_im': s5_lam_im, 's5_log_step': s5_log_step, 's5_b_re': s5_b_re, 's5_b_im': s5_b_im, 's5_c_re': s5_c_re, 's5_c_im': s5_c_im, 's5_d': s5_d, 's5_w_glu': s5_w_glu, 'm2_conv_w': m2_conv_w, 'm2_conv_b': m2_conv_b, 'm2_dt_bias': m2_dt_bias, 'm2_a_log': m2_a_log, 'm2_d': m2_d, 'm2_norm_g': m2_norm_g, 'm2_w_out': m2_w_out, 'lru_conv_w': lru_conv_w, 'lru_conv_b': lru_conv_b, 'lru_w_a': lru_w_a, 'lru_b_a': lru_b_a, 'lru_w_x': lru_w_x, 'lru_b_x': lru_b_x, 'lru_lam': lru_lam, 'lru_w_out': lru_w_out, 'w_o': w_o, 'moe_w_router': moe_w_router, 'moe_w1': moe_w1, 'moe_w3': moe_w3, 'moe_w2': moe_w2, 'final_norm_g': final_norm_g}


def reference(x, c, ctx, c_ctx, w_mod, b_mod, norm1_g, norm2_g, w_in, s5_lam_re, s5_lam_im, s5_log_step, s5_b_re, s5_b_im, s5_c_re, s5_c_im, s5_d, s5_w_glu, m2_conv_w, m2_conv_b, m2_dt_bias, m2_a_log, m2_d, m2_norm_g, m2_w_out, lru_conv_w, lru_conv_b, lru_w_a, lru_b_a, lru_w_x, lru_b_x, lru_lam, lru_w_out, w_o, moe_w_router, moe_w1, moe_w3, moe_w2, final_norm_g):
    rows = x.shape[1] // GRID_W
    sc = jax.nn.silu(c)
    scc = jax.nn.silu(c_ctx)
    for i in range(DEPTH):
        p = {'w_in': w_in[i], 's5_lam_re': s5_lam_re[i], 's5_lam_im': s5_lam_im[i], 's5_log_step': s5_log_step[i], 's5_b_re': s5_b_re[i], 's5_b_im': s5_b_im[i], 's5_c_re': s5_c_re[i], 's5_c_im': s5_c_im[i], 's5_d': s5_d[i], 's5_w_glu': s5_w_glu[i], 'm2_conv_w': m2_conv_w[i], 'm2_conv_b': m2_conv_b[i], 'm2_dt_bias': m2_dt_bias[i], 'm2_a_log': m2_a_log[i], 'm2_d': m2_d[i], 'm2_norm_g': m2_norm_g[i], 'm2_w_out': m2_w_out[i], 'lru_conv_w': lru_conv_w[i], 'lru_conv_b': lru_conv_b[i], 'lru_w_a': lru_w_a[i], 'lru_b_a': lru_b_a[i], 'lru_w_x': lru_w_x[i], 'lru_b_x': lru_b_x[i], 'lru_lam': lru_lam[i], 'lru_w_out': lru_w_out[i], 'w_o': w_o[i], 'moe_w_router': moe_w_router[i], 'moe_w1': moe_w1[i], 'moe_w3': moe_w3[i], 'moe_w2': moe_w2[i]}
        mod_x = jnp.split((sc @ w_mod[i] + b_mod[i])[:, None, :], 6, axis=-1)
        mod_c = jnp.split((scc @ w_mod[i] + b_mod[i])[None, None, :], 6, axis=-1)
        mc, ctx_states = _token_mixer(_modulate(ctx, norm1_g[i], mod_c[0], mod_c[1]), p, None, None)
        if i < DEPTH - 1:
            ctx = ctx + mod_c[2] * mc
            ctx = ctx + mod_c[5] * _expert_choice_moe(_modulate(ctx, norm2_g[i], mod_c[3], mod_c[4]), p)
        mx, _ = _token_mixer(_modulate(x, norm1_g[i], mod_x[0], mod_x[1]), p, ctx_states, rows)
        x = x + mod_x[2] * mx
        x = x + mod_x[5] * _expert_choice_moe(_modulate(x, norm2_g[i], mod_x[3], mod_x[4]), p)
    return _rmsnorm(x, final_norm_g)
```

```python
import functools
import math

import numpy as np
import jax
import jax.numpy as jnp
from jax import lax
from jax.experimental import pallas as pl
from jax.experimental.pallas import tpu as pltpu

F32 = jnp.float32
BF16 = jnp.bfloat16
I32 = jnp.int32

GRID_W = 64
EPS = 1e-6
CONV_W = 4
S5_GROUP = 16
S5_STATE = 64
M2_HEAD_DIM = 64
M2_GROUPS = 4
M2_STATE = 64
M2_CHUNK = 128
LRU_BLOCKS = 8
LRU_C = 8.0
N_EXPERTS = 16
CAPACITY = 2
N_BRANCH = 3

LANE = 128
SUBLANE = 8
S5_CHUNK = 128
LRU_TILE = 64
VMEM_LIMIT = 56 * 1024 * 1024


def _cparams(sem):
    return pltpu.CompilerParams(dimension_semantics=sem, vmem_limit_bytes=VMEM_LIMIT)


def _dot(a, b):
    return jnp.dot(a.astype(BF16), b.astype(BF16), preferred_element_type=F32)


def _dot_nt(a, b):
    return lax.dot_general(a.astype(BF16), b.astype(BF16), (((1,), (1,)), ((), ())), preferred_element_type=F32)


def _split2(a):
    hi = a.astype(BF16)
    lo = (a - hi.astype(F32)).astype(BF16)
    return hi, lo


def _split3(a):
    hi = a.astype(BF16)
    r = a - hi.astype(F32)
    mid = r.astype(BF16)
    lo = (r - mid.astype(F32)).astype(BF16)
    return hi, mid, lo


def _dot3(a, b):
    ah, al = _split2(a)
    bh, bl = _split2(b)
    d = lambda x, y: jnp.dot(x, y, preferred_element_type=F32)
    return d(ah, bh) + (d(ah, bl) + d(al, bh))


def _dot_exact_lhs(m01, x):
    hi, mid, lo = _split3(x)
    d = lambda y: jnp.dot(m01, y, preferred_element_type=F32)
    return d(hi) + (d(mid) + d(lo))


def _silu(x):
    return x * jax.nn.sigmoid(x)


def _softplus(x):
    return jnp.maximum(x, 0.0) + jnp.log(1.0 + jnp.exp(-jnp.abs(x)))


def _rms_scale(x):
    return x * lax.rsqrt(jnp.mean(x * x, axis=-1, keepdims=True) + EPS)


def _mod_kernel(c_ref, w_ref, b_ref, o_ref):
    o_ref[...] = _dot3(_silu(c_ref[...]), w_ref[...]) + b_ref[...]


def _modulation(c_rows, w_mod, b_mod):
    depth, d, n6 = w_mod.shape
    rm = c_rows.shape[0]
    tn = min(1024, n6)
    return pl.pallas_call(
        _mod_kernel,
        out_shape=jax.ShapeDtypeStruct((depth, rm, n6), F32),
        grid=(depth, n6 // tn),
        in_specs=[pl.BlockSpec((rm, d), lambda l, j: (0, 0)),
                  pl.BlockSpec((None, d, tn), lambda l, j: (l, 0, j)),
                  pl.BlockSpec((None, 1, tn), lambda l, j: (l, 0, j))],
        out_specs=pl.BlockSpec((None, rm, tn), lambda l, j: (l, 0, j)),
        compiler_params=_cparams(("arbitrary", "arbitrary")),
        name="modulation",
    )(c_rows, w_mod, b_mod.reshape(depth, 1, n6))


def _inproj_kernel(x_ref, g_ref, sh_ref, sc_ref, w_ref, o_ref, *, nchunk):
    h = _rms_scale(x_ref[...]) * g_ref[...] * (1.0 + sc_ref[...]) + sh_ref[...]
    hb = h.astype(BF16)
    npad = o_ref.shape[1]
    for n0 in range(0, npad, nchunk):
        o_ref[:, n0:n0 + nchunk] = jnp.dot(hb, w_ref[:, n0:n0 + nchunk], preferred_element_type=F32)


def _mod_index(bm, tm, seq):
    if bm == 1:
        return lambda i: (0, 0, 0)
    return lambda i: ((i * tm) // seq, 0, 0)


def _inproj(x2d, seq, g, shift, scale, w):
    t, d = x2d.shape
    npad = w.shape[1]
    tm = min(256, seq)
    mi = _mod_index(shift.shape[0], tm, seq)
    return pl.pallas_call(
        functools.partial(_inproj_kernel, nchunk=512),
        out_shape=jax.ShapeDtypeStruct((t, npad), F32),
        grid=(t // tm,),
        in_specs=[pl.BlockSpec((tm, d), lambda i: (i, 0)),
                  pl.BlockSpec((1, d), lambda i: (0, 0)),
                  pl.BlockSpec((None, 1, d), mi),
                  pl.BlockSpec((None, 1, d), mi),
                  pl.BlockSpec((d, npad), lambda i: (0, 0), pipeline_mode=pl.Buffered(1))],
        out_specs=pl.BlockSpec((tm, npad), lambda i: (i, 0)),
        compiler_params=_cparams(("arbitrary",)),
        name="inproj",
    )(x2d, g.reshape(1, d), shift, scale, w)


def _s5_kernel(u_ref, t_ref, wst_ref, wout_ref, dec_ref, h0_ref, y_ref, fin_ref, ef_ref, eb_ref, *, nc, bs):
    q = u_ref.shape[2]
    a = jnp.concatenate([u_ref[j] for j in range(S5_GROUP)], axis=-1).astype(BF16)
    loc = jnp.dot(a, wst_ref[...], preferred_element_type=F32)
    dec = dec_ref[...]
    half = S5_STATE

    def step(cur, da, db, add):
        return da * cur + db * pltpu.roll(cur, half, 1) + add

    cur = h0_ref[0]
    for c in range(nc):
        ef_ref[c * bs:(c + 1) * bs, :] = cur
        cur = step(cur, dec[0:1], dec[1:2], loc[c * bs:(c + 1) * bs, 0:LANE])
    fin_ref[0] = cur
    cur = h0_ref[1]
    for c in reversed(range(nc)):
        eb_ref[c * bs:(c + 1) * bs, :] = cur
        cur = step(cur, dec[2:3], dec[3:4], loc[c * bs:(c + 1) * bs, LANE:2 * LANE])
    fin_ref[1] = cur
    e = jnp.concatenate([ef_ref[...], eb_ref[...]], axis=-1).astype(BF16)
    acc = jnp.dot(a, t_ref[...], preferred_element_type=F32) + jnp.dot(e, wout_ref[...], preferred_element_type=F32)
    for i in range(S5_GROUP):
        y_ref[i] = acc[:, i * q:(i + 1) * q]


def _s5_scan(ut, s5w, h0, nc, bs):
    toep, wst, wout, dec = s5w
    groups = toep.shape[0]
    r, q = ut.shape[1], ut.shape[2]
    kq = S5_GROUP * q
    return pl.pallas_call(
        functools.partial(_s5_kernel, nc=nc, bs=bs),
        out_shape=(jax.ShapeDtypeStruct(ut.shape, F32), jax.ShapeDtypeStruct((groups, 2, bs, LANE), F32)),
        grid=(groups,),
        in_specs=[pl.BlockSpec((S5_GROUP, r, q), lambda g: (g, 0, 0)),
                  pl.BlockSpec((None, kq, kq), lambda g: (g, 0, 0)),
                  pl.BlockSpec((None, kq, 2 * LANE), lambda g: (g, 0, 0)),
                  pl.BlockSpec((None, 2 * LANE, kq), lambda g: (g, 0, 0)),
                  pl.BlockSpec((None, SUBLANE, LANE), lambda g: (g, 0, 0)),
                  pl.BlockSpec((None, 2, bs, LANE), lambda g: (g, 0, 0, 0))],
        out_specs=(pl.BlockSpec((S5_GROUP, r, q), lambda g: (g, 0, 0)),
                   pl.BlockSpec((None, 2, bs, LANE), lambda g: (g, 0, 0, 0))),
        scratch_shapes=[pltpu.VMEM((r, LANE), F32), pltpu.VMEM((r, LANE), F32)],
        compiler_params=_cparams(("arbitrary",)),
        name="s5_scan",
    )(ut, toep, wst, wout, dec, h0)


def _s5_weights(lam_re, lam_im, log_step, b_re, b_im, c_re, c_im, q):
    hp = lax.Precision.HIGHEST
    g, p = lam_re.shape[1], lam_re.shape[2]
    ii = S5_GROUP
    tau = jnp.arange(q + 1, dtype=F32)[:, None, None]
    ks, wsts, wouts, decs = [], [], [], []
    for d in range(2):
        lr, li = lam_re[d].astype(F32), lam_im[d].astype(F32)
        step = jnp.exp(log_step[d].astype(F32))[:, None]
        ar, ai = lr * step, li * step
        mag = jnp.exp(tau * ar)
        pre, pim = mag * jnp.cos(tau * ai), mag * jnp.sin(tau * ai)
        bar_re, bar_im = pre[1], pim[1]
        den = lr * lr + li * li
        nr, ni = bar_re - 1.0, bar_im
        coef_re = (nr * lr + ni * li) / den
        coef_im = (ni * lr - nr * li) / den
        bre, bim = b_re[d].astype(F32), b_im[d].astype(F32)
        bb_re = coef_re[..., None] * bre - coef_im[..., None] * bim
        bb_im = coef_re[..., None] * bim + coef_im[..., None] * bre
        cre, cim = c_re[d].astype(F32), c_im[d].astype(F32)
        cp_re = cre[None] * pre[:, :, None, :] - cim[None] * pim[:, :, None, :]
        cp_im = cre[None] * pim[:, :, None, :] + cim[None] * pre[:, :, None, :]
        k = (jnp.einsum('tgip,gpj->tgij', cp_re[:q], bb_re, precision=hp)
             - jnp.einsum('tgip,gpj->tgij', cp_im[:q], bb_im, precision=hp))
        ks.append(k)
        pw_re = pre[q - 1::-1][:q] if d == 0 else pre[:q]
        pw_im = pim[q - 1::-1][:q] if d == 0 else pim[:q]
        w_re = pw_re[:, :, :, None] * bb_re[None] - pw_im[:, :, :, None] * bb_im[None]
        w_im = pw_re[:, :, :, None] * bb_im[None] + pw_im[:, :, :, None] * bb_re[None]
        w = jnp.concatenate([w_re, w_im], axis=2)
        wsts.append(jnp.transpose(w, (1, 3, 0, 2)).reshape(g, ii * q, 2 * p))
        if d == 0:
            o_re, o_im = cp_re[1:q + 1], cp_im[1:q + 1]
        else:
            o_re, o_im = cp_re[q:0:-1], cp_im[q:0:-1]
        o = jnp.concatenate([o_re, -o_im], axis=3)
        wouts.append(jnp.transpose(o, (1, 3, 2, 0)).reshape(g, 2 * p, ii * q))
        dr, di = pre[q], pim[q]
        decs.append(jnp.concatenate([dr, dr], axis=-1))
        decs.append(jnp.concatenate([-di, di], axis=-1))
    tt = jnp.arange(q)
    lag = tt[None, :] - tt[:, None]
    kf = jnp.where((lag >= 0)[:, :, None, None, None], ks[0][jnp.clip(lag, 0, q - 1)], 0.0)
    kb = jnp.where((lag <= 0)[:, :, None, None, None], ks[1][jnp.clip(-lag, 0, q - 1)], 0.0)
    toep = jnp.transpose(kf + kb, (2, 4, 0, 3, 1)).reshape(g, ii * q, ii * q).astype(BF16)
    wst = jnp.concatenate(wsts, axis=-1).astype(BF16)
    wout = jnp.concatenate(wouts, axis=1).astype(BF16)
    dec = jnp.stack(decs + decs, axis=1)
    return toep, wst, wout, dec


def _s5_branch(proj, bs, seq, colmajor, s5w, h0, cols):
    p_u, s5w_width = cols
    groups = s5w_width // S5_GROUP
    q = min(S5_CHUNK, seq)
    nc = seq // q
    u = proj[:, p_u:p_u + s5w_width].reshape(bs, seq, s5w_width)
    if colmajor:
        rows = seq // GRID_W
        u = u.reshape(bs, rows, GRID_W, s5w_width).transpose(0, 2, 1, 3).reshape(bs, seq, s5w_width)
    ut = u.reshape(bs, nc, q, groups, S5_GROUP).transpose(3, 4, 1, 0, 2).reshape(s5w_width, nc * bs, q)
    yt, fin = _s5_scan(ut, s5w, h0, nc, bs)
    y = yt.reshape(groups, S5_GROUP, nc, bs, q).transpose(3, 2, 4, 0, 1).reshape(bs, seq, s5w_width)
    if colmajor:
        rows = seq // GRID_W
        y = y.reshape(bs, GRID_W, rows, s5w_width).transpose(0, 2, 1, 3).reshape(bs, seq, s5w_width)
    return y.reshape(bs * seq, s5w_width), fin


def _conv_rows(pad_ref, base, n, taps, bias):
    w = pad_ref[pl.ds(base, n + 2 * SUBLANE), :]
    tot = n + 2 * SUBLANE
    xm2 = pltpu.roll(w, 2, 0)[SUBLANE:SUBLANE + n]
    xm1 = pltpu.roll(w, 1, 0)[SUBLANE:SUBLANE + n]
    x0 = w[SUBLANE:SUBLANE + n]
    xp1 = pltpu.roll(w, tot - 1, 0)[SUBLANE:SUBLANE + n]
    return taps[0:1] * xm2 + taps[1:2] * xm1 + taps[2:3] * x0 + taps[3:4] * xp1 + bias


def _fill_padded(pad_ref, x_ref, seq):
    zeros = jnp.zeros((SUBLANE, pad_ref.shape[1]), F32)
    pad_ref[0:SUBLANE, :] = zeros
    pad_ref[SUBLANE + seq:2 * SUBLANE + seq, :] = zeros
    pad_ref[SUBLANE:SUBLANE + seq, :] = x_ref[...]


def _scan_tile(a, v, h, reverse):
    s = a.shape[0]
    row = lax.broadcasted_iota(I32, a.shape, 0)
    k = 1
    while k < s:
        sh = s - k if reverse else k
        ok = (row < s - k) if reverse else (row >= k)
        a_sh = pltpu.roll(a, sh, 0)
        v_sh = pltpu.roll(v, sh, 0)
        v = v + a * jnp.where(ok, v_sh, 0.0)
        a = a * jnp.where(ok, a_sh, 1.0)
        k *= 2
    return v + a * h


def _lru_kernel(x_ref, g_ref, w_ref, pp_ref, h0_ref, o_ref, fin_ref, pad_ref, a0_ref, v0_ref, a1_ref, v1_ref, *, seq, ch):
    pp = pp_ref[...]
    bias = pp[0:1]
    lam = pp[1:2]
    cb = pp[2:3, 0:LANE]
    taps = pp[3:7, 0:LANE]
    _fill_padded(pad_ref, x_ref, seq)
    av = ((a0_ref, v0_ref), (a1_ref, v1_ref))

    def gates(i, carry):
        base = pl.multiple_of(i * ch, ch)
        xc = _conv_rows(pad_ref, base, ch, taps, cb)
        gt = _dot(xc, w_ref[...]) + bias
        for d in range(2):
            r = jax.nn.sigmoid(gt[:, 2 * d * LANE:(2 * d + 1) * LANE])
            ig = jax.nn.sigmoid(gt[:, (2 * d + 1) * LANE:(2 * d + 2) * LANE])
            log_a = -LRU_C * r * _softplus(-lam[:, d * LANE:(d + 1) * LANE])
            av[d][0][pl.ds(base, ch), :] = jnp.exp(log_a)
            av[d][1][pl.ds(base, ch), :] = jnp.sqrt(jnp.maximum(1.0 - jnp.exp(2.0 * log_a), EPS)) * (ig * xc)
        return carry

    lax.fori_loop(0, seq // ch, gates, 0)
    tile = min(LRU_TILE, seq)
    nt = seq // tile

    def fwd(i, h):
        r0 = pl.multiple_of(i * tile, tile)
        hall = _scan_tile(a0_ref[pl.ds(r0, tile), :], v0_ref[pl.ds(r0, tile), :], h, False)
        o_ref[pl.ds(r0, tile), :] = hall
        return hall[tile - 1:tile, :]

    h0 = h0_ref[...]
    hf = lax.fori_loop(0, nt, fwd, h0[0:1])

    def bwd(i, h):
        r0 = pl.multiple_of((nt - 1 - i) * tile, tile)
        hall = _scan_tile(a1_ref[pl.ds(r0, tile), :], v1_ref[pl.ds(r0, tile), :], h, True)
        o_ref[pl.ds(r0, tile), :] = (o_ref[pl.ds(r0, tile), :] + hall) * jax.nn.gelu(g_ref[pl.ds(r0, tile), :])
        return hall[0:1, :]

    hb = lax.fori_loop(0, nt, bwd, h0[1:2])
    fin_ref[...] = jnp.concatenate([hf, hb], axis=0)


def _lru_branch(proj, bs, seq, lw, h0, cols):
    p_xl, p_gl, width = cols
    ncb = width // LANE
    bx, bg = p_xl // LANE, p_gl // LANE
    ch = min(256, seq)
    sc = pltpu.VMEM((seq, LANE), F32)
    hy, fin = pl.pallas_call(
        functools.partial(_lru_kernel, seq=seq, ch=ch),
        out_shape=(jax.ShapeDtypeStruct((bs * seq, width), F32), jax.ShapeDtypeStruct((bs, ncb, 2, LANE), F32)),
        grid=(bs, ncb),
        in_specs=[pl.BlockSpec((seq, LANE), lambda b, c: (b, bx + c)),
                  pl.BlockSpec((seq, LANE), lambda b, c: (b, bg + c)),
                  pl.BlockSpec((None, LANE, 4 * LANE), lambda b, c: (c, 0, 0)),
                  pl.BlockSpec((None, SUBLANE, 4 * LANE), lambda b, c: (c, 0, 0)),
                  pl.BlockSpec((None, None, 2, LANE), lambda b, c: (b, c, 0, 0))],
        out_specs=(pl.BlockSpec((seq, LANE), lambda b, c: (b, c)),
                   pl.BlockSpec((None, None, 2, LANE), lambda b, c: (b, c, 0, 0))),
        scratch_shapes=[pltpu.VMEM((seq + 2 * SUBLANE, LANE), F32), sc, sc, sc, sc],
        compiler_params=_cparams(("arbitrary", "arbitrary")),
        name="rglru",
    )(proj, proj, lw["lru_w"], lw["lru_pp"], h0)
    return hy, fin


def _ssd_kernel(xm_ref, bc_ref, dt_ref, cwx_ref, cwb_ref, pp_ref, h0_ref, y_ref, fin_ref,
                xpad_ref, bpad_ref, xa_ref, ba_ref, dtv_ref, la_ref, st_ref, *, seq):
    cq = M2_CHUNK
    nc = seq // cq
    hd = M2_HEAD_DIM
    npair = xm_ref.shape[1] // LANE
    pp = pp_ref[...]
    dt_bias, a_neg, dskip = pp[0:1], -jnp.exp(pp[1:2]), pp[2:3]
    cwx, cwb = cwx_ref[...], cwb_ref[...]
    _fill_padded(xpad_ref, xm_ref, seq)
    _fill_padded(bpad_ref, bc_ref, seq)

    def prep(i, carry):
        base = pl.multiple_of(i * cq, cq)
        xa_ref[pl.ds(base, cq), :] = _silu(_conv_rows(xpad_ref, base, cq, cwx[0:4], cwx[4:5]))
        ba_ref[pl.ds(base, cq), :] = _silu(_conv_rows(bpad_ref, base, cq, cwb[0:4], cwb[4:5]))
        dtv = _softplus(dt_ref[pl.ds(base, cq), :] + dt_bias)
        dtv_ref[pl.ds(base, cq), :] = dtv
        la_ref[pl.ds(base, cq), :] = dtv * a_neg
        return carry

    lax.fori_loop(0, nc, prep, 0)

    rowi = lax.broadcasted_iota(I32, (cq, cq), 0)
    coli = lax.broadcasted_iota(I32, (cq, cq), 1)
    tri = (coli <= rowi).astype(BF16)
    lane = lax.broadcasted_iota(I32, (cq, LANE), 1)
    lane1 = lax.broadcasted_iota(I32, (1, LANE), 1)
    first_half = lane < hd
    first_half1 = lane1 < hd

    def expand(c0, c1):
        return jnp.where(first_half, c0, c1)

    def chunk(c, d):
        r0 = pl.multiple_of(c * cq, cq)
        bc = ba_ref[pl.ds(r0, cq), :]
        dtv = dtv_ref[pl.ds(r0, cq), :]
        la = la_ref[pl.ds(r0, cq), :]
        cum = _dot_exact_lhs(tri, la)
        tot = cum[cq - 1:cq, :]
        ex = cum if d == 0 else cum - la
        ext = ex.T
        cm = jnp.where(first_half, 0.0, bc)
        bs_ = jnp.where(first_half, 0.0, pltpu.roll(bc, hd, 1))
        cb = _dot_nt(cm, bs_)
        bt = bc.T[0:M2_STATE, :]
        mask = (coli <= rowi) if d == 0 else (coli >= rowi)
        for pr in range(npair):
            xa = xa_ref[pl.ds(r0, cq), pr * LANE:(pr + 1) * LANE]
            cols = [d * 4 + 2 * pr, d * 4 + 2 * pr + 1]
            dt_e = expand(dtv[:, cols[0]:cols[0] + 1], dtv[:, cols[1]:cols[1] + 1])
            xs = xa * dt_e
            ydiag = jnp.zeros((cq, LANE), F32)
            for hh, col in enumerate(cols):
                ecol, erow = ex[:, col:col + 1], ext[col:col + 1, :]
                diff = (ecol - erow) if d == 0 else (erow - ecol)
                lm = jnp.where(mask, jnp.exp(jnp.minimum(diff, 0.0)), 0.0)
                xh = jnp.where(first_half if hh == 0 else ~first_half, xs, 0.0)
                ydiag = ydiag + _dot(cb * lm, xh)
            t0, t1 = tot[:, cols[0]:cols[0] + 1], tot[:, cols[1]:cols[1] + 1]
            if d == 0:
                din = expand(jnp.exp(t0 - ex[:, cols[0]:cols[0] + 1]), jnp.exp(t1 - ex[:, cols[1]:cols[1] + 1]))
                dout = expand(jnp.exp(ex[:, cols[0]:cols[0] + 1]), jnp.exp(ex[:, cols[1]:cols[1] + 1]))
            else:
                din = expand(jnp.exp(ex[:, cols[0]:cols[0] + 1]), jnp.exp(ex[:, cols[1]:cols[1] + 1]))
                dout = expand(jnp.exp(t0 - ex[:, cols[0]:cols[0] + 1]), jnp.exp(t1 - ex[:, cols[1]:cols[1] + 1]))
            sloc = _dot(bt, xs * din)
            sprev = st_ref[pr]
            yoff = _dot(cm, sprev) * dout
            dch = jnp.where(first_half1, jnp.exp(t0), jnp.exp(t1))
            st_ref[pr, M2_STATE:2 * M2_STATE, :] = dch * sprev[M2_STATE:2 * M2_STATE, :] + sloc
            lanes = slice(pr * LANE, (pr + 1) * LANE)
            if d == 0:
                dsk = jnp.where(first_half1, dskip[:, 2 * pr:2 * pr + 1], dskip[:, 2 * pr + 1:2 * pr + 2])
                y_ref[pl.ds(r0, cq), lanes] = ydiag + yoff + dsk * xa
            else:
                y_ref[pl.ds(r0, cq), lanes] = y_ref[pl.ds(r0, cq), lanes] + ydiag + yoff

    for d in range(2):
        st_ref[...] = jnp.zeros(st_ref.shape, F32)
        st_ref[:, M2_STATE:2 * M2_STATE, :] = h0_ref[d]
        if d == 0:
            lax.fori_loop(0, nc, lambda i, carry: (chunk(i, 0), carry)[1], 0)
        else:
            lax.fori_loop(0, nc, lambda i, carry: (chunk(nc - 1 - i, 1), carry)[1], 0)
        fin_ref[d] = st_ref[:, M2_STATE:2 * M2_STATE, :]


def _ssd_branch(proj, bs, seq, lw, h0, cols):
    p_xm, p_bc, p_dt, inner = cols
    gw = inner // M2_GROUPS
    npair = gw // LANE
    bxm, bbc, bdt = p_xm // gw, p_bc // LANE, p_dt // LANE
    y, fin = pl.pallas_call(
        functools.partial(_ssd_kernel, seq=seq),
        out_shape=(jax.ShapeDtypeStruct((bs * seq, inner), F32),
                   jax.ShapeDtypeStruct((bs, M2_GROUPS, 2, npair, M2_STATE, LANE), F32)),
        grid=(bs, M2_GROUPS),
        in_specs=[pl.BlockSpec((seq, gw), lambda b, g: (b, bxm + g)),
                  pl.BlockSpec((seq, LANE), lambda b, g: (b, bbc + g)),
                  pl.BlockSpec((seq, LANE), lambda b, g: (b, bdt + g)),
                  pl.BlockSpec((None, SUBLANE, gw), lambda b, g: (g, 0, 0)),
                  pl.BlockSpec((None, SUBLANE, LANE), lambda b, g: (g, 0, 0)),
                  pl.BlockSpec((None, SUBLANE, LANE), lambda b, g: (g, 0, 0)),
                  pl.BlockSpec((None, None, 2, npair, M2_STATE, LANE), lambda b, g: (b, g, 0, 0, 0, 0))],
        out_specs=(pl.BlockSpec((seq, gw), lambda b, g: (b, g)),
                   pl.BlockSpec((None, None, 2, npair, M2_STATE, LANE), lambda b, g: (b, g, 0, 0, 0, 0))),
        scratch_shapes=[pltpu.VMEM((seq + 2 * SUBLANE, gw), F32), pltpu.VMEM((seq + 2 * SUBLANE, LANE), F32),
                        pltpu.VMEM((seq, gw), F32), pltpu.VMEM((seq, LANE), F32),
                        pltpu.VMEM((seq, LANE), F32), pltpu.VMEM((seq, LANE), F32),
                        pltpu.VMEM((npair, 2 * M2_STATE, LANE), F32)],
        compiler_params=_cparams(("arbitrary", "arbitrary")),
        name="ssd",
    )(proj, proj, proj, lw["m2_cwx"], lw["m2_cwb"], lw["m2_pp"], h0)
    return y, fin


def _final_kernel(x_ref, y5_ref, u_ref, ys_ref, z_ref, hy_ref, g0_ref, g1_ref, g2_ref,
                  al_ref, sh_ref, sc_ref, d5_ref, wglu_ref, ng_ref, wm2_ref, wlru_ref, wo_ref, n2g_ref, wr_ref,
                  x1_ref, h2_ref, aff_ref):
    d = x_ref.shape[1]
    t5 = jax.nn.gelu(y5_ref[...] + d5_ref[...] * u_ref[...])
    vg = _dot(t5, wglu_ref[...])
    ya = vg[:, :d] * jax.nn.sigmoid(vg[:, d:])
    tb = _rms_scale(ys_ref[...] * _silu(z_ref[...])) * ng_ref[...]
    yb = _dot(tb, wm2_ref[...])
    yc = _dot(hy_ref[...], wlru_ref[...])
    merged = (jax.nn.sigmoid(g0_ref[...]) * ya + jax.nn.sigmoid(g1_ref[...]) * yb) + jax.nn.sigmoid(g2_ref[...]) * yc
    x1 = x_ref[...] + al_ref[...] * _dot(merged, wo_ref[...])
    x1_ref[...] = x1
    h2 = _rms_scale(x1) * n2g_ref[...] * (1.0 + sc_ref[...]) + sh_ref[...]
    h2_ref[...] = h2.astype(BF16)
    logits = _dot3(h2, wr_ref[...])
    valid = lax.broadcasted_iota(I32, logits.shape, 1) < N_EXPERTS
    logits = jnp.where(valid, logits, -jnp.inf)
    m = jnp.max(logits, axis=-1, keepdims=True)
    e = jnp.where(valid, jnp.exp(logits - m), 0.0)
    aff_ref[...] = e / jnp.sum(e, axis=-1, keepdims=True)


def _final(x2d, proj, y5, yssd, hy, seq, mods, lw, cols):
    t, d = x2d.shape
    tm = min(256, seq)
    alpha, shift2, scale2 = mods
    mi = _mod_index(alpha.shape[0], tm, seq)
    p_z, p_g, p_u, s5w = cols
    row = lambda w: pl.BlockSpec((tm, w), lambda i: (i, 0))
    pcol = lambda w, off: pl.BlockSpec((tm, w), lambda i: (i, off // w))
    full = lambda a: pl.BlockSpec(a.shape, lambda i: (0,) * a.ndim, pipeline_mode=pl.Buffered(1))
    mspec = pl.BlockSpec((None, 1, d), mi)
    weights = [lw["s5_d"], lw["s5_w_glu"], lw["m2_norm_g"], lw["m2_w_out"], lw["lru_w_out"], lw["w_o"],
               lw["norm2_g"], lw["w_router"]]
    return pl.pallas_call(
        _final_kernel,
        out_shape=(jax.ShapeDtypeStruct((t, d), F32), jax.ShapeDtypeStruct((t, d), BF16),
                   jax.ShapeDtypeStruct((t, LANE), F32)),
        grid=(t // tm,),
        in_specs=[row(d), row(s5w), pcol(s5w, p_u), row(d), pcol(d, p_z), row(hy.shape[1]),
                  pcol(d, p_g), pcol(d, p_g + d), pcol(d, p_g + 2 * d), mspec, mspec, mspec]
                 + [full(w) for w in weights],
        out_specs=(row(d), row(d), row(LANE)),
        compiler_params=_cparams(("arbitrary",)),
        name="merge_out",
    )(x2d, y5, proj, yssd, proj, hy, proj, proj, proj, alpha, shift2, scale2, *weights)


def _topk_kernel(aff_ref, slot_ref, *, cap):
    a = aff_ref[...]
    nblk, ne, _ = a.shape
    key = lax.bitcast_convert_type(a, I32)

    def count(m):
        return jnp.sum(jnp.sum(m.astype(I32), axis=0, keepdims=True), axis=2, keepdims=True)

    def body(i, lo):
        cand = lo | (jnp.int32(1) << (30 - i))
        return jnp.where(count(key >= cand) >= cap, cand, lo)

    kth = lax.fori_loop(0, 31, body, jnp.zeros((1, ne, 1), I32))
    gt = key > kth
    eq = key == kth
    need = cap - count(gt)
    rowi = lax.broadcasted_iota(I32, (LANE, LANE), 0)
    coli = lax.broadcasted_iota(I32, (LANE, LANE), 1)
    upper = (rowi <= coli).astype(BF16)

    def exclusive_rank(m):
        mf = m.astype(F32)
        incl = jnp.dot(mf.reshape(nblk * ne, LANE).astype(BF16), upper, preferred_element_type=F32).reshape(nblk, ne, LANE)
        offs, run = [], jnp.zeros((1, ne, 1), F32)
        for k in range(nblk):
            offs.append(run)
            run = run + incl[k:k + 1, :, LANE - 1:LANE]
        return (incl - mf + jnp.concatenate(offs, axis=0)).astype(I32)

    sel = gt | (eq & (exclusive_rank(eq) < need))
    slot_ref[...] = jnp.where(sel, exclusive_rank(sel), -1)


def _topk_slots(aff_t, cap):
    bs, nblk, ne, _ = aff_t.shape
    spec = pl.BlockSpec((None, nblk, ne, LANE), lambda b: (b, 0, 0, 0))
    return pl.pallas_call(
        functools.partial(_topk_kernel, cap=cap),
        out_shape=jax.ShapeDtypeStruct(aff_t.shape, I32),
        grid=(bs,),
        in_specs=[spec],
        out_specs=spec,
        compiler_params=_cparams(("arbitrary",)),
        name="route_topk",
    )(aff_t)


def _moe_ffn_kernel(h_ref, slot_ref, w1_ref, w3_ref, w2_ref, y_ref, p_ref, *, cap):
    nblk = slot_ref.shape[0]
    sidx = lax.broadcasted_iota(I32, (cap, LANE), 0)
    for k in range(nblk):
        p_ref[:, k * LANE:(k + 1) * LANE] = (slot_ref[k:k + 1, :] == sidx).astype(BF16)
    xs = jnp.dot(p_ref[...], h_ref[...], preferred_element_type=F32).astype(BF16)
    hid = _silu(jnp.dot(xs, w1_ref[...], preferred_element_type=F32)) * jnp.dot(xs, w3_ref[...], preferred_element_type=F32)
    y_ref[...] = jnp.dot(hid.astype(BF16), w2_ref[...], preferred_element_type=F32).astype(BF16)


def _moe_ffn(h2, slot_e, w1, w3, w2, cap):
    bs, n, d = h2.shape
    ne, nblk = slot_e.shape[1], slot_e.shape[2]
    wspec = lambda w: pl.BlockSpec((None,) + w.shape[1:], lambda e, b: (e, 0, 0))
    return pl.pallas_call(
        functools.partial(_moe_ffn_kernel, cap=cap),
        out_shape=jax.ShapeDtypeStruct((bs, ne, cap, d), BF16),
        grid=(ne, bs),
        in_specs=[pl.BlockSpec((None, n, d), lambda e, b: (b, 0, 0)),
                  pl.BlockSpec((None, None, nblk, LANE), lambda e, b: (b, e, 0, 0)),
                  wspec(w1), wspec(w3), wspec(w2)],
        out_specs=pl.BlockSpec((None, None, cap, d), lambda e, b: (b, e, 0, 0)),
        scratch_shapes=[pltpu.VMEM((cap, n), BF16)],
        compiler_params=_cparams(("arbitrary", "arbitrary")),
        name="moe_ffn",
    )(h2, slot_e, w1, w3, w2)


def _combine_kernel(x_ref, y_ref, slot_ref, aff_ref, al_ref, fg_ref, o_ref, *, cap, final_norm):
    slot = slot_ref[...]
    aff = aff_ref[...]
    tq = slot.shape[0]
    sidx = lax.broadcasted_iota(I32, (tq, cap), 1)
    acc = jnp.zeros(x_ref.shape, F32)
    for e in range(N_EXPERTS):
        pt = (slot[:, e:e + 1] == sidx).astype(BF16)
        acc = acc + aff[:, e:e + 1] * jnp.dot(pt, y_ref[e], preferred_element_type=F32)
    x2 = x_ref[...] + al_ref[...] * acc
    if final_norm:
        x2 = _rms_scale(x2) * fg_ref[...]
    o_ref[...] = x2


def _combine(x2d, y, slot_t, aff, seq, alpha, final_g, final_norm):
    t, d = x2d.shape
    bs, ne, cap, _ = y.shape
    tq = min(512, seq)
    nq = seq // tq
    mi = (lambda i: (0, 0, 0)) if alpha.shape[0] == 1 else (lambda i: (i // nq, 0, 0))
    return pl.pallas_call(
        functools.partial(_combine_kernel, cap=cap, final_norm=final_norm),
        out_shape=jax.ShapeDtypeStruct((t, d), F32),
        grid=(t // tq,),
        in_specs=[pl.BlockSpec((tq, d), lambda i: (i, 0)),
                  pl.BlockSpec((None, ne, cap, d), lambda i: (i // nq, 0, 0, 0)),
                  pl.BlockSpec((tq, LANE), lambda i: (i, 0)),
                  pl.BlockSpec((tq, LANE), lambda i: (i, 0)),
                  pl.BlockSpec((None, 1, d), mi),
                  pl.BlockSpec((1, d), lambda i: (0, 0))],
        out_specs=pl.BlockSpec((tq, d), lambda i: (i, 0)),
        compiler_params=_cparams(("arbitrary",)),
        name="moe_combine",
    )(x2d, y, slot_t, aff, alpha, final_g.reshape(1, d))


def _moe(x1, h2, aff, bs, seq, alpha, lw, final_g, final_norm):
    d = x1.shape[1]
    cap = CAPACITY * seq // N_EXPERTS
    nblk = seq // LANE
    aff_t = aff[:, :N_EXPERTS].reshape(bs, nblk, LANE, N_EXPERTS).transpose(0, 1, 3, 2)
    slot = _topk_slots(aff_t, cap)
    slot_e = slot.transpose(0, 2, 1, 3)
    slot_t = slot.transpose(0, 1, 3, 2).reshape(bs * seq, N_EXPERTS)
    slot_t = jnp.pad(slot_t, ((0, 0), (0, LANE - N_EXPERTS)), constant_values=-1)
    y = _moe_ffn(h2.reshape(bs, seq, d), slot_e, lw["moe_w1"], lw["moe_w3"], lw["moe_w2"], cap)
    return _combine(x1, y, slot_t, aff, seq, alpha, final_g, final_norm)


def _layout(d):
    s5w, inner, lruw = d // 2, d, d // 2
    gn = M2_GROUPS * M2_STATE
    heads = inner // M2_HEAD_DIM
    o_u = 0
    o_z = o_u + s5w
    o_xbc = o_z + inner
    o_dt = o_xbc + inner + 2 * gn
    o_xl = o_dt + 2 * heads
    o_gl = o_xl + lruw
    o_g = o_gl + lruw
    d_in = o_g + N_BRANCH * d
    p_z = 0
    p_g = p_z + inner
    p_xm = p_g + N_BRANCH * d
    p_bc = p_xm + inner
    p_u = p_bc + M2_GROUPS * LANE
    p_xl = p_u + s5w
    p_gl = p_xl + lruw
    p_dt = p_gl + lruw
    npad = p_dt + M2_GROUPS * LANE
    hpg = heads // M2_GROUPS
    assert hpg == 4 and M2_STATE == 64 and M2_HEAD_DIM == 64, "SSD kernel packs two 64-wide heads per lane tile"
    perm = np.full((npad,), d_in, np.int32)
    perm[p_z:p_z + inner] = o_z + np.arange(inner)
    perm[p_g:p_g + N_BRANCH * d] = o_g + np.arange(N_BRANCH * d)
    perm[p_xm:p_xm + inner] = o_xbc + np.arange(inner)
    bcp = np.zeros((M2_GROUPS * LANE,), np.int32)
    for g in range(M2_GROUPS):
        bcp[g * LANE:g * LANE + M2_STATE] = inner + g * M2_STATE + np.arange(M2_STATE)
        bcp[g * LANE + M2_STATE:(g + 1) * LANE] = inner + gn + g * M2_STATE + np.arange(M2_STATE)
    perm[p_bc:p_bc + M2_GROUPS * LANE] = o_xbc + bcp
    perm[p_u:p_u + s5w] = o_u + np.arange(s5w)
    perm[p_xl:p_xl + lruw] = o_xl + np.arange(lruw)
    perm[p_gl:p_gl + lruw] = o_gl + np.arange(lruw)
    for g in range(M2_GROUPS):
        for dd in range(2):
            for j in range(hpg):
                perm[p_dt + g * LANE + dd * hpg + j] = o_dt + dd * heads + g * hpg + j
    return dict(s5w=s5w, inner=inner, lruw=lruw, heads=heads, hpg=hpg, perm=perm, bcp=bcp, npad=npad,
                p_z=p_z, p_g=p_g, p_xm=p_xm, p_bc=p_bc, p_u=p_u, p_xl=p_xl, p_gl=p_gl, p_dt=p_dt)


def _rows8(rows, width):
    out = jnp.zeros((SUBLANE, width), F32)
    for i, r in enumerate(rows):
        out = out.at[i, :r.shape[0]].set(r.astype(F32))
    return out


def _pack_layer(lay, p, q_lat, q_ctx):
    d = p["w_in"].shape[0]
    inner, lruw, hpg, heads = lay["inner"], lay["lruw"], lay["hpg"], lay["heads"]
    lw = {}
    w_ext = jnp.concatenate([p["w_in"], jnp.zeros((d, 1), p["w_in"].dtype)], axis=1)
    lw["w_in"] = w_ext[:, lay["perm"]].astype(BF16)
    lw["s5_lat"] = _s5_weights(p["s5_lam_re"], p["s5_lam_im"], p["s5_log_step"], p["s5_b_re"], p["s5_b_im"],
                               p["s5_c_re"], p["s5_c_im"], q_lat)
    lw["s5_ctx"] = lw["s5_lat"] if q_ctx == q_lat else _s5_weights(
        p["s5_lam_re"], p["s5_lam_im"], p["s5_log_step"], p["s5_b_re"], p["s5_b_im"], p["s5_c_re"], p["s5_c_im"], q_ctx)
    gw = inner // M2_GROUPS
    cw, cb = p["m2_conv_w"], p["m2_conv_b"]
    lw["m2_cwx"] = jnp.stack([_rows8([cw[k, g * gw:(g + 1) * gw] for k in range(CONV_W)] + [cb[g * gw:(g + 1) * gw]], gw)
                              for g in range(M2_GROUPS)])
    cwb, cbb = cw[:, lay["bcp"]], cb[lay["bcp"]]
    lw["m2_cwb"] = jnp.stack([_rows8([cwb[k, g * LANE:(g + 1) * LANE] for k in range(CONV_W)] + [cbb[g * LANE:(g + 1) * LANE]], LANE)
                              for g in range(M2_GROUPS)])
    dtb = p["m2_dt_bias"].reshape(2, M2_GROUPS, hpg)
    alog = p["m2_a_log"].reshape(2, M2_GROUPS, hpg)
    dsk = p["m2_d"].reshape(M2_GROUPS, hpg)
    lw["m2_pp"] = jnp.stack([_rows8([dtb[:, g].reshape(-1), alog[:, g].reshape(-1), dsk[g]], LANE) for g in range(M2_GROUPS)])
    blk = lruw // LRU_BLOCKS
    eye = jnp.eye(LRU_BLOCKS, dtype=F32)

    def dense(w):
        return jnp.einsum('hij,hk->hikj', w.astype(F32), eye).reshape(lruw, lruw)

    ncb = lruw // LANE
    assert LANE % blk == 0
    mats = [dense(p["lru_w_a"][0]), dense(p["lru_w_x"][0]), dense(p["lru_w_a"][1]), dense(p["lru_w_x"][1])]
    lw["lru_w"] = jnp.stack([jnp.concatenate([m[c * LANE:(c + 1) * LANE, c * LANE:(c + 1) * LANE] for m in mats], axis=1)
                             for c in range(ncb)]).astype(BF16)
    sl = lambda v, c: v[c * LANE:(c + 1) * LANE]
    pps = []
    for c in range(ncb):
        bias = jnp.concatenate([sl(p["lru_b_a"][0], c), sl(p["lru_b_x"][0], c), sl(p["lru_b_a"][1], c), sl(p["lru_b_x"][1], c)])
        lam = jnp.concatenate([sl(p["lru_lam"][0], c), sl(p["lru_lam"][1], c)])
        pps.append(_rows8([bias, lam, sl(p["lru_conv_b"], c)] + [sl(p["lru_conv_w"][k], c) for k in range(CONV_W)], 4 * LANE))
    lw["lru_pp"] = jnp.stack(pps)
    lw["s5_d"] = p["s5_d"].reshape(1, -1).astype(F32)
    lw["s5_w_glu"] = p["s5_w_glu"].astype(BF16)
    lw["m2_norm_g"] = p["m2_norm_g"].reshape(1, -1).astype(F32)
    lw["m2_w_out"] = p["m2_w_out"].astype(BF16)
    lw["lru_w_out"] = p["lru_w_out"].astype(BF16)
    lw["w_o"] = p["w_o"].astype(BF16)
    lw["norm2_g"] = p["norm2_g"].reshape(1, -1).astype(F32)
    lw["w_router"] = jnp.pad(p["moe_w_router"].astype(F32), ((0, 0), (0, LANE - N_EXPERTS)))
    lw["moe_w1"] = p["moe_w1"].astype(BF16)
    lw["moe_w3"] = p["moe_w3"].astype(BF16)
    lw["moe_w2"] = p["moe_w2"].astype(BF16)
    return lw


def _mixer(x2d, bs, seq, colmajor, norm_g, shift, scale, lw, lay, h0, s5w):
    proj = _inproj(x2d, seq, norm_g, shift, scale, lw["w_in"])
    y5, f5 = _s5_branch(proj, bs, seq, colmajor, s5w, h0[0], (lay["p_u"], lay["s5w"]))
    ys, fm = _ssd_branch(proj, bs, seq, lw, h0[1], (lay["p_xm"], lay["p_bc"], lay["p_dt"], lay["inner"]))
    hy, fl = _lru_branch(proj, bs, seq, lw, h0[2], (lay["p_xl"], lay["p_gl"], lay["lruw"]))
    return proj, y5, ys, hy, (f5, fm, fl)


def kernel(x, c, ctx, c_ctx, w_mod, b_mod, norm1_g, norm2_g, w_in, s5_lam_re, s5_lam_im, s5_log_step, s5_b_re, s5_b_im, s5_c_re, s5_c_im, s5_d, s5_w_glu, m2_conv_w, m2_conv_b, m2_dt_bias, m2_a_log, m2_d, m2_norm_g, m2_w_out, lru_conv_w, lru_conv_b, lru_w_a, lru_b_a, lru_w_x, lru_b_x, lru_lam, lru_w_out, w_o, moe_w_router, moe_w1, moe_w3, moe_w2, final_norm_g):
    bsz, seq, d = x.shape
    cl = ctx.shape[1]
    depth = w_mod.shape[0]
    lay = _layout(d)
    stacked = dict(norm2_g=norm2_g, w_in=w_in, s5_lam_re=s5_lam_re, s5_lam_im=s5_lam_im, s5_log_step=s5_log_step,
                   s5_b_re=s5_b_re, s5_b_im=s5_b_im, s5_c_re=s5_c_re, s5_c_im=s5_c_im, s5_d=s5_d, s5_w_glu=s5_w_glu,
                   m2_conv_w=m2_conv_w, m2_conv_b=m2_conv_b, m2_dt_bias=m2_dt_bias, m2_a_log=m2_a_log, m2_d=m2_d,
                   m2_norm_g=m2_norm_g, m2_w_out=m2_w_out, lru_conv_w=lru_conv_w, lru_conv_b=lru_conv_b,
                   lru_w_a=lru_w_a, lru_b_a=lru_b_a, lru_w_x=lru_w_x, lru_b_x=lru_b_x, lru_lam=lru_lam,
                   lru_w_out=lru_w_out, w_o=w_o, moe_w_router=moe_w_router, moe_w1=moe_w1, moe_w3=moe_w3, moe_w2=moe_w2)
    rm = -(-(bsz + 1) // SUBLANE) * SUBLANE
    c_rows = jnp.zeros((rm, d), F32).at[:bsz].set(c.astype(F32)).at[bsz].set(c_ctx.astype(F32))
    mods = _modulation(c_rows, w_mod.astype(F32), b_mod.astype(F32))

    groups = lay["s5w"] // S5_GROUP
    ncb = lay["lruw"] // LANE
    npair = lay["inner"] // M2_GROUPS // LANE
    zero_h0 = (jnp.zeros((groups, 2, bsz, LANE), F32),
               jnp.zeros((bsz, M2_GROUPS, 2, npair, M2_STATE, LANE), F32),
               jnp.zeros((bsz, ncb, 2, LANE), F32))

    xs = x.reshape(bsz * seq, d).astype(F32)
    cs = ctx.reshape(bsz * cl, d).astype(F32)
    q_lat, q_ctx = min(S5_CHUNK, seq), min(S5_CHUNK, cl)
    for i in range(depth):
        p = {k: v[i] for k, v in stacked.items()}
        lw = _pack_layer(lay, p, q_lat, q_ctx)
        mx = [mods[i, :bsz, k * d:(k + 1) * d].reshape(bsz, 1, d) for k in range(6)]
        mc = [mods[i, bsz:bsz + 1, k * d:(k + 1) * d].reshape(1, 1, d) for k in range(6)]
        fcols = (lay["p_z"], lay["p_g"], lay["p_u"], lay["s5w"])
        cproj, cy5, cys, chy, cstates = _mixer(cs, bsz, cl, False, norm1_g[i], mc[0], mc[1], lw, lay, zero_h0, lw["s5_ctx"])
        if i < depth - 1:
            c1, ch2, caff = _final(cs, cproj, cy5, cys, chy, cl, (mc[2], mc[3], mc[4]), lw, fcols)
            cs = _moe(c1, ch2, caff, bsz, cl, mc[5], lw, final_norm_g, False)
        xproj, y5, ys, hy, _ = _mixer(xs, bsz, seq, True, norm1_g[i], mx[0], mx[1], lw, lay, cstates, lw["s5_lat"])
        x1, h2, aff = _final(xs, xproj, y5, ys, hy, seq, (mx[2], mx[3], mx[4]), lw, fcols)
        xs = _moe(x1, h2, aff, bsz, seq, mx[5], lw, final_norm_g, i == depth - 1)
    return xs.reshape(bsz, seq, d).astype(x.dtype)
```

```python
import functools
import math

import numpy as np
import jax
import jax.numpy as jnp
from jax import lax
from jax.experimental import pallas as pl
from jax.experimental.pallas import tpu as pltpu

F32 = jnp.float32
BF16 = jnp.bfloat16
I32 = jnp.int32

GRID_W = 64
EPS = 1e-6
CONV_W = 4
S5_GROUP = 16
S5_STATE = 64
M2_HEAD_DIM = 64
M2_GROUPS = 4
M2_STATE = 64
M2_CHUNK = 128
LRU_BLOCKS = 8
LRU_C = 8.0
N_EXPERTS = 16
CAPACITY = 2
N_BRANCH = 3

LANE = 128
SUBLANE = 8
S5_CHUNK = 128
LRU_TILE = 64
VMEM_LIMIT = 56 * 1024 * 1024


def _cparams(sem):
    return pltpu.CompilerParams(dimension_semantics=sem, vmem_limit_bytes=VMEM_LIMIT)


def _dot(a, b):
    return jnp.dot(a.astype(BF16), b.astype(BF16), preferred_element_type=F32)


def _dot_nt(a, b):
    return lax.dot_general(a.astype(BF16), b.astype(BF16), (((1,), (1,)), ((), ())), preferred_element_type=F32)


def _split2(a):
    hi = a.astype(BF16)
    lo = (a - hi.astype(F32)).astype(BF16)
    return hi, lo


def _split3(a):
    hi = a.astype(BF16)
    r = a - hi.astype(F32)
    mid = r.astype(BF16)
    lo = (r - mid.astype(F32)).astype(BF16)
    return hi, mid, lo


def _dot3(a, b):
    ah, al = _split2(a)
    bh, bl = _split2(b)
    d = lambda x, y: jnp.dot(x, y, preferred_element_type=F32)
    return d(ah, bh) + (d(ah, bl) + d(al, bh))


def _dot_exact_lhs(m01, x):
    hi, mid, lo = _split3(x)
    d = lambda y: jnp.dot(m01, y, preferred_element_type=F32)
    return d(hi) + (d(mid) + d(lo))


def _silu(x):
    return x * jax.nn.sigmoid(x)


def _softplus(x):
    return jnp.maximum(x, 0.0) + jnp.log(1.0 + jnp.exp(-jnp.abs(x)))


def _rms_scale(x):
    return x * lax.rsqrt(jnp.mean(x * x, axis=-1, keepdims=True) + EPS)


def _mod_kernel(c_ref, w_ref, b_ref, o_ref):
    o_ref[...] = _dot3(_silu(c_ref[...]), w_ref[...]) + b_ref[...]


def _modulation(c_rows, w_mod, b_mod):
    depth, d, n6 = w_mod.shape
    rm = c_rows.shape[0]
    tn = min(1024, n6)
    return pl.pallas_call(
        _mod_kernel,
        out_shape=jax.ShapeDtypeStruct((depth, rm, n6), F32),
        grid=(depth, n6 // tn),
        in_specs=[pl.BlockSpec((rm, d), lambda l, j: (0, 0)),
                  pl.BlockSpec((None, d, tn), lambda l, j: (l, 0, j)),
                  pl.BlockSpec((None, 1, tn), lambda l, j: (l, 0, j))],
        out_specs=pl.BlockSpec((None, rm, tn), lambda l, j: (l, 0, j)),
        compiler_params=_cparams(("arbitrary", "arbitrary")),
        name="modulation",
    )(c_rows, w_mod, b_mod.reshape(depth, 1, n6))


def _inproj_kernel(x_ref, g_ref, sh_ref, sc_ref, w_ref, o_ref, *, nchunk):
    h = _rms_scale(x_ref[...]) * g_ref[...] * (1.0 + sc_ref[...]) + sh_ref[...]
    hb = h.astype(BF16)
    npad = o_ref.shape[1]
    for n0 in range(0, npad, nchunk):
        o_ref[:, n0:n0 + nchunk] = jnp.dot(hb, w_ref[:, n0:n0 + nchunk], preferred_element_type=F32)


def _mod_index(bm, tm, seq):
    if bm == 1:
        return lambda i: (0, 0, 0)
    return lambda i: ((i * tm) // seq, 0, 0)


def _inproj(x2d, seq, g, shift, scale, w):
    t, d = x2d.shape
    npad = w.shape[1]
    tm = min(256, seq)
    mi = _mod_index(shift.shape[0], tm, seq)
    return pl.pallas_call(
        functools.partial(_inproj_kernel, nchunk=512),
        out_shape=jax.ShapeDtypeStruct((t, npad), F32),
        grid=(t // tm,),
        in_specs=[pl.BlockSpec((tm, d), lambda i: (i, 0)),
                  pl.BlockSpec((1, d), lambda i: (0, 0)),
                  pl.BlockSpec((None, 1, d), mi),
                  pl.BlockSpec((None, 1, d), mi),
                  pl.BlockSpec((d, npad), lambda i: (0, 0), pipeline_mode=pl.Buffered(1))],
        out_specs=pl.BlockSpec((tm, npad), lambda i: (i, 0)),
        compiler_params=_cparams(("arbitrary",)),
        name="inproj",
    )(x2d, g.reshape(1, d), shift, scale, w)


def _build_toeplitz(cp_ref, cn_ref, t_ref, q):
    trow = lax.broadcasted_iota(I32, (q, q), 0)
    tcol = lax.broadcasted_iota(I32, (q, q), 1)
    causal = tcol >= trow

    def body(j, carry):
        cp = cp_ref[j]
        cn = cn_ref[j]
        r0 = pl.multiple_of(j * q, q)
        for i in range(S5_GROUP):
            a = pltpu.roll(jnp.broadcast_to(cp[i:i + 1, :], (q, q)), 0, 1, stride=1, stride_axis=0)
            b = pltpu.roll(jnp.broadcast_to(cn[i:i + 1, :], (q, q)), 0, 1, stride=1, stride_axis=0)
            t_ref[pl.ds(r0, q), i * q:(i + 1) * q] = jnp.where(causal, a, b).astype(BF16)
        return carry

    lax.fori_loop(0, S5_GROUP, body, 0)


def _s5_kernel(u_ref, cp_ref, cn_ref, wst_ref, wout_ref, dec_ref, h0_ref, y_ref, fin_ref, ef_ref, eb_ref, t_ref, *, nc, bs):
    q = u_ref.shape[2]
    _build_toeplitz(cp_ref, cn_ref, t_ref, q)
    a = jnp.concatenate([u_ref[j] for j in range(S5_GROUP)], axis=-1).astype(BF16)
    loc = jnp.dot(a, wst_ref[...], preferred_element_type=F32)
    dec = dec_ref[...]
    half = S5_STATE

    def step(cur, da, db, add):
        return da * cur + db * pltpu.roll(cur, half, 1) + add

    cur = h0_ref[0]
    for c in range(nc):
        ef_ref[c * bs:(c + 1) * bs, :] = cur
        cur = step(cur, dec[0:1], dec[1:2], loc[c * bs:(c + 1) * bs, 0:LANE])
    fin_ref[0] = cur
    cur = h0_ref[1]
    for c in reversed(range(nc)):
        eb_ref[c * bs:(c + 1) * bs, :] = cur
        cur = step(cur, dec[2:3], dec[3:4], loc[c * bs:(c + 1) * bs, LANE:2 * LANE])
    fin_ref[1] = cur
    e = jnp.concatenate([ef_ref[...], eb_ref[...]], axis=-1).astype(BF16)
    acc = jnp.dot(a, t_ref[...], preferred_element_type=F32) + jnp.dot(e, wout_ref[...], preferred_element_type=F32)
    for i in range(S5_GROUP):
        y_ref[i] = acc[:, i * q:(i + 1) * q]


def _s5_scan(ut, s5w, h0, nc, bs):
    cpos, cneg, wst, wout, dec = s5w
    groups = cpos.shape[0]
    r, q = ut.shape[1], ut.shape[2]
    assert q == LANE, "the Toeplitz builder rotates one 128-lane tile per block"
    kq = S5_GROUP * q
    lagspec = pl.BlockSpec((None, S5_GROUP, S5_GROUP, q), lambda g: (g, 0, 0, 0))
    return pl.pallas_call(
        functools.partial(_s5_kernel, nc=nc, bs=bs),
        out_shape=(jax.ShapeDtypeStruct(ut.shape, F32), jax.ShapeDtypeStruct((groups, 2, bs, LANE), F32)),
        grid=(groups,),
        in_specs=[pl.BlockSpec((S5_GROUP, r, q), lambda g: (g, 0, 0)),
                  lagspec, lagspec,
                  pl.BlockSpec((None, kq, 2 * LANE), lambda g: (g, 0, 0)),
                  pl.BlockSpec((None, 2 * LANE, kq), lambda g: (g, 0, 0)),
                  pl.BlockSpec((None, SUBLANE, LANE), lambda g: (g, 0, 0)),
                  pl.BlockSpec((None, 2, bs, LANE), lambda g: (g, 0, 0, 0))],
        out_specs=(pl.BlockSpec((S5_GROUP, r, q), lambda g: (g, 0, 0)),
                   pl.BlockSpec((None, 2, bs, LANE), lambda g: (g, 0, 0, 0))),
        scratch_shapes=[pltpu.VMEM((r, LANE), F32), pltpu.VMEM((r, LANE), F32), pltpu.VMEM((kq, kq), BF16)],
        compiler_params=_cparams(("arbitrary",)),
        name="s5_scan",
    )(ut, cpos, cneg, wst, wout, dec, h0)


def _s5_weights(lam_re, lam_im, log_step, b_re, b_im, c_re, c_im, q):
    hp = lax.Precision.HIGHEST
    g, p = lam_re.shape[1], lam_re.shape[2]
    ii = S5_GROUP
    tau = jnp.arange(q + 1, dtype=F32)[:, None, None]
    ks, wsts, wouts, decs = [], [], [], []
    for d in range(2):
        lr, li = lam_re[d].astype(F32), lam_im[d].astype(F32)
        step = jnp.exp(log_step[d].astype(F32))[:, None]
        ar, ai = lr * step, li * step
        mag = jnp.exp(tau * ar)
        pre, pim = mag * jnp.cos(tau * ai), mag * jnp.sin(tau * ai)
        bar_re, bar_im = pre[1], pim[1]
        den = lr * lr + li * li
        nr, ni = bar_re - 1.0, bar_im
        coef_re = (nr * lr + ni * li) / den
        coef_im = (ni * lr - nr * li) / den
        bre, bim = b_re[d].astype(F32), b_im[d].astype(F32)
        bb_re = coef_re[..., None] * bre - coef_im[..., None] * bim
        bb_im = coef_re[..., None] * bim + coef_im[..., None] * bre
        cre, cim = c_re[d].astype(F32), c_im[d].astype(F32)
        cp_re = cre[None] * pre[:, :, None, :] - cim[None] * pim[:, :, None, :]
        cp_im = cre[None] * pim[:, :, None, :] + cim[None] * pre[:, :, None, :]
        k = (jnp.einsum('tgip,gpj->tgij', cp_re[:q], bb_re, precision=hp)
             - jnp.einsum('tgip,gpj->tgij', cp_im[:q], bb_im, precision=hp))
        ks.append(k)
        pw_re = pre[q - 1::-1][:q] if d == 0 else pre[:q]
        pw_im = pim[q - 1::-1][:q] if d == 0 else pim[:q]
        w_re = pw_re[:, :, :, None] * bb_re[None] - pw_im[:, :, :, None] * bb_im[None]
        w_im = pw_re[:, :, :, None] * bb_im[None] + pw_im[:, :, :, None] * bb_re[None]
        w = jnp.concatenate([w_re, w_im], axis=2)
        wsts.append(jnp.transpose(w, (1, 3, 0, 2)).reshape(g, ii * q, 2 * p))
        if d == 0:
            o_re, o_im = cp_re[1:q + 1], cp_im[1:q + 1]
        else:
            o_re, o_im = cp_re[q:0:-1], cp_im[q:0:-1]
        o = jnp.concatenate([o_re, -o_im], axis=3)
        wouts.append(jnp.transpose(o, (1, 3, 2, 0)).reshape(g, 2 * p, ii * q))
        dr, di = pre[q], pim[q]
        decs.append(jnp.concatenate([dr, dr], axis=-1))
        decs.append(jnp.concatenate([-di, di], axis=-1))
    cpos = jnp.transpose(ks[0].at[0].add(ks[1][0]), (1, 3, 2, 0))
    kb_rev = ks[1][::-1]
    cneg = jnp.concatenate([jnp.zeros_like(kb_rev[:1]), kb_rev[:q - 1]], axis=0)
    cneg = jnp.transpose(cneg, (1, 3, 2, 0))
    wst = jnp.concatenate(wsts, axis=-1).astype(BF16)
    wout = jnp.concatenate(wouts, axis=1).astype(BF16)
    dec = jnp.stack(decs + decs, axis=1)
    return cpos, cneg, wst, wout, dec


def _s5_branch(proj, bs, seq, colmajor, s5w, h0, cols):
    p_u, s5w_width = cols
    groups = s5w_width // S5_GROUP
    q = min(S5_CHUNK, seq)
    nc = seq // q
    u = proj[:, p_u:p_u + s5w_width].reshape(bs, seq, s5w_width)
    if colmajor:
        rows = seq // GRID_W
        u = u.reshape(bs, rows, GRID_W, s5w_width).transpose(0, 2, 1, 3).reshape(bs, seq, s5w_width)
    ut = u.reshape(bs, nc, q, groups, S5_GROUP).transpose(3, 4, 1, 0, 2).reshape(s5w_width, nc * bs, q)
    yt, fin = _s5_scan(ut, s5w, h0, nc, bs)
    y = yt.reshape(groups, S5_GROUP, nc, bs, q).transpose(3, 2, 4, 0, 1).reshape(bs, seq, s5w_width)
    if colmajor:
        rows = seq // GRID_W
        y = y.reshape(bs, GRID_W, rows, s5w_width).transpose(0, 2, 1, 3).reshape(bs, seq, s5w_width)
    return y.reshape(bs * seq, s5w_width), fin


def _conv_rows(pad_ref, base, n, taps, bias):
    w = pad_ref[pl.ds(base, n + 2 * SUBLANE), :]
    tot = n + 2 * SUBLANE
    xm2 = pltpu.roll(w, 2, 0)[SUBLANE:SUBLANE + n]
    xm1 = pltpu.roll(w, 1, 0)[SUBLANE:SUBLANE + n]
    x0 = w[SUBLANE:SUBLANE + n]
    xp1 = pltpu.roll(w, tot - 1, 0)[SUBLANE:SUBLANE + n]
    return taps[0:1] * xm2 + taps[1:2] * xm1 + taps[2:3] * x0 + taps[3:4] * xp1 + bias


def _fill_padded(pad_ref, x_ref, seq):
    zeros = jnp.zeros((SUBLANE, pad_ref.shape[1]), F32)
    pad_ref[0:SUBLANE, :] = zeros
    pad_ref[SUBLANE + seq:2 * SUBLANE + seq, :] = zeros
    pad_ref[SUBLANE:SUBLANE + seq, :] = x_ref[...]


def _scan_tile(a, v, h, reverse):
    s = a.shape[0]
    row = lax.broadcasted_iota(I32, a.shape, 0)
    k = 1
    while k < s:
        sh = s - k if reverse else k
        ok = (row < s - k) if reverse else (row >= k)
        a_sh = pltpu.roll(a, sh, 0)
        v_sh = pltpu.roll(v, sh, 0)
        v = v + a * jnp.where(ok, v_sh, 0.0)
        a = a * jnp.where(ok, a_sh, 1.0)
        k *= 2
    return v + a * h


def _lru_kernel(x_ref, g_ref, w_ref, pp_ref, h0_ref, o_ref, fin_ref, pad_ref, a0_ref, v0_ref, a1_ref, v1_ref, *, seq, ch):
    pp = pp_ref[...]
    bias = pp[0:1]
    lam = pp[1:2]
    cb = pp[2:3, 0:LANE]
    taps = pp[3:7, 0:LANE]
    _fill_padded(pad_ref, x_ref, seq)
    av = ((a0_ref, v0_ref), (a1_ref, v1_ref))

    def gates(i, carry):
        base = pl.multiple_of(i * ch, ch)
        xc = _conv_rows(pad_ref, base, ch, taps, cb)
        gt = _dot(xc, w_ref[...]) + bias
        for d in range(2):
            r = jax.nn.sigmoid(gt[:, 2 * d * LANE:(2 * d + 1) * LANE])
            ig = jax.nn.sigmoid(gt[:, (2 * d + 1) * LANE:(2 * d + 2) * LANE])
            log_a = -LRU_C * r * _softplus(-lam[:, d * LANE:(d + 1) * LANE])
            av[d][0][pl.ds(base, ch), :] = jnp.exp(log_a)
            av[d][1][pl.ds(base, ch), :] = jnp.sqrt(jnp.maximum(1.0 - jnp.exp(2.0 * log_a), EPS)) * (ig * xc)
        return carry

    lax.fori_loop(0, seq // ch, gates, 0)
    tile = min(LRU_TILE, seq)
    nt = seq // tile

    def fwd(i, h):
        r0 = pl.multiple_of(i * tile, tile)
        hall = _scan_tile(a0_ref[pl.ds(r0, tile), :], v0_ref[pl.ds(r0, tile), :], h, False)
        o_ref[pl.ds(r0, tile), :] = hall
        return hall[tile - 1:tile, :]

    h0 = h0_ref[...]
    hf = lax.fori_loop(0, nt, fwd, h0[0:1])

    def bwd(i, h):
        r0 = pl.multiple_of((nt - 1 - i) * tile, tile)
        hall = _scan_tile(a1_ref[pl.ds(r0, tile), :], v1_ref[pl.ds(r0, tile), :], h, True)
        o_ref[pl.ds(r0, tile), :] = (o_ref[pl.ds(r0, tile), :] + hall) * jax.nn.gelu(g_ref[pl.ds(r0, tile), :])
        return hall[0:1, :]

    hb = lax.fori_loop(0, nt, bwd, h0[1:2])
    fin_ref[...] = jnp.concatenate([hf, hb], axis=0)


def _lru_branch(proj, bs, seq, lw, h0, cols):
    p_xl, p_gl, width = cols
    ncb = width // LANE
    bx, bg = p_xl // LANE, p_gl // LANE
    ch = min(256, seq)
    sc = pltpu.VMEM((seq, LANE), F32)
    hy, fin = pl.pallas_call(
        functools.partial(_lru_kernel, seq=seq, ch=ch),
        out_shape=(jax.ShapeDtypeStruct((bs * seq, width), F32), jax.ShapeDtypeStruct((bs, ncb, 2, LANE), F32)),
        grid=(bs, ncb),
        in_specs=[pl.BlockSpec((seq, LANE), lambda b, c: (b, bx + c)),
                  pl.BlockSpec((seq, LANE), lambda b, c: (b, bg + c)),
                  pl.BlockSpec((None, LANE, 4 * LANE), lambda b, c: (c, 0, 0)),
                  pl.BlockSpec((None, SUBLANE, 4 * LANE), lambda b, c: (c, 0, 0)),
                  pl.BlockSpec((None, None, 2, LANE), lambda b, c: (b, c, 0, 0))],
        out_specs=(pl.BlockSpec((seq, LANE), lambda b, c: (b, c)),
                   pl.BlockSpec((None, None, 2, LANE), lambda b, c: (b, c, 0, 0))),
        scratch_shapes=[pltpu.VMEM((seq + 2 * SUBLANE, LANE), F32), sc, sc, sc, sc],
        compiler_params=_cparams(("arbitrary", "arbitrary")),
        name="rglru",
    )(proj, proj, lw["lru_w"], lw["lru_pp"], h0)
    return hy, fin


def _ssd_kernel(xm_ref, bc_ref, dt_ref, cwx_ref, cwb_ref, pp_ref, h0_ref, y_ref, fin_ref,
                xpad_ref, bpad_ref, xa_ref, ba_ref, ex_ref, cb_ref, ext_ref, dtt_ref, tot_ref,
                sf_ref, sb_ref, df_ref, db_ref, *, seq):
    cq = M2_CHUNK
    nc = seq // cq
    hd = M2_HEAD_DIM
    ns = M2_STATE
    npair = xm_ref.shape[1] // LANE
    hpg = 2 * npair
    pp = pp_ref[...]
    dt_bias, a_neg, dskip = pp[0:1], -jnp.exp(pp[1:2]), pp[2:3]
    cwx, cwb = cwx_ref[...], cwb_ref[...]
    _fill_padded(xpad_ref, xm_ref, seq)
    _fill_padded(bpad_ref, bc_ref, seq)

    rowi = lax.broadcasted_iota(I32, (cq, cq), 0)
    coli = lax.broadcasted_iota(I32, (cq, cq), 1)
    tri = (coli <= rowi).astype(BF16)
    lane = lax.broadcasted_iota(I32, (cq, LANE), 1)
    first_half = lane < hd
    first_half_s = lax.broadcasted_iota(I32, (ns, LANE), 1) < hd
    first_half1 = lax.broadcasted_iota(I32, (1, LANE), 1) < hd
    srefs, drefs = (sf_ref, sb_ref), (df_ref, db_ref)

    def local(c, carry):
        r0 = pl.multiple_of(c * cq, cq)
        r8 = pl.multiple_of(c * SUBLANE, SUBLANE)
        xa_all = _silu(_conv_rows(xpad_ref, r0, cq, cwx[0:4], cwx[4:5]))
        xa_ref[pl.ds(r0, cq), :] = xa_all
        bc = _silu(_conv_rows(bpad_ref, r0, cq, cwb[0:4], cwb[4:5]))
        ba_ref[pl.ds(r0, cq), :] = bc
        dtv = _softplus(dt_ref[pl.ds(r0, cq), :] + dt_bias)
        la = dtv * a_neg
        cum = _dot_exact_lhs(tri, la)
        tot = cum[cq - 1:cq, :]
        ex = jnp.where(lane < hpg, cum, cum - la)
        ex_ref[pl.ds(r0, cq), :] = ex
        ext = ex.T[0:SUBLANE, :]
        dtt = dtv.T[0:SUBLANE, :]
        ext_ref[pl.ds(r8, SUBLANE), :] = ext
        dtt_ref[pl.ds(r8, SUBLANE), :] = dtt
        tot_ref[pl.ds(r8, SUBLANE), :] = jnp.broadcast_to(tot, (SUBLANE, LANE))
        bt = bc.T[ns:2 * ns, :]
        cb_ref[pl.ds(r0, cq), :] = _dot(bc[:, 0:ns], bt)
        for d in range(2):
            for pr in range(npair):
                xa = xa_all[:, pr * LANE:(pr + 1) * LANE]
                rs, ts = [], []
                for hh in range(2):
                    col = d * hpg + 2 * pr + hh
                    th = tot[:, col:col + 1]
                    erow = ext[col:col + 1, :]
                    din = jnp.exp(th - erow) if d == 0 else jnp.exp(erow)
                    rs.append(_dot(bt * (din * dtt[col:col + 1, :]), xa))
                    ts.append(jnp.exp(th))
                srefs[d][c, pr] = jnp.where(first_half_s, rs[0], rs[1])
                r8p = pl.multiple_of((c * npair + pr) * SUBLANE, SUBLANE)
                drefs[d][pl.ds(r8p, SUBLANE), :] = jnp.broadcast_to(jnp.where(first_half1, ts[0], ts[1]), (SUBLANE, LANE))
        return carry

    lax.fori_loop(0, nc, local, 0, unroll=4 if nc % 4 == 0 else 1)

    def recur(d):
        def step(i, cur):
            c = i if d == 0 else nc - 1 - i
            out = []
            for pr in range(npair):
                r8p = pl.multiple_of((c * npair + pr) * SUBLANE, SUBLANE)
                loc = srefs[d][c, pr]
                srefs[d][c, pr] = cur[pr]
                out.append(drefs[d][pl.ds(r8p, 1), :] * cur[pr] + loc)
            return tuple(out)
        return lax.fori_loop(0, nc, step, tuple(h0_ref[d, pr] for pr in range(npair)))

    for d in range(2):
        fin = recur(d)
        for pr in range(npair):
            fin_ref[d, pr] = fin[pr]

    lower = coli <= rowi
    upper = coli >= rowi

    def output(c, carry):
        r0 = pl.multiple_of(c * cq, cq)
        r8 = pl.multiple_of(c * SUBLANE, SUBLANE)
        ex = ex_ref[pl.ds(r0, cq), :]
        ext = ext_ref[pl.ds(r8, SUBLANE), :]
        dtt = dtt_ref[pl.ds(r8, SUBLANE), :]
        tot = tot_ref[pl.ds(r8, 1), :]
        cb = cb_ref[pl.ds(r0, cq), :]
        cm = ba_ref[pl.ds(r0, cq), 0:ns]
        for pr in range(npair):
            lanes = slice(pr * LANE, (pr + 1) * LANE)
            xa = xa_ref[pl.ds(r0, cq), lanes]
            ydiag = jnp.zeros((cq, LANE), F32)
            douts = ([], [])
            for hh in range(2):
                cf, cbk = 2 * pr + hh, hpg + 2 * pr + hh
                ecf = jnp.broadcast_to(ex[:, cf:cf + 1], (cq, cq))
                ecb = jnp.broadcast_to(ex[:, cbk:cbk + 1], (cq, cq))
                lf = jnp.where(lower, jnp.exp(jnp.minimum(ecf - ext[cf:cf + 1, :], 0.0)), 0.0) * dtt[cf:cf + 1, :]
                lb = jnp.where(upper, jnp.exp(jnp.minimum(ext[cbk:cbk + 1, :] - ecb, 0.0)), 0.0) * dtt[cbk:cbk + 1, :]
                xh = jnp.where(first_half if hh == 0 else ~first_half, xa, 0.0)
                ydiag = ydiag + _dot(cb * (lf + lb), xh)
                douts[0].append(jnp.exp(ecf))
                douts[1].append(jnp.exp(tot[:, cbk:cbk + 1] - ecb))
            so = _dot(cm, jnp.concatenate([sf_ref[c, pr], sb_ref[c, pr]], axis=1))
            yoff = (so[:, 0:LANE] * jnp.where(first_half, douts[0][0], douts[0][1])
                    + so[:, LANE:2 * LANE] * jnp.where(first_half, douts[1][0], douts[1][1]))
            dsk = jnp.where(first_half1, dskip[:, 2 * pr:2 * pr + 1], dskip[:, 2 * pr + 1:2 * pr + 2])
            y_ref[pl.ds(r0, cq), lanes] = ydiag + yoff + dsk * xa
        return carry

    lax.fori_loop(0, nc, output, 0, unroll=2 if nc % 2 == 0 else 1)


def _ssd_branch(proj, bs, seq, lw, h0, cols):
    p_xm, p_bc, p_dt, inner = cols
    gw = inner // M2_GROUPS
    npair = gw // LANE
    nc = seq // M2_CHUNK
    bxm, bbc, bdt = p_xm // gw, p_bc // LANE, p_dt // LANE
    y, fin = pl.pallas_call(
        functools.partial(_ssd_kernel, seq=seq),
        out_shape=(jax.ShapeDtypeStruct((bs * seq, inner), F32),
                   jax.ShapeDtypeStruct((bs, M2_GROUPS, 2, npair, M2_STATE, LANE), F32)),
        grid=(bs, M2_GROUPS),
        in_specs=[pl.BlockSpec((seq, gw), lambda b, g: (b, bxm + g)),
                  pl.BlockSpec((seq, LANE), lambda b, g: (b, bbc + g)),
                  pl.BlockSpec((seq, LANE), lambda b, g: (b, bdt + g)),
                  pl.BlockSpec((None, SUBLANE, gw), lambda b, g: (g, 0, 0)),
                  pl.BlockSpec((None, SUBLANE, LANE), lambda b, g: (g, 0, 0)),
                  pl.BlockSpec((None, SUBLANE, LANE), lambda b, g: (g, 0, 0)),
                  pl.BlockSpec((None, None, 2, npair, M2_STATE, LANE), lambda b, g: (b, g, 0, 0, 0, 0))],
        out_specs=(pl.BlockSpec((seq, gw), lambda b, g: (b, g)),
                   pl.BlockSpec((None, None, 2, npair, M2_STATE, LANE), lambda b, g: (b, g, 0, 0, 0, 0))),
        scratch_shapes=[pltpu.VMEM((seq + 2 * SUBLANE, gw), F32), pltpu.VMEM((seq + 2 * SUBLANE, LANE), F32),
                        pltpu.VMEM((seq, gw), F32), pltpu.VMEM((seq, LANE), F32),
                        pltpu.VMEM((seq, LANE), F32), pltpu.VMEM((seq, LANE), F32),
                        pltpu.VMEM((nc * SUBLANE, LANE), F32), pltpu.VMEM((nc * SUBLANE, LANE), F32),
                        pltpu.VMEM((nc * SUBLANE, LANE), F32),
                        pltpu.VMEM((nc, npair, M2_STATE, LANE), F32), pltpu.VMEM((nc, npair, M2_STATE, LANE), F32),
                        pltpu.VMEM((nc * npair * SUBLANE, LANE), F32), pltpu.VMEM((nc * npair * SUBLANE, LANE), F32)],
        compiler_params=_cparams(("arbitrary", "arbitrary")),
        name="ssd",
    )(proj, proj, proj, lw["m2_cwx"], lw["m2_cwb"], lw["m2_pp"], h0)
    return y, fin


def _final_kernel(x_ref, y5_ref, u_ref, ys_ref, z_ref, hy_ref, g0_ref, g1_ref, g2_ref,
                  al_ref, sh_ref, sc_ref, d5_ref, wglu_ref, ng_ref, wm2_ref, wlru_ref, wo_ref, n2g_ref, wr_ref,
                  x1_ref, h2_ref, aff_ref):
    d = x_ref.shape[1]
    t5 = jax.nn.gelu(y5_ref[...] + d5_ref[...] * u_ref[...])
    vg = _dot(t5, wglu_ref[...])
    ya = vg[:, :d] * jax.nn.sigmoid(vg[:, d:])
    tb = _rms_scale(ys_ref[...] * _silu(z_ref[...])) * ng_ref[...]
    yb = _dot(tb, wm2_ref[...])
    yc = _dot(hy_ref[...], wlru_ref[...])
    merged = (jax.nn.sigmoid(g0_ref[...]) * ya + jax.nn.sigmoid(g1_ref[...]) * yb) + jax.nn.sigmoid(g2_ref[...]) * yc
    x1 = x_ref[...] + al_ref[...] * _dot(merged, wo_ref[...])
    x1_ref[...] = x1
    h2 = _rms_scale(x1) * n2g_ref[...] * (1.0 + sc_ref[...]) + sh_ref[...]
    h2_ref[...] = h2.astype(BF16)
    logits = _dot3(h2, wr_ref[...])
    valid = lax.broadcasted_iota(I32, logits.shape, 1) < N_EXPERTS
    logits = jnp.where(valid, logits, -jnp.inf)
    m = jnp.max(logits, axis=-1, keepdims=True)
    e = jnp.where(valid, jnp.exp(logits - m), 0.0)
    aff_ref[...] = e / jnp.sum(e, axis=-1, keepdims=True)


def _final(x2d, proj, y5, yssd, hy, seq, mods, lw, cols):
    t, d = x2d.shape
    tm = min(256, seq)
    alpha, shift2, scale2 = mods
    mi = _mod_index(alpha.shape[0], tm, seq)
    p_z, p_g, p_u, s5w = cols
    row = lambda w: pl.BlockSpec((tm, w), lambda i: (i, 0))
    pcol = lambda w, off: pl.BlockSpec((tm, w), lambda i: (i, off // w))
    full = lambda a: pl.BlockSpec(a.shape, lambda i: (0,) * a.ndim, pipeline_mode=pl.Buffered(1))
    mspec = pl.BlockSpec((None, 1, d), mi)
    weights = [lw["s5_d"], lw["s5_w_glu"], lw["m2_norm_g"], lw["m2_w_out"], lw["lru_w_out"], lw["w_o"],
               lw["norm2_g"], lw["w_router"]]
    return pl.pallas_call(
        _final_kernel,
        out_shape=(jax.ShapeDtypeStruct((t, d), F32), jax.ShapeDtypeStruct((t, d), BF16),
                   jax.ShapeDtypeStruct((t, LANE), F32)),
        grid=(t // tm,),
        in_specs=[row(d), row(s5w), pcol(s5w, p_u), row(d), pcol(d, p_z), row(hy.shape[1]),
                  pcol(d, p_g), pcol(d, p_g + d), pcol(d, p_g + 2 * d), mspec, mspec, mspec]
                 + [full(w) for w in weights],
        out_specs=(row(d), row(d), row(LANE)),
        compiler_params=_cparams(("arbitrary",)),
        name="merge_out",
    )(x2d, y5, proj, yssd, proj, hy, proj, proj, proj, alpha, shift2, scale2, *weights)


def _topk_kernel(aff_ref, slot_ref, *, cap):
    a = aff_ref[...]
    nblk, ne, _ = a.shape
    key = lax.bitcast_convert_type(a, I32)

    def count(m):
        return jnp.sum(jnp.sum(m.astype(I32), axis=0, keepdims=True), axis=2, keepdims=True)

    def body(i, lo):
        cand = lo | (jnp.int32(1) << (30 - i))
        return jnp.where(count(key >= cand) >= cap, cand, lo)

    kth = lax.fori_loop(0, 31, body, jnp.zeros((1, ne, 1), I32))
    gt = key > kth
    eq = key == kth
    need = cap - count(gt)
    rowi = lax.broadcasted_iota(I32, (LANE, LANE), 0)
    coli = lax.broadcasted_iota(I32, (LANE, LANE), 1)
    upper = (rowi <= coli).astype(BF16)

    def exclusive_rank(m):
        mf = m.astype(F32)
        incl = jnp.dot(mf.reshape(nblk * ne, LANE).astype(BF16), upper, preferred_element_type=F32).reshape(nblk, ne, LANE)
        offs, run = [], jnp.zeros((1, ne, 1), F32)
        for k in range(nblk):
            offs.append(run)
            run = run + incl[k:k + 1, :, LANE - 1:LANE]
        return (incl - mf + jnp.concatenate(offs, axis=0)).astype(I32)

    sel = gt | (eq & (exclusive_rank(eq) < need))
    slot_ref[...] = jnp.where(sel, exclusive_rank(sel), -1)


def _topk_slots(aff_t, cap):
    bs, nblk, ne, _ = aff_t.shape
    spec = pl.BlockSpec((None, nblk, ne, LANE), lambda b: (b, 0, 0, 0))
    return pl.pallas_call(
        functools.partial(_topk_kernel, cap=cap),
        out_shape=jax.ShapeDtypeStruct(aff_t.shape, I32),
        grid=(bs,),
        in_specs=[spec],
        out_specs=spec,
        compiler_params=_cparams(("arbitrary",)),
        name="route_topk",
    )(aff_t)


def _moe_ffn_kernel(h_ref, slot_ref, w1_ref, w3_ref, w2_ref, y_ref, p_ref, *, cap):
    nblk = slot_ref.shape[0]
    sidx = lax.broadcasted_iota(I32, (cap, LANE), 0)
    for k in range(nblk):
        p_ref[:, k * LANE:(k + 1) * LANE] = (slot_ref[k:k + 1, :] == sidx).astype(BF16)
    xs = jnp.dot(p_ref[...], h_ref[...], preferred_element_type=F32).astype(BF16)
    hid = _silu(jnp.dot(xs, w1_ref[...], preferred_element_type=F32)) * jnp.dot(xs, w3_ref[...], preferred_element_type=F32)
    y_ref[...] = jnp.dot(hid.astype(BF16), w2_ref[...], preferred_element_type=F32).astype(BF16)


def _moe_ffn(h2, slot_e, w1, w3, w2, cap):
    bs, n, d = h2.shape
    ne, nblk = slot_e.shape[1], slot_e.shape[2]
    wspec = lambda w: pl.BlockSpec((None,) + w.shape[1:], lambda e, b: (e, 0, 0))
    return pl.pallas_call(
        functools.partial(_moe_ffn_kernel, cap=cap),
        out_shape=jax.ShapeDtypeStruct((bs, ne, cap, d), BF16),
        grid=(ne, bs),
        in_specs=[pl.BlockSpec((None, n, d), lambda e, b: (b, 0, 0)),
                  pl.BlockSpec((None, None, nblk, LANE), lambda e, b: (b, e, 0, 0)),
                  wspec(w1), wspec(w3), wspec(w2)],
        out_specs=pl.BlockSpec((None, None, cap, d), lambda e, b: (b, e, 0, 0)),
        scratch_shapes=[pltpu.VMEM((cap, n), BF16)],
        compiler_params=_cparams(("arbitrary", "arbitrary")),
        name="moe_ffn",
    )(h2, slot_e, w1, w3, w2)


def _combine_kernel(x_ref, y_ref, slot_ref, aff_ref, al_ref, fg_ref, o_ref, *, cap, final_norm):
    slot = slot_ref[...]
    aff = aff_ref[...]
    tq = slot.shape[0]
    sidx = lax.broadcasted_iota(I32, (tq, cap), 1)
    acc = jnp.zeros(x_ref.shape, F32)
    for e in range(N_EXPERTS):
        pt = (slot[:, e:e + 1] == sidx).astype(BF16)
        acc = acc + aff[:, e:e + 1] * jnp.dot(pt, y_ref[e], preferred_element_type=F32)
    x2 = x_ref[...] + al_ref[...] * acc
    if final_norm:
        x2 = _rms_scale(x2) * fg_ref[...]
    o_ref[...] = x2


def _combine(x2d, y, slot_t, aff, seq, alpha, final_g, final_norm):
    t, d = x2d.shape
    bs, ne, cap, _ = y.shape
    tq = min(512, seq)
    nq = seq // tq
    mi = (lambda i: (0, 0, 0)) if alpha.shape[0] == 1 else (lambda i: (i // nq, 0, 0))
    return pl.pallas_call(
        functools.partial(_combine_kernel, cap=cap, final_norm=final_norm),
        out_shape=jax.ShapeDtypeStruct((t, d), F32),
        grid=(t // tq,),
        in_specs=[pl.BlockSpec((tq, d), lambda i: (i, 0)),
                  pl.BlockSpec((None, ne, cap, d), lambda i: (i // nq, 0, 0, 0)),
                  pl.BlockSpec((tq, LANE), lambda i: (i, 0)),
                  pl.BlockSpec((tq, LANE), lambda i: (i, 0)),
                  pl.BlockSpec((None, 1, d), mi),
                  pl.BlockSpec((1, d), lambda i: (0, 0))],
        out_specs=pl.BlockSpec((tq, d), lambda i: (i, 0)),
        compiler_params=_cparams(("arbitrary",)),
        name="moe_combine",
    )(x2d, y, slot_t, aff, alpha, final_g.reshape(1, d))


def _moe(x1, h2, aff, bs, seq, alpha, lw, final_g, final_norm):
    d = x1.shape[1]
    cap = CAPACITY * seq // N_EXPERTS
    nblk = seq // LANE
    aff_t = aff[:, :N_EXPERTS].reshape(bs, nblk, LANE, N_EXPERTS).transpose(0, 1, 3, 2)
    slot = _topk_slots(aff_t, cap)
    slot_e = slot.transpose(0, 2, 1, 3)
    slot_t = slot.transpose(0, 1, 3, 2).reshape(bs * seq, N_EXPERTS)
    slot_t = jnp.pad(slot_t, ((0, 0), (0, LANE - N_EXPERTS)), constant_values=-1)
    y = _moe_ffn(h2.reshape(bs, seq, d), slot_e, lw["moe_w1"], lw["moe_w3"], lw["moe_w2"], cap)
    return _combine(x1, y, slot_t, aff, seq, alpha, final_g, final_norm)


def _layout(d):
    s5w, inner, lruw = d // 2, d, d // 2
    gn = M2_GROUPS * M2_STATE
    heads = inner // M2_HEAD_DIM
    o_u = 0
    o_z = o_u + s5w
    o_xbc = o_z + inner
    o_dt = o_xbc + inner + 2 * gn
    o_xl = o_dt + 2 * heads
    o_gl = o_xl + lruw
    o_g = o_gl + lruw
    d_in = o_g + N_BRANCH * d
    p_z = 0
    p_g = p_z + inner
    p_xm = p_g + N_BRANCH * d
    p_bc = p_xm + inner
    p_u = p_bc + M2_GROUPS * LANE
    p_xl = p_u + s5w
    p_gl = p_xl + lruw
    p_dt = p_gl + lruw
    npad = p_dt + M2_GROUPS * LANE
    hpg = heads // M2_GROUPS
    assert hpg == 4 and M2_STATE == 64 and M2_HEAD_DIM == 64, "SSD kernel packs two 64-wide heads per lane tile"
    perm = np.full((npad,), d_in, np.int32)
    perm[p_z:p_z + inner] = o_z + np.arange(inner)
    perm[p_g:p_g + N_BRANCH * d] = o_g + np.arange(N_BRANCH * d)
    perm[p_xm:p_xm + inner] = o_xbc + np.arange(inner)
    bcp = np.zeros((M2_GROUPS * LANE,), np.int32)
    for g in range(M2_GROUPS):
        bcp[g * LANE:g * LANE + M2_STATE] = inner + gn + g * M2_STATE + np.arange(M2_STATE)
        bcp[g * LANE + M2_STATE:(g + 1) * LANE] = inner + g * M2_STATE + np.arange(M2_STATE)
    perm[p_bc:p_bc + M2_GROUPS * LANE] = o_xbc + bcp
    perm[p_u:p_u + s5w] = o_u + np.arange(s5w)
    perm[p_xl:p_xl + lruw] = o_xl + np.arange(lruw)
    perm[p_gl:p_gl + lruw] = o_gl + np.arange(lruw)
    for g in range(M2_GROUPS):
        for dd in range(2):
            for j in range(hpg):
                perm[p_dt + g * LANE + dd * hpg + j] = o_dt + dd * heads + g * hpg + j
    return dict(s5w=s5w, inner=inner, lruw=lruw, heads=heads, hpg=hpg, perm=perm, bcp=bcp, npad=npad,
                p_z=p_z, p_g=p_g, p_xm=p_xm, p_bc=p_bc, p_u=p_u, p_xl=p_xl, p_gl=p_gl, p_dt=p_dt)


def _rows8(rows, width):
    out = jnp.zeros((SUBLANE, width), F32)
    for i, r in enumerate(rows):
        out = out.at[i, :r.shape[0]].set(r.astype(F32))
    return out


def _pack_layer(lay, p, q_lat, q_ctx):
    d = p["w_in"].shape[0]
    inner, lruw, hpg, heads = lay["inner"], lay["lruw"], lay["hpg"], lay["heads"]
    lw = {}
    w_ext = jnp.concatenate([p["w_in"], jnp.zeros((d, 1), p["w_in"].dtype)], axis=1)
    lw["w_in"] = w_ext[:, lay["perm"]].astype(BF16)
    lw["s5_lat"] = _s5_weights(p["s5_lam_re"], p["s5_lam_im"], p["s5_log_step"], p["s5_b_re"], p["s5_b_im"],
                               p["s5_c_re"], p["s5_c_im"], q_lat)
    lw["s5_ctx"] = lw["s5_lat"] if q_ctx == q_lat else _s5_weights(
        p["s5_lam_re"], p["s5_lam_im"], p["s5_log_step"], p["s5_b_re"], p["s5_b_im"], p["s5_c_re"], p["s5_c_im"], q_ctx)
    gw = inner // M2_GROUPS
    cw, cb = p["m2_conv_w"], p["m2_conv_b"]
    lw["m2_cwx"] = jnp.stack([_rows8([cw[k, g * gw:(g + 1) * gw] for k in range(CONV_W)] + [cb[g * gw:(g + 1) * gw]], gw)
                              for g in range(M2_GROUPS)])
    cwb, cbb = cw[:, lay["bcp"]], cb[lay["bcp"]]
    lw["m2_cwb"] = jnp.stack([_rows8([cwb[k, g * LANE:(g + 1) * LANE] for k in range(CONV_W)] + [cbb[g * LANE:(g + 1) * LANE]], LANE)
                              for g in range(M2_GROUPS)])
    dtb = p["m2_dt_bias"].reshape(2, M2_GROUPS, hpg)
    alog = p["m2_a_log"].reshape(2, M2_GROUPS, hpg)
    dsk = p["m2_d"].reshape(M2_GROUPS, hpg)
    lw["m2_pp"] = jnp.stack([_rows8([dtb[:, g].reshape(-1), alog[:, g].reshape(-1), dsk[g]], LANE) for g in range(M2_GROUPS)])
    blk = lruw // LRU_BLOCKS
    eye = jnp.eye(LRU_BLOCKS, dtype=F32)

    def dense(w):
        return jnp.einsum('hij,hk->hikj', w.astype(F32), eye).reshape(lruw, lruw)

    ncb = lruw // LANE
    assert LANE % blk == 0
    mats = [dense(p["lru_w_a"][0]), dense(p["lru_w_x"][0]), dense(p["lru_w_a"][1]), dense(p["lru_w_x"][1])]
    lw["lru_w"] = jnp.stack([jnp.concatenate([m[c * LANE:(c + 1) * LANE, c * LANE:(c + 1) * LANE] for m in mats], axis=1)
                             for c in range(ncb)]).astype(BF16)
    sl = lambda v, c: v[c * LANE:(c + 1) * LANE]
    pps = []
    for c in range(ncb):
        bias = jnp.concatenate([sl(p["lru_b_a"][0], c), sl(p["lru_b_x"][0], c), sl(p["lru_b_a"][1], c), sl(p["lru_b_x"][1], c)])
        lam = jnp.concatenate([sl(p["lru_lam"][0], c), sl(p["lru_lam"][1], c)])
        pps.append(_rows8([bias, lam, sl(p["lru_conv_b"], c)] + [sl(p["lru_conv_w"][k], c) for k in range(CONV_W)], 4 * LANE))
    lw["lru_pp"] = jnp.stack(pps)
    lw["s5_d"] = p["s5_d"].reshape(1, -1).astype(F32)
    lw["s5_w_glu"] = p["s5_w_glu"].astype(BF16)
    lw["m2_norm_g"] = p["m2_norm_g"].reshape(1, -1).astype(F32)
    lw["m2_w_out"] = p["m2_w_out"].astype(BF16)
    lw["lru_w_out"] = p["lru_w_out"].astype(BF16)
    lw["w_o"] = p["w_o"].astype(BF16)
    lw["norm2_g"] = p["norm2_g"].reshape(1, -1).astype(F32)
    lw["w_router"] = jnp.pad(p["moe_w_router"].astype(F32), ((0, 0), (0, LANE - N_EXPERTS)))
    lw["moe_w1"] = p["moe_w1"].astype(BF16)
    lw["moe_w3"] = p["moe_w3"].astype(BF16)
    lw["moe_w2"] = p["moe_w2"].astype(BF16)
    return lw


def _mixer(x2d, bs, seq, colmajor, norm_g, shift, scale, lw, lay, h0, s5w):
    proj = _inproj(x2d, seq, norm_g, shift, scale, lw["w_in"])
    y5, f5 = _s5_branch(proj, bs, seq, colmajor, s5w, h0[0], (lay["p_u"], lay["s5w"]))
    ys, fm = _ssd_branch(proj, bs, seq, lw, h0[1], (lay["p_xm"], lay["p_bc"], lay["p_dt"], lay["inner"]))
    hy, fl = _lru_branch(proj, bs, seq, lw, h0[2], (lay["p_xl"], lay["p_gl"], lay["lruw"]))
    return proj, y5, ys, hy, (f5, fm, fl)


def kernel(x, c, ctx, c_ctx, w_mod, b_mod, norm1_g, norm2_g, w_in, s5_lam_re, s5_lam_im, s5_log_step, s5_b_re, s5_b_im, s5_c_re, s5_c_im, s5_d, s5_w_glu, m2_conv_w, m2_conv_b, m2_dt_bias, m2_a_log, m2_d, m2_norm_g, m2_w_out, lru_conv_w, lru_conv_b, lru_w_a, lru_b_a, lru_w_x, lru_b_x, lru_lam, lru_w_out, w_o, moe_w_router, moe_w1, moe_w3, moe_w2, final_norm_g):
    bsz, seq, d = x.shape
    cl = ctx.shape[1]
    depth = w_mod.shape[0]
    lay = _layout(d)
    stacked = dict(norm2_g=norm2_g, w_in=w_in, s5_lam_re=s5_lam_re, s5_lam_im=s5_lam_im, s5_log_step=s5_log_step,
                   s5_b_re=s5_b_re, s5_b_im=s5_b_im, s5_c_re=s5_c_re, s5_c_im=s5_c_im, s5_d=s5_d, s5_w_glu=s5_w_glu,
                   m2_conv_w=m2_conv_w, m2_conv_b=m2_conv_b, m2_dt_bias=m2_dt_bias, m2_a_log=m2_a_log, m2_d=m2_d,
                   m2_norm_g=m2_norm_g, m2_w_out=m2_w_out, lru_conv_w=lru_conv_w, lru_conv_b=lru_conv_b,
                   lru_w_a=lru_w_a, lru_b_a=lru_b_a, lru_w_x=lru_w_x, lru_b_x=lru_b_x, lru_lam=lru_lam,
                   lru_w_out=lru_w_out, w_o=w_o, moe_w_router=moe_w_router, moe_w1=moe_w1, moe_w3=moe_w3, moe_w2=moe_w2)
    rm = -(-(bsz + 1) // SUBLANE) * SUBLANE
    c_rows = jnp.zeros((rm, d), F32).at[:bsz].set(c.astype(F32)).at[bsz].set(c_ctx.astype(F32))
    mods = _modulation(c_rows, w_mod.astype(F32), b_mod.astype(F32))

    groups = lay["s5w"] // S5_GROUP
    ncb = lay["lruw"] // LANE
    npair = lay["inner"] // M2_GROUPS // LANE
    zero_h0 = (jnp.zeros((groups, 2, bsz, LANE), F32),
               jnp.zeros((bsz, M2_GROUPS, 2, npair, M2_STATE, LANE), F32),
               jnp.zeros((bsz, ncb, 2, LANE), F32))

    xs = x.reshape(bsz * seq, d).astype(F32)
    cs = ctx.reshape(bsz * cl, d).astype(F32)
    q_lat, q_ctx = min(S5_CHUNK, seq), min(S5_CHUNK, cl)
    for i in range(depth):
        p = {k: v[i] for k, v in stacked.items()}
        lw = _pack_layer(lay, p, q_lat, q_ctx)
        mx = [mods[i, :bsz, k * d:(k + 1) * d].reshape(bsz, 1, d) for k in range(6)]
        mc = [mods[i, bsz:bsz + 1, k * d:(k + 1) * d].reshape(1, 1, d) for k in range(6)]
        fcols = (lay["p_z"], lay["p_g"], lay["p_u"], lay["s5w"])
        cproj, cy5, cys, chy, cstates = _mixer(cs, bsz, cl, False, norm1_g[i], mc[0], mc[1], lw, lay, zero_h0, lw["s5_ctx"])
        if i < depth - 1:
            c1, ch2, caff = _final(cs, cproj, cy5, cys, chy, cl, (mc[2], mc[3], mc[4]), lw, fcols)
            cs = _moe(c1, ch2, caff, bsz, cl, mc[5], lw, final_norm_g, False)
        xproj, y5, ys, hy, _ = _mixer(xs, bsz, seq, True, norm1_g[i], mx[0], mx[1], lw, lay, cstates, lw["s5_lat"])
        x1, h2, aff = _final(xs, xproj, y5, ys, hy, seq, (mx[2], mx[3], mx[4]), lw, fcols)
        xs = _moe(x1, h2, aff, bsz, seq, mx[5], lw, final_norm_g, i == depth - 1)
    return xs.reshape(bsz, seq, d).astype(x.dtype)
```

```python
import functools
import math

import numpy as np
import jax
import jax.numpy as jnp
from jax import lax
from jax.experimental import pallas as pl
from jax.experimental.pallas import tpu as pltpu

F32 = jnp.float32
BF16 = jnp.bfloat16
I32 = jnp.int32

GRID_W = 64
EPS = 1e-6
CONV_W = 4
S5_GROUP = 16
S5_STATE = 64
M2_HEAD_DIM = 64
M2_GROUPS = 4
M2_STATE = 64
M2_CHUNK = 128
LRU_BLOCKS = 8
LRU_C = 8.0
N_EXPERTS = 16
CAPACITY = 2
N_BRANCH = 3

LANE = 128
SUBLANE = 8
S5_CHUNK = 128
LRU_TILE = 64
VMEM_LIMIT = 56 * 1024 * 1024


def _cparams(sem):
    return pltpu.CompilerParams(dimension_semantics=sem, vmem_limit_bytes=VMEM_LIMIT)


def _dot(a, b):
    return jnp.dot(a.astype(BF16), b.astype(BF16), preferred_element_type=F32)


def _dot_nt(a, b):
    return lax.dot_general(a.astype(BF16), b.astype(BF16), (((1,), (1,)), ((), ())), preferred_element_type=F32)


def _split2(a):
    hi = a.astype(BF16)
    lo = (a - hi.astype(F32)).astype(BF16)
    return hi, lo


def _split3(a):
    hi = a.astype(BF16)
    r = a - hi.astype(F32)
    mid = r.astype(BF16)
    lo = (r - mid.astype(F32)).astype(BF16)
    return hi, mid, lo


def _dot3(a, b):
    ah, al = _split2(a)
    bh, bl = _split2(b)
    d = lambda x, y: jnp.dot(x, y, preferred_element_type=F32)
    return d(ah, bh) + (d(ah, bl) + d(al, bh))


def _dot_exact_lhs(m01, x):
    hi, mid, lo = _split3(x)
    d = lambda y: jnp.dot(m01, y, preferred_element_type=F32)
    return d(hi) + (d(mid) + d(lo))


def _silu(x):
    return x * jax.nn.sigmoid(x)


def _softplus(x):
    return jnp.maximum(x, 0.0) + jnp.log(1.0 + jnp.exp(-jnp.abs(x)))


def _rms_scale(x):
    return x * lax.rsqrt(jnp.mean(x * x, axis=-1, keepdims=True) + EPS)


def _mod_kernel(c_ref, w_ref, b_ref, o_ref):
    o_ref[...] = _dot3(_silu(c_ref[...]), w_ref[...]) + b_ref[...]


def _modulation(c_rows, w_mod, b_mod):
    depth, d, n6 = w_mod.shape
    rm = c_rows.shape[0]
    tn = min(1024, n6)
    return pl.pallas_call(
        _mod_kernel,
        out_shape=jax.ShapeDtypeStruct((depth, rm, n6), F32),
        grid=(depth, n6 // tn),
        in_specs=[pl.BlockSpec((rm, d), lambda l, j: (0, 0)),
                  pl.BlockSpec((None, d, tn), lambda l, j: (l, 0, j)),
                  pl.BlockSpec((None, 1, tn), lambda l, j: (l, 0, j))],
        out_specs=pl.BlockSpec((None, rm, tn), lambda l, j: (l, 0, j)),
        compiler_params=_cparams(("arbitrary", "arbitrary")),
        name="modulation",
    )(c_rows, w_mod, b_mod.reshape(depth, 1, n6))


def _inproj_kernel(x_ref, g_ref, sh_ref, sc_ref, w_ref, o_ref, *, nchunk):
    h = _rms_scale(x_ref[...]) * g_ref[...] * (1.0 + sc_ref[...]) + sh_ref[...]
    hb = h.astype(BF16)
    npad = o_ref.shape[1]
    for n0 in range(0, npad, nchunk):
        o_ref[:, n0:n0 + nchunk] = jnp.dot(hb, w_ref[:, n0:n0 + nchunk], preferred_element_type=F32)


def _mod_index(bm, tm, seq):
    if bm == 1:
        return lambda i: (0, 0, 0)
    return lambda i: ((i * tm) // seq, 0, 0)


def _inproj(x2d, seq, g, shift, scale, w):
    t, d = x2d.shape
    npad = w.shape[1]
    tm = min(256, seq)
    mi = _mod_index(shift.shape[0], tm, seq)
    return pl.pallas_call(
        functools.partial(_inproj_kernel, nchunk=512),
        out_shape=jax.ShapeDtypeStruct((t, npad), F32),
        grid=(t // tm,),
        in_specs=[pl.BlockSpec((tm, d), lambda i: (i, 0)),
                  pl.BlockSpec((1, d), lambda i: (0, 0)),
                  pl.BlockSpec((None, 1, d), mi),
                  pl.BlockSpec((None, 1, d), mi),
                  pl.BlockSpec((d, npad), lambda i: (0, 0), pipeline_mode=pl.Buffered(1))],
        out_specs=pl.BlockSpec((tm, npad), lambda i: (i, 0)),
        compiler_params=_cparams(("arbitrary",)),
        name="inproj",
    )(x2d, g.reshape(1, d), shift, scale, w)


def _build_toeplitz(cp_ref, cn_ref, t_ref, q):
    trow = lax.broadcasted_iota(I32, (q, q), 0)
    tcol = lax.broadcasted_iota(I32, (q, q), 1)
    causal = tcol >= trow

    def body(j, carry):
        cp = cp_ref[j]
        cn = cn_ref[j]
        r0 = pl.multiple_of(j * q, q)
        for i in range(S5_GROUP):
            a = pltpu.roll(jnp.broadcast_to(cp[i:i + 1, :], (q, q)), 0, 1, stride=1, stride_axis=0)
            b = pltpu.roll(jnp.broadcast_to(cn[i:i + 1, :], (q, q)), 0, 1, stride=1, stride_axis=0)
            t_ref[pl.ds(r0, q), i * q:(i + 1) * q] = jnp.where(causal, a, b).astype(BF16)
        return carry

    lax.fori_loop(0, S5_GROUP, body, 0)


def _s5_kernel(u_ref, cp_ref, cn_ref, wst_ref, wout_ref, dec_ref, h0_ref, y_ref, fin_ref, ef_ref, eb_ref, t_ref, *, nc, bs):
    q = u_ref.shape[2]
    _build_toeplitz(cp_ref, cn_ref, t_ref, q)
    a = jnp.concatenate([u_ref[j] for j in range(S5_GROUP)], axis=-1).astype(BF16)
    loc = jnp.dot(a, wst_ref[...], preferred_element_type=F32)
    dec = dec_ref[...]
    half = S5_STATE

    def step(cur, da, db, add):
        return da * cur + db * pltpu.roll(cur, half, 1) + add

    cur = h0_ref[0]
    for c in range(nc):
        ef_ref[c * bs:(c + 1) * bs, :] = cur
        cur = step(cur, dec[0:1], dec[1:2], loc[c * bs:(c + 1) * bs, 0:LANE])
    fin_ref[0] = cur
    cur = h0_ref[1]
    for c in reversed(range(nc)):
        eb_ref[c * bs:(c + 1) * bs, :] = cur
        cur = step(cur, dec[2:3], dec[3:4], loc[c * bs:(c + 1) * bs, LANE:2 * LANE])
    fin_ref[1] = cur
    e = jnp.concatenate([ef_ref[...], eb_ref[...]], axis=-1).astype(BF16)
    acc = jnp.dot(a, t_ref[...], preferred_element_type=F32) + jnp.dot(e, wout_ref[...], preferred_element_type=F32)
    for i in range(S5_GROUP):
        y_ref[i] = acc[:, i * q:(i + 1) * q]


def _s5_scan(ut, s5w, h0, nc, bs):
    cpos, cneg, wst, wout, dec = s5w
    groups = cpos.shape[0]
    r, q = ut.shape[1], ut.shape[2]
    assert q == LANE, "the Toeplitz builder rotates one 128-lane tile per block"
    kq = S5_GROUP * q
    lagspec = pl.BlockSpec((None, S5_GROUP, S5_GROUP, q), lambda g: (g, 0, 0, 0))
    return pl.pallas_call(
        functools.partial(_s5_kernel, nc=nc, bs=bs),
        out_shape=(jax.ShapeDtypeStruct(ut.shape, F32), jax.ShapeDtypeStruct((groups, 2, bs, LANE), F32)),
        grid=(groups,),
        in_specs=[pl.BlockSpec((S5_GROUP, r, q), lambda g: (g, 0, 0)),
                  lagspec, lagspec,
                  pl.BlockSpec((None, kq, 2 * LANE), lambda g: (g, 0, 0)),
                  pl.BlockSpec((None, 2 * LANE, kq), lambda g: (g, 0, 0)),
                  pl.BlockSpec((None, SUBLANE, LANE), lambda g: (g, 0, 0)),
                  pl.BlockSpec((None, 2, bs, LANE), lambda g: (g, 0, 0, 0))],
        out_specs=(pl.BlockSpec((S5_GROUP, r, q), lambda g: (g, 0, 0)),
                   pl.BlockSpec((None, 2, bs, LANE), lambda g: (g, 0, 0, 0))),
        scratch_shapes=[pltpu.VMEM((r, LANE), F32), pltpu.VMEM((r, LANE), F32), pltpu.VMEM((kq, kq), BF16)],
        compiler_params=_cparams(("arbitrary",)),
        name="s5_scan",
    )(ut, cpos, cneg, wst, wout, dec, h0)


def _s5_weights(lam_re, lam_im, log_step, b_re, b_im, c_re, c_im, q):
    hp = lax.Precision.HIGHEST
    g, p = lam_re.shape[1], lam_re.shape[2]
    ii = S5_GROUP
    tau = jnp.arange(q + 1, dtype=F32)[:, None, None]
    ks, wsts, wouts, decs = [], [], [], []
    for d in range(2):
        lr, li = lam_re[d].astype(F32), lam_im[d].astype(F32)
        step = jnp.exp(log_step[d].astype(F32))[:, None]
        ar, ai = lr * step, li * step
        mag = jnp.exp(tau * ar)
        pre, pim = mag * jnp.cos(tau * ai), mag * jnp.sin(tau * ai)
        bar_re, bar_im = pre[1], pim[1]
        den = lr * lr + li * li
        nr, ni = bar_re - 1.0, bar_im
        coef_re = (nr * lr + ni * li) / den
        coef_im = (ni * lr - nr * li) / den
        bre, bim = b_re[d].astype(F32), b_im[d].astype(F32)
        bb_re = coef_re[..., None] * bre - coef_im[..., None] * bim
        bb_im = coef_re[..., None] * bim + coef_im[..., None] * bre
        cre, cim = c_re[d].astype(F32), c_im[d].astype(F32)
        cp_re = cre[None] * pre[:, :, None, :] - cim[None] * pim[:, :, None, :]
        cp_im = cre[None] * pim[:, :, None, :] + cim[None] * pre[:, :, None, :]
        k = (jnp.einsum('tgip,gpj->tgij', cp_re[:q], bb_re, precision=hp)
             - jnp.einsum('tgip,gpj->tgij', cp_im[:q], bb_im, precision=hp))
        ks.append(k)
        pw_re = pre[q - 1::-1][:q] if d == 0 else pre[:q]
        pw_im = pim[q - 1::-1][:q] if d == 0 else pim[:q]
        w_re = pw_re[:, :, :, None] * bb_re[None] - pw_im[:, :, :, None] * bb_im[None]
        w_im = pw_re[:, :, :, None] * bb_im[None] + pw_im[:, :, :, None] * bb_re[None]
        w = jnp.concatenate([w_re, w_im], axis=2)
        wsts.append(jnp.transpose(w, (1, 3, 0, 2)).reshape(g, ii * q, 2 * p))
        if d == 0:
            o_re, o_im = cp_re[1:q + 1], cp_im[1:q + 1]
        else:
            o_re, o_im = cp_re[q:0:-1], cp_im[q:0:-1]
        o = jnp.concatenate([o_re, -o_im], axis=3)
        wouts.append(jnp.transpose(o, (1, 3, 2, 0)).reshape(g, 2 * p, ii * q))
        dr, di = pre[q], pim[q]
        decs.append(jnp.concatenate([dr, dr], axis=-1))
        decs.append(jnp.concatenate([-di, di], axis=-1))
    cpos = jnp.transpose(ks[0].at[0].add(ks[1][0]), (1, 3, 2, 0))
    kb_rev = ks[1][::-1]
    cneg = jnp.concatenate([jnp.zeros_like(kb_rev[:1]), kb_rev[:q - 1]], axis=0)
    cneg = jnp.transpose(cneg, (1, 3, 2, 0))
    wst = jnp.concatenate(wsts, axis=-1).astype(BF16)
    wout = jnp.concatenate(wouts, axis=1).astype(BF16)
    dec = jnp.stack(decs + decs, axis=1)
    return cpos, cneg, wst, wout, dec


def _s5_branch(proj, bs, seq, colmajor, s5w, h0, cols):
    p_u, s5w_width = cols
    groups = s5w_width // S5_GROUP
    q = min(S5_CHUNK, seq)
    nc = seq // q
    u = proj[:, p_u:p_u + s5w_width].reshape(bs, seq, s5w_width)
    if colmajor:
        rows = seq // GRID_W
        u = u.reshape(bs, rows, GRID_W, s5w_width).transpose(0, 2, 1, 3).reshape(bs, seq, s5w_width)
    ut = u.reshape(bs, nc, q, groups, S5_GROUP).transpose(3, 4, 1, 0, 2).reshape(s5w_width, nc * bs, q)
    yt, fin = _s5_scan(ut, s5w, h0, nc, bs)
    y = yt.reshape(groups, S5_GROUP, nc, bs, q).transpose(3, 2, 4, 0, 1).reshape(bs, seq, s5w_width)
    if colmajor:
        rows = seq // GRID_W
        y = y.reshape(bs, GRID_W, rows, s5w_width).transpose(0, 2, 1, 3).reshape(bs, seq, s5w_width)
    return y.reshape(bs * seq, s5w_width), fin


def _conv_rows(pad_ref, base, n, taps, bias):
    w = pad_ref[pl.ds(base, n + 2 * SUBLANE), :]
    tot = n + 2 * SUBLANE
    xm2 = pltpu.roll(w, 2, 0)[SUBLANE:SUBLANE + n]
    xm1 = pltpu.roll(w, 1, 0)[SUBLANE:SUBLANE + n]
    x0 = w[SUBLANE:SUBLANE + n]
    xp1 = pltpu.roll(w, tot - 1, 0)[SUBLANE:SUBLANE + n]
    return taps[0:1] * xm2 + taps[1:2] * xm1 + taps[2:3] * x0 + taps[3:4] * xp1 + bias


def _fill_padded(pad_ref, x_ref, seq):
    zeros = jnp.zeros((SUBLANE, pad_ref.shape[1]), F32)
    pad_ref[0:SUBLANE, :] = zeros
    pad_ref[SUBLANE + seq:2 * SUBLANE + seq, :] = zeros
    pad_ref[SUBLANE:SUBLANE + seq, :] = x_ref[...]


def _scan_tile(a, v, h, reverse):
    s = a.shape[0]
    row = lax.broadcasted_iota(I32, a.shape, 0)
    k = 1
    while k < s:
        sh = s - k if reverse else k
        ok = (row < s - k) if reverse else (row >= k)
        a_sh = pltpu.roll(a, sh, 0)
        v_sh = pltpu.roll(v, sh, 0)
        v = v + a * jnp.where(ok, v_sh, 0.0)
        a = a * jnp.where(ok, a_sh, 1.0)
        k *= 2
    return v + a * h


def _lru_kernel(x_ref, g_ref, w_ref, pp_ref, h0_ref, o_ref, fin_ref, pad_ref, a0_ref, v0_ref, a1_ref, v1_ref, *, seq, ch):
    pp = pp_ref[...]
    bias = pp[0:1]
    lam = pp[1:2]
    cb = pp[2:3, 0:LANE]
    taps = pp[3:7, 0:LANE]
    _fill_padded(pad_ref, x_ref, seq)
    av = ((a0_ref, v0_ref), (a1_ref, v1_ref))

    def gates(i, carry):
        base = pl.multiple_of(i * ch, ch)
        xc = _conv_rows(pad_ref, base, ch, taps, cb)
        gt = _dot(xc, w_ref[...]) + bias
        for d in range(2):
            r = jax.nn.sigmoid(gt[:, 2 * d * LANE:(2 * d + 1) * LANE])
            ig = jax.nn.sigmoid(gt[:, (2 * d + 1) * LANE:(2 * d + 2) * LANE])
            log_a = -LRU_C * r * _softplus(-lam[:, d * LANE:(d + 1) * LANE])
            av[d][0][pl.ds(base, ch), :] = jnp.exp(log_a)
            av[d][1][pl.ds(base, ch), :] = jnp.sqrt(jnp.maximum(1.0 - jnp.exp(2.0 * log_a), EPS)) * (ig * xc)
        return carry

    lax.fori_loop(0, seq // ch, gates, 0)
    tile = min(LRU_TILE, seq)
    nt = seq // tile

    def fwd(i, h):
        r0 = pl.multiple_of(i * tile, tile)
        hall = _scan_tile(a0_ref[pl.ds(r0, tile), :], v0_ref[pl.ds(r0, tile), :], h, False)
        o_ref[pl.ds(r0, tile), :] = hall
        return hall[tile - 1:tile, :]

    h0 = h0_ref[...]
    hf = lax.fori_loop(0, nt, fwd, h0[0:1])

    def bwd(i, h):
        r0 = pl.multiple_of((nt - 1 - i) * tile, tile)
        hall = _scan_tile(a1_ref[pl.ds(r0, tile), :], v1_ref[pl.ds(r0, tile), :], h, True)
        o_ref[pl.ds(r0, tile), :] = (o_ref[pl.ds(r0, tile), :] + hall) * jax.nn.gelu(g_ref[pl.ds(r0, tile), :])
        return hall[0:1, :]

    hb = lax.fori_loop(0, nt, bwd, h0[1:2])
    fin_ref[...] = jnp.concatenate([hf, hb], axis=0)


def _lru_branch(proj, bs, seq, lw, h0, cols):
    p_xl, p_gl, width = cols
    ncb = width // LANE
    bx, bg = p_xl // LANE, p_gl // LANE
    ch = min(256, seq)
    sc = pltpu.VMEM((seq, LANE), F32)
    hy, fin = pl.pallas_call(
        functools.partial(_lru_kernel, seq=seq, ch=ch),
        out_shape=(jax.ShapeDtypeStruct((bs * seq, width), F32), jax.ShapeDtypeStruct((bs, ncb, 2, LANE), F32)),
        grid=(bs, ncb),
        in_specs=[pl.BlockSpec((seq, LANE), lambda b, c: (b, bx + c)),
                  pl.BlockSpec((seq, LANE), lambda b, c: (b, bg + c)),
                  pl.BlockSpec((None, LANE, 4 * LANE), lambda b, c: (c, 0, 0)),
                  pl.BlockSpec((None, SUBLANE, 4 * LANE), lambda b, c: (c, 0, 0)),
                  pl.BlockSpec((None, None, 2, LANE), lambda b, c: (b, c, 0, 0))],
        out_specs=(pl.BlockSpec((seq, LANE), lambda b, c: (b, c)),
                   pl.BlockSpec((None, None, 2, LANE), lambda b, c: (b, c, 0, 0))),
        scratch_shapes=[pltpu.VMEM((seq + 2 * SUBLANE, LANE), F32), sc, sc, sc, sc],
        compiler_params=_cparams(("arbitrary", "arbitrary")),
        name="rglru",
    )(proj, proj, lw["lru_w"], lw["lru_pp"], h0)
    return hy, fin


def _ssd_kernel(xm_ref, bc_ref, dt_ref, cwx_ref, cwb_ref, pp_ref, h0_ref, y_ref, fin_ref,
                xpad_ref, bpad_ref, xa_ref, ba_ref, ex_ref, cb_ref, ext_ref, dtt_ref, tot_ref,
                sf_ref, sb_ref, df_ref, db_ref, *, seq):
    cq = M2_CHUNK
    nc = seq // cq
    hd = M2_HEAD_DIM
    ns = M2_STATE
    npair = xm_ref.shape[1] // LANE
    hpg = 2 * npair
    pp = pp_ref[...]
    dt_bias, a_neg, dskip = pp[0:1], -jnp.exp(pp[1:2]), pp[2:3]
    cwx, cwb = cwx_ref[...], cwb_ref[...]
    _fill_padded(xpad_ref, xm_ref, seq)
    _fill_padded(bpad_ref, bc_ref, seq)

    rowi = lax.broadcasted_iota(I32, (cq, cq), 0)
    coli = lax.broadcasted_iota(I32, (cq, cq), 1)
    tri = (coli <= rowi).astype(BF16)
    lane = lax.broadcasted_iota(I32, (cq, LANE), 1)
    first_half = lane < hd
    first_half_s = lax.broadcasted_iota(I32, (ns, LANE), 1) < hd
    first_half1 = lax.broadcasted_iota(I32, (1, LANE), 1) < hd
    srefs, drefs = (sf_ref, sb_ref), (df_ref, db_ref)

    def local(c, carry):
        r0 = pl.multiple_of(c * cq, cq)
        r8 = pl.multiple_of(c * SUBLANE, SUBLANE)
        xa_all = _silu(_conv_rows(xpad_ref, r0, cq, cwx[0:4], cwx[4:5]))
        xa_ref[pl.ds(r0, cq), :] = xa_all
        bc = _silu(_conv_rows(bpad_ref, r0, cq, cwb[0:4], cwb[4:5]))
        ba_ref[pl.ds(r0, cq), :] = bc
        dtv = _softplus(dt_ref[pl.ds(r0, cq), :] + dt_bias)
        la = dtv * a_neg
        cum = _dot_exact_lhs(tri, la)
        tot = cum[cq - 1:cq, :]
        ex = jnp.where(lane < hpg, cum, cum - la)
        ex_ref[pl.ds(r0, cq), :] = ex
        ext = ex.T[0:SUBLANE, :]
        dtt = dtv.T[0:SUBLANE, :]
        ext_ref[pl.ds(r8, SUBLANE), :] = ext
        dtt_ref[pl.ds(r8, SUBLANE), :] = dtt
        tot_ref[pl.ds(r8, SUBLANE), :] = jnp.broadcast_to(tot, (SUBLANE, LANE))
        bt = bc.T[ns:2 * ns, :]
        cb_ref[pl.ds(r0, cq), :] = _dot(bc[:, 0:ns], bt)
        for d in range(2):
            for pr in range(npair):
                xa = xa_all[:, pr * LANE:(pr + 1) * LANE]
                rs, ts = [], []
                for hh in range(2):
                    col = d * hpg + 2 * pr + hh
                    th = tot[:, col:col + 1]
                    erow = ext[col:col + 1, :]
                    din = jnp.exp(th - erow) if d == 0 else jnp.exp(erow)
                    rs.append(_dot(bt * (din * dtt[col:col + 1, :]), xa))
                    ts.append(jnp.exp(th))
                srefs[d][c, pr] = jnp.where(first_half_s, rs[0], rs[1])
                r8p = pl.multiple_of((c * npair + pr) * SUBLANE, SUBLANE)
                drefs[d][pl.ds(r8p, SUBLANE), :] = jnp.broadcast_to(jnp.where(first_half1, ts[0], ts[1]), (SUBLANE, LANE))
        return carry

    lax.fori_loop(0, nc, local, 0, unroll=4 if nc % 4 == 0 else 1)

    def recur(d):
        def step(i, cur):
            c = i if d == 0 else nc - 1 - i
            out = []
            for pr in range(npair):
                r8p = pl.multiple_of((c * npair + pr) * SUBLANE, SUBLANE)
                loc = srefs[d][c, pr]
                srefs[d][c, pr] = cur[pr]
                out.append(drefs[d][pl.ds(r8p, 1), :] * cur[pr] + loc)
            return tuple(out)
        return lax.fori_loop(0, nc, step, tuple(h0_ref[d, pr] for pr in range(npair)))

    for d in range(2):
        fin = recur(d)
        for pr in range(npair):
            fin_ref[d, pr] = fin[pr]

    lower = coli <= rowi
    upper = coli >= rowi

    def output(c, carry):
        r0 = pl.multiple_of(c * cq, cq)
        r8 = pl.multiple_of(c * SUBLANE, SUBLANE)
        ex = ex_ref[pl.ds(r0, cq), :]
        ext = ext_ref[pl.ds(r8, SUBLANE), :]
        dtt = dtt_ref[pl.ds(r8, SUBLANE), :]
        tot = tot_ref[pl.ds(r8, 1), :]
        cb = cb_ref[pl.ds(r0, cq), :]
        cm = ba_ref[pl.ds(r0, cq), 0:ns]
        for pr in range(npair):
            lanes = slice(pr * LANE, (pr + 1) * LANE)
            xa = xa_ref[pl.ds(r0, cq), lanes]
            ydiag = jnp.zeros((cq, LANE), F32)
            douts = ([], [])
            for hh in range(2):
                cf, cbk = 2 * pr + hh, hpg + 2 * pr + hh
                ecf = jnp.broadcast_to(ex[:, cf:cf + 1], (cq, cq))
                ecb = jnp.broadcast_to(ex[:, cbk:cbk + 1], (cq, cq))
                lf = jnp.where(lower, jnp.exp(jnp.minimum(ecf - ext[cf:cf + 1, :], 0.0)), 0.0) * dtt[cf:cf + 1, :]
                lb = jnp.where(upper, jnp.exp(jnp.minimum(ext[cbk:cbk + 1, :] - ecb, 0.0)), 0.0) * dtt[cbk:cbk + 1, :]
                xh = jnp.where(first_half if hh == 0 else ~first_half, xa, 0.0)
                ydiag = ydiag + _dot(cb * (lf + lb), xh)
                douts[0].append(jnp.exp(ecf))
                douts[1].append(jnp.exp(tot[:, cbk:cbk + 1] - ecb))
            so = _dot(cm, jnp.concatenate([sf_ref[c, pr], sb_ref[c, pr]], axis=1))
            yoff = (so[:, 0:LANE] * jnp.where(first_half, douts[0][0], douts[0][1])
                    + so[:, LANE:2 * LANE] * jnp.where(first_half, douts[1][0], douts[1][1]))
            dsk = jnp.where(first_half1, dskip[:, 2 * pr:2 * pr + 1], dskip[:, 2 * pr + 1:2 * pr + 2])
            y_ref[pl.ds(r0, cq), lanes] = ydiag + yoff + dsk * xa
        return carry

    lax.fori_loop(0, nc, output, 0, unroll=2 if nc % 2 == 0 else 1)


def _ssd_branch(proj, bs, seq, lw, h0, cols):
    p_xm, p_bc, p_dt, inner = cols
    gw = inner // M2_GROUPS
    npair = gw // LANE
    nc = seq // M2_CHUNK
    bxm, bbc, bdt = p_xm // gw, p_bc // LANE, p_dt // LANE
    y, fin = pl.pallas_call(
        functools.partial(_ssd_kernel, seq=seq),
        out_shape=(jax.ShapeDtypeStruct((bs * seq, inner), F32),
                   jax.ShapeDtypeStruct((bs, M2_GROUPS, 2, npair, M2_STATE, LANE), F32)),
        grid=(bs, M2_GROUPS),
        in_specs=[pl.BlockSpec((seq, gw), lambda b, g: (b, bxm + g)),
                  pl.BlockSpec((seq, LANE), lambda b, g: (b, bbc + g)),
                  pl.BlockSpec((seq, LANE), lambda b, g: (b, bdt + g)),
                  pl.BlockSpec((None, SUBLANE, gw), lambda b, g: (g, 0, 0)),
                  pl.BlockSpec((None, SUBLANE, LANE), lambda b, g: (g, 0, 0)),
                  pl.BlockSpec((None, SUBLANE, LANE), lambda b, g: (g, 0, 0)),
                  pl.BlockSpec((None, None, 2, npair, M2_STATE, LANE), lambda b, g: (b, g, 0, 0, 0, 0))],
        out_specs=(pl.BlockSpec((seq, gw), lambda b, g: (b, g)),
                   pl.BlockSpec((None, None, 2, npair, M2_STATE, LANE), lambda b, g: (b, g, 0, 0, 0, 0))),
        scratch_shapes=[pltpu.VMEM((seq + 2 * SUBLANE, gw), F32), pltpu.VMEM((seq + 2 * SUBLANE, LANE), F32),
                        pltpu.VMEM((seq, gw), F32), pltpu.VMEM((seq, LANE), F32),
                        pltpu.VMEM((seq, LANE), F32), pltpu.VMEM((seq, LANE), F32),
                        pltpu.VMEM((nc * SUBLANE, LANE), F32), pltpu.VMEM((nc * SUBLANE, LANE), F32),
                        pltpu.VMEM((nc * SUBLANE, LANE), F32),
                        pltpu.VMEM((nc, npair, M2_STATE, LANE), F32), pltpu.VMEM((nc, npair, M2_STATE, LANE), F32),
                        pltpu.VMEM((nc * npair * SUBLANE, LANE), F32), pltpu.VMEM((nc * npair * SUBLANE, LANE), F32)],
        compiler_params=_cparams(("arbitrary", "arbitrary")),
        name="ssd",
    )(proj, proj, proj, lw["m2_cwx"], lw["m2_cwb"], lw["m2_pp"], h0)
    return y, fin


def _final_kernel(x_ref, y5_ref, u_ref, ys_ref, z_ref, hy_ref, g0_ref, g1_ref, g2_ref,
                  al_ref, sh_ref, sc_ref, d5_ref, wglu_ref, ng_ref, wm2_ref, wlru_ref, wo_ref, n2g_ref, wr_ref,
                  x1_ref, h2_ref, aff_ref):
    d = x_ref.shape[1]
    t5 = jax.nn.gelu(y5_ref[...] + d5_ref[...] * u_ref[...])
    vg = _dot(t5, wglu_ref[...])
    ya = vg[:, :d] * jax.nn.sigmoid(vg[:, d:])
    tb = _rms_scale(ys_ref[...] * _silu(z_ref[...])) * ng_ref[...]
    yb = _dot(tb, wm2_ref[...])
    yc = _dot(hy_ref[...], wlru_ref[...])
    merged = (jax.nn.sigmoid(g0_ref[...]) * ya + jax.nn.sigmoid(g1_ref[...]) * yb) + jax.nn.sigmoid(g2_ref[...]) * yc
    x1 = x_ref[...] + al_ref[...] * _dot(merged, wo_ref[...])
    x1_ref[...] = x1
    h2 = _rms_scale(x1) * n2g_ref[...] * (1.0 + sc_ref[...]) + sh_ref[...]
    h2_ref[...] = h2.astype(BF16)
    logits = _dot3(h2, wr_ref[...])
    valid = lax.broadcasted_iota(I32, logits.shape, 1) < N_EXPERTS
    logits = jnp.where(valid, logits, -jnp.inf)
    m = jnp.max(logits, axis=-1, keepdims=True)
    e = jnp.where(valid, jnp.exp(logits - m), 0.0)
    aff_ref[...] = e / jnp.sum(e, axis=-1, keepdims=True)


def _final(x2d, proj, y5, yssd, hy, seq, mods, lw, cols):
    t, d = x2d.shape
    tm = min(256, seq)
    alpha, shift2, scale2 = mods
    mi = _mod_index(alpha.shape[0], tm, seq)
    p_z, p_g, p_u, s5w = cols
    row = lambda w: pl.BlockSpec((tm, w), lambda i: (i, 0))
    pcol = lambda w, off: pl.BlockSpec((tm, w), lambda i: (i, off // w))
    full = lambda a: pl.BlockSpec(a.shape, lambda i: (0,) * a.ndim, pipeline_mode=pl.Buffered(1))
    mspec = pl.BlockSpec((None, 1, d), mi)
    weights = [lw["s5_d"], lw["s5_w_glu"], lw["m2_norm_g"], lw["m2_w_out"], lw["lru_w_out"], lw["w_o"],
               lw["norm2_g"], lw["w_router"]]
    return pl.pallas_call(
        _final_kernel,
        out_shape=(jax.ShapeDtypeStruct((t, d), F32), jax.ShapeDtypeStruct((t, d), BF16),
                   jax.ShapeDtypeStruct((t, LANE), F32)),
        grid=(t // tm,),
        in_specs=[row(d), row(s5w), pcol(s5w, p_u), row(d), pcol(d, p_z), row(hy.shape[1]),
                  pcol(d, p_g), pcol(d, p_g + d), pcol(d, p_g + 2 * d), mspec, mspec, mspec]
                 + [full(w) for w in weights],
        out_specs=(row(d), row(d), row(LANE)),
        compiler_params=_cparams(("arbitrary",)),
        name="merge_out",
    )(x2d, y5, proj, yssd, proj, hy, proj, proj, proj, alpha, shift2, scale2, *weights)


def _topk_kernel(aff_ref, slot_ref, start_ref, *, cap):
    a = aff_ref[...]
    nblk, ne, _ = a.shape
    key = lax.bitcast_convert_type(a, I32)

    def count(m):
        return jnp.sum(jnp.sum(m.astype(I32), axis=0, keepdims=True), axis=2, keepdims=True)

    def body(i, lo):
        cand = lo | (jnp.int32(1) << (30 - i))
        return jnp.where(count(key >= cand) >= cap, cand, lo)

    kth = lax.fori_loop(0, 31, body, jnp.zeros((1, ne, 1), I32))
    gt = key > kth
    eq = key == kth
    need = cap - count(gt)
    rowi = lax.broadcasted_iota(I32, (LANE, LANE), 0)
    coli = lax.broadcasted_iota(I32, (LANE, LANE), 1)
    upper = (rowi <= coli).astype(BF16)

    def exclusive_rank(m):
        mf = m.astype(F32)
        incl = jnp.dot(mf.reshape(nblk * ne, LANE).astype(BF16), upper, preferred_element_type=F32).reshape(nblk, ne, LANE)
        offs, run = [], jnp.zeros((1, ne, 1), F32)
        for k in range(nblk):
            offs.append(run)
            run = run + incl[k:k + 1, :, LANE - 1:LANE]
        offs = jnp.concatenate(offs, axis=0)
        return (incl - mf + offs).astype(I32), offs.astype(I32)

    sel = gt | (eq & (exclusive_rank(eq)[0] < need))
    rank, offs = exclusive_rank(sel)
    slot_ref[...] = jnp.where(sel, rank, -1)
    start_ref[...] = jnp.broadcast_to(offs, start_ref.shape)


def _topk_slots(aff_t, cap):
    bs, nblk, ne, _ = aff_t.shape
    spec = pl.BlockSpec((None, nblk, ne, LANE), lambda b: (b, 0, 0, 0))
    return pl.pallas_call(
        functools.partial(_topk_kernel, cap=cap),
        out_shape=(jax.ShapeDtypeStruct(aff_t.shape, I32), jax.ShapeDtypeStruct(aff_t.shape, I32)),
        grid=(bs,),
        in_specs=[spec],
        out_specs=(spec, spec),
        compiler_params=_cparams(("arbitrary",)),
        name="route_topk",
    )(aff_t)


def _moe_ffn_kernel(start_ref, h_ref, slot_ref, w1_ref, w3_ref, w2_ref, y_ref, xs_ref, *, cap, tw, sb):
    e, b = pl.program_id(0), pl.program_id(1)
    nblk = slot_ref.shape[0]
    bpw = tw // LANE
    nw = nblk // bpw
    base = (b * pl.num_programs(0) + e) * nblk
    begins = [start_ref[base + k * bpw] for k in range(nw)]
    ends = begins[1:] + [cap]
    sidx = lax.broadcasted_iota(I32, (sb, LANE), 0)
    for j in range(cap // sb):
        lo, hi = j * sb, (j + 1) * sb
        k_lo = sum(jnp.asarray(en <= lo, I32) for en in ends)
        k_hi = sum(jnp.asarray(bg < hi, I32) for bg in begins)
        xs_ref[lo:hi, :] = jnp.zeros((sb, xs_ref.shape[1]), F32)

        def body(k, carry, lo=lo, hi=hi):
            t0 = pl.multiple_of(k * tw, tw)
            p = jnp.concatenate([(slot_ref[pl.ds(k * bpw + i, 1), :] == sidx + lo).astype(BF16) for i in range(bpw)], axis=1)
            xs_ref[lo:hi, :] += jnp.dot(p, h_ref[pl.ds(t0, tw), :], preferred_element_type=F32)
            return carry

        lax.fori_loop(k_lo, k_hi, body, 0)
    xs = xs_ref[...].astype(BF16)
    hid = _silu(jnp.dot(xs, w1_ref[...], preferred_element_type=F32)) * jnp.dot(xs, w3_ref[...], preferred_element_type=F32)
    y_ref[...] = jnp.dot(hid.astype(BF16), w2_ref[...], preferred_element_type=F32).astype(BF16)


def _moe_ffn(h2, slot_e, starts, w1, w3, w2, cap):
    bs, n, d = h2.shape
    ne, nblk = slot_e.shape[1], slot_e.shape[2]
    tw = min(512, n)
    sb = min(LANE, cap)
    wspec = lambda w: pl.BlockSpec((None,) + w.shape[1:], lambda e, b, s: (e, 0, 0))
    return pl.pallas_call(
        functools.partial(_moe_ffn_kernel, cap=cap, tw=tw, sb=sb),
        out_shape=jax.ShapeDtypeStruct((bs, ne, cap, d), BF16),
        grid_spec=pltpu.PrefetchScalarGridSpec(
            num_scalar_prefetch=1,
            grid=(ne, bs),
            in_specs=[pl.BlockSpec((None, n, d), lambda e, b, s: (b, 0, 0)),
                      pl.BlockSpec((None, None, nblk, LANE), lambda e, b, s: (b, e, 0, 0)),
                      wspec(w1), wspec(w3), wspec(w2)],
            out_specs=pl.BlockSpec((None, None, cap, d), lambda e, b, s: (b, e, 0, 0)),
            scratch_shapes=[pltpu.VMEM((cap, d), F32)]),
        compiler_params=_cparams(("arbitrary", "arbitrary")),
        name="moe_ffn",
    )(starts, h2, slot_e, w1, w3, w2)


def _combine_kernel(x_ref, y_ref, slot_ref, aff_ref, al_ref, fg_ref, o_ref, *, cap, final_norm):
    slot = slot_ref[...]
    aff = aff_ref[...]
    tq = slot.shape[0]
    sidx = lax.broadcasted_iota(I32, (tq, cap), 1)
    acc = jnp.zeros(x_ref.shape, F32)
    for e in range(N_EXPERTS):
        pt = (slot[:, e:e + 1] == sidx).astype(BF16)
        acc = acc + aff[:, e:e + 1] * jnp.dot(pt, y_ref[e], preferred_element_type=F32)
    x2 = x_ref[...] + al_ref[...] * acc
    if final_norm:
        x2 = _rms_scale(x2) * fg_ref[...]
    o_ref[...] = x2


def _combine(x2d, y, slot_t, aff, seq, alpha, final_g, final_norm):
    t, d = x2d.shape
    bs, ne, cap, _ = y.shape
    tq = min(512, seq)
    nq = seq // tq
    mi = (lambda i: (0, 0, 0)) if alpha.shape[0] == 1 else (lambda i: (i // nq, 0, 0))
    return pl.pallas_call(
        functools.partial(_combine_kernel, cap=cap, final_norm=final_norm),
        out_shape=jax.ShapeDtypeStruct((t, d), F32),
        grid=(t // tq,),
        in_specs=[pl.BlockSpec((tq, d), lambda i: (i, 0)),
                  pl.BlockSpec((None, ne, cap, d), lambda i: (i // nq, 0, 0, 0)),
                  pl.BlockSpec((tq, LANE), lambda i: (i, 0)),
                  pl.BlockSpec((tq, LANE), lambda i: (i, 0)),
                  pl.BlockSpec((None, 1, d), mi),
                  pl.BlockSpec((1, d), lambda i: (0, 0))],
        out_specs=pl.BlockSpec((tq, d), lambda i: (i, 0)),
        compiler_params=_cparams(("arbitrary",)),
        name="moe_combine",
    )(x2d, y, slot_t, aff, alpha, final_g.reshape(1, d))


def _moe(x1, h2, aff, bs, seq, alpha, lw, final_g, final_norm):
    d = x1.shape[1]
    cap = CAPACITY * seq // N_EXPERTS
    nblk = seq // LANE
    aff_t = aff[:, :N_EXPERTS].reshape(bs, nblk, LANE, N_EXPERTS).transpose(0, 1, 3, 2)
    slot, start = _topk_slots(aff_t, cap)
    slot_e = slot.transpose(0, 2, 1, 3)
    slot_t = slot.transpose(0, 1, 3, 2).reshape(bs * seq, N_EXPERTS)
    slot_t = jnp.pad(slot_t, ((0, 0), (0, LANE - N_EXPERTS)), constant_values=-1)
    starts = start[:, :, :, 0].transpose(0, 2, 1).reshape(-1)
    y = _moe_ffn(h2.reshape(bs, seq, d), slot_e, starts, lw["moe_w1"], lw["moe_w3"], lw["moe_w2"], cap)
    return _combine(x1, y, slot_t, aff, seq, alpha, final_g, final_norm)


def _layout(d):
    s5w, inner, lruw = d // 2, d, d // 2
    gn = M2_GROUPS * M2_STATE
    heads = inner // M2_HEAD_DIM
    o_u = 0
    o_z = o_u + s5w
    o_xbc = o_z + inner
    o_dt = o_xbc + inner + 2 * gn
    o_xl = o_dt + 2 * heads
    o_gl = o_xl + lruw
    o_g = o_gl + lruw
    d_in = o_g + N_BRANCH * d
    p_z = 0
    p_g = p_z + inner
    p_xm = p_g + N_BRANCH * d
    p_bc = p_xm + inner
    p_u = p_bc + M2_GROUPS * LANE
    p_xl = p_u + s5w
    p_gl = p_xl + lruw
    p_dt = p_gl + lruw
    npad = p_dt + M2_GROUPS * LANE
    hpg = heads // M2_GROUPS
    assert hpg == 4 and M2_STATE == 64 and M2_HEAD_DIM == 64, "SSD kernel packs two 64-wide heads per lane tile"
    perm = np.full((npad,), d_in, np.int32)
    perm[p_z:p_z + inner] = o_z + np.arange(inner)
    perm[p_g:p_g + N_BRANCH * d] = o_g + np.arange(N_BRANCH * d)
    perm[p_xm:p_xm + inner] = o_xbc + np.arange(inner)
    bcp = np.zeros((M2_GROUPS * LANE,), np.int32)
    for g in range(M2_GROUPS):
        bcp[g * LANE:g * LANE + M2_STATE] = inner + gn + g * M2_STATE + np.arange(M2_STATE)
        bcp[g * LANE + M2_STATE:(g + 1) * LANE] = inner + g * M2_STATE + np.arange(M2_STATE)
    perm[p_bc:p_bc + M2_GROUPS * LANE] = o_xbc + bcp
    perm[p_u:p_u + s5w] = o_u + np.arange(s5w)
    perm[p_xl:p_xl + lruw] = o_xl + np.arange(lruw)
    perm[p_gl:p_gl + lruw] = o_gl + np.arange(lruw)
    for g in range(M2_GROUPS):
        for dd in range(2):
            for j in range(hpg):
                perm[p_dt + g * LANE + dd * hpg + j] = o_dt + dd * heads + g * hpg + j
    return dict(s5w=s5w, inner=inner, lruw=lruw, heads=heads, hpg=hpg, perm=perm, bcp=bcp, npad=npad,
                p_z=p_z, p_g=p_g, p_xm=p_xm, p_bc=p_bc, p_u=p_u, p_xl=p_xl, p_gl=p_gl, p_dt=p_dt)


def _rows8(rows, width):
    out = jnp.zeros((SUBLANE, width), F32)
    for i, r in enumerate(rows):
        out = out.at[i, :r.shape[0]].set(r.astype(F32))
    return out


def _pack_layer(lay, p, q_lat, q_ctx):
    d = p["w_in"].shape[0]
    inner, lruw, hpg, heads = lay["inner"], lay["lruw"], lay["hpg"], lay["heads"]
    lw = {}
    w_ext = jnp.concatenate([p["w_in"], jnp.zeros((d, 1), p["w_in"].dtype)], axis=1)
    lw["w_in"] = w_ext[:, lay["perm"]].astype(BF16)
    lw["s5_lat"] = _s5_weights(p["s5_lam_re"], p["s5_lam_im"], p["s5_log_step"], p["s5_b_re"], p["s5_b_im"],
                               p["s5_c_re"], p["s5_c_im"], q_lat)
    lw["s5_ctx"] = lw["s5_lat"] if q_ctx == q_lat else _s5_weights(
        p["s5_lam_re"], p["s5_lam_im"], p["s5_log_step"], p["s5_b_re"], p["s5_b_im"], p["s5_c_re"], p["s5_c_im"], q_ctx)
    gw = inner // M2_GROUPS
    cw, cb = p["m2_conv_w"], p["m2_conv_b"]
    lw["m2_cwx"] = jnp.stack([_rows8([cw[k, g * gw:(g + 1) * gw] for k in range(CONV_W)] + [cb[g * gw:(g + 1) * gw]], gw)
                              for g in range(M2_GROUPS)])
    cwb, cbb = cw[:, lay["bcp"]], cb[lay["bcp"]]
    lw["m2_cwb"] = jnp.stack([_rows8([cwb[k, g * LANE:(g + 1) * LANE] for k in range(CONV_W)] + [cbb[g * LANE:(g + 1) * LANE]], LANE)
                              for g in range(M2_GROUPS)])
    dtb = p["m2_dt_bias"].reshape(2, M2_GROUPS, hpg)
    alog = p["m2_a_log"].reshape(2, M2_GROUPS, hpg)
    dsk = p["m2_d"].reshape(M2_GROUPS, hpg)
    lw["m2_pp"] = jnp.stack([_rows8([dtb[:, g].reshape(-1), alog[:, g].reshape(-1), dsk[g]], LANE) for g in range(M2_GROUPS)])
    blk = lruw // LRU_BLOCKS
    eye = jnp.eye(LRU_BLOCKS, dtype=F32)

    def dense(w):
        return jnp.einsum('hij,hk->hikj', w.astype(F32), eye).reshape(lruw, lruw)

    ncb = lruw // LANE
    assert LANE % blk == 0
    mats = [dense(p["lru_w_a"][0]), dense(p["lru_w_x"][0]), dense(p["lru_w_a"][1]), dense(p["lru_w_x"][1])]
    lw["lru_w"] = jnp.stack([jnp.concatenate([m[c * LANE:(c + 1) * LANE, c * LANE:(c + 1) * LANE] for m in mats], axis=1)
                             for c in range(ncb)]).astype(BF16)
    sl = lambda v, c: v[c * LANE:(c + 1) * LANE]
    pps = []
    for c in range(ncb):
        bias = jnp.concatenate([sl(p["lru_b_a"][0], c), sl(p["lru_b_x"][0], c), sl(p["lru_b_a"][1], c), sl(p["lru_b_x"][1], c)])
        lam = jnp.concatenate([sl(p["lru_lam"][0], c), sl(p["lru_lam"][1], c)])
        pps.append(_rows8([bias, lam, sl(p["lru_conv_b"], c)] + [sl(p["lru_conv_w"][k], c) for k in range(CONV_W)], 4 * LANE))
    lw["lru_pp"] = jnp.stack(pps)
    lw["s5_d"] = p["s5_d"].reshape(1, -1).astype(F32)
    lw["s5_w_glu"] = p["s5_w_glu"].astype(BF16)
    lw["m2_norm_g"] = p["m2_norm_g"].reshape(1, -1).astype(F32)
    lw["m2_w_out"] = p["m2_w_out"].astype(BF16)
    lw["lru_w_out"] = p["lru_w_out"].astype(BF16)
    lw["w_o"] = p["w_o"].astype(BF16)
    lw["norm2_g"] = p["norm2_g"].reshape(1, -1).astype(F32)
    lw["w_router"] = jnp.pad(p["moe_w_router"].astype(F32), ((0, 0), (0, LANE - N_EXPERTS)))
    lw["moe_w1"] = p["moe_w1"].astype(BF16)
    lw["moe_w3"] = p["moe_w3"].astype(BF16)
    lw["moe_w2"] = p["moe_w2"].astype(BF16)
    return lw


def _mixer(x2d, bs, seq, colmajor, norm_g, shift, scale, lw, lay, h0, s5w):
    proj = _inproj(x2d, seq, norm_g, shift, scale, lw["w_in"])
    y5, f5 = _s5_branch(proj, bs, seq, colmajor, s5w, h0[0], (lay["p_u"], lay["s5w"]))
    ys, fm = _ssd_branch(proj, bs, seq, lw, h0[1], (lay["p_xm"], lay["p_bc"], lay["p_dt"], lay["inner"]))
    hy, fl = _lru_branch(proj, bs, seq, lw, h0[2], (lay["p_xl"], lay["p_gl"], lay["lruw"]))
    return proj, y5, ys, hy, (f5, fm, fl)


def kernel(x, c, ctx, c_ctx, w_mod, b_mod, norm1_g, norm2_g, w_in, s5_lam_re, s5_lam_im, s5_log_step, s5_b_re, s5_b_im, s5_c_re, s5_c_im, s5_d, s5_w_glu, m2_conv_w, m2_conv_b, m2_dt_bias, m2_a_log, m2_d, m2_norm_g, m2_w_out, lru_conv_w, lru_conv_b, lru_w_a, lru_b_a, lru_w_x, lru_b_x, lru_lam, lru_w_out, w_o, moe_w_router, moe_w1, moe_w3, moe_w2, final_norm_g):
    bsz, seq, d = x.shape
    cl = ctx.shape[1]
    depth = w_mod.shape[0]
    lay = _layout(d)
    stacked = dict(norm2_g=norm2_g, w_in=w_in, s5_lam_re=s5_lam_re, s5_lam_im=s5_lam_im, s5_log_step=s5_log_step,
                   s5_b_re=s5_b_re, s5_b_im=s5_b_im, s5_c_re=s5_c_re, s5_c_im=s5_c_im, s5_d=s5_d, s5_w_glu=s5_w_glu,
                   m2_conv_w=m2_conv_w, m2_conv_b=m2_conv_b, m2_dt_bias=m2_dt_bias, m2_a_log=m2_a_log, m2_d=m2_d,
                   m2_norm_g=m2_norm_g, m2_w_out=m2_w_out, lru_conv_w=lru_conv_w, lru_conv_b=lru_conv_b,
                   lru_w_a=lru_w_a, lru_b_a=lru_b_a, lru_w_x=lru_w_x, lru_b_x=lru_b_x, lru_lam=lru_lam,
                   lru_w_out=lru_w_out, w_o=w_o, moe_w_router=moe_w_router, moe_w1=moe_w1, moe_w3=moe_w3, moe_w2=moe_w2)
    rm = -(-(bsz + 1) // SUBLANE) * SUBLANE
    c_rows = jnp.zeros((rm, d), F32).at[:bsz].set(c.astype(F32)).at[bsz].set(c_ctx.astype(F32))
    mods = _modulation(c_rows, w_mod.astype(F32), b_mod.astype(F32))

    groups = lay["s5w"] // S5_GROUP
    ncb = lay["lruw"] // LANE
    npair = lay["inner"] // M2_GROUPS // LANE
    zero_h0 = (jnp.zeros((groups, 2, bsz, LANE), F32),
               jnp.zeros((bsz, M2_GROUPS, 2, npair, M2_STATE, LANE), F32),
               jnp.zeros((bsz, ncb, 2, LANE), F32))

    xs = x.reshape(bsz * seq, d).astype(F32)
    cs = ctx.reshape(bsz * cl, d).astype(F32)
    q_lat, q_ctx = min(S5_CHUNK, seq), min(S5_CHUNK, cl)
    packed = jax.vmap(lambda p: _pack_layer(lay, p, q_lat, q_ctx))(stacked)
    for i in range(depth):
        lw = jax.tree_util.tree_map(lambda v: v[i], packed)
        mx = [mods[i, :bsz, k * d:(k + 1) * d].reshape(bsz, 1, d) for k in range(6)]
        mc = [mods[i, bsz:bsz + 1, k * d:(k + 1) * d].reshape(1, 1, d) for k in range(6)]
        fcols = (lay["p_z"], lay["p_g"], lay["p_u"], lay["s5w"])
        cproj, cy5, cys, chy, cstates = _mixer(cs, bsz, cl, False, norm1_g[i], mc[0], mc[1], lw, lay, zero_h0, lw["s5_ctx"])
        if i < depth - 1:
            c1, ch2, caff = _final(cs, cproj, cy5, cys, chy, cl, (mc[2], mc[3], mc[4]), lw, fcols)
            cs = _moe(c1, ch2, caff, bsz, cl, mc[5], lw, final_norm_g, False)
        xproj, y5, ys, hy, _ = _mixer(xs, bsz, seq, True, norm1_g[i], mx[0], mx[1], lw, lay, cstates, lw["s5_lat"])
        x1, h2, aff = _final(xs, xproj, y5, ys, hy, seq, (mx[2], mx[3], mx[4]), lw, fcols)
        xs = _moe(x1, h2, aff, bsz, seq, mx[5], lw, final_norm_g, i == depth - 1)
    return xs.reshape(bsz, seq, d).astype(x.dtype)
```

```python
import functools
import math

import numpy as np
import jax
import jax.numpy as jnp
from jax import lax
from jax.experimental import pallas as pl
from jax.experimental.pallas import tpu as pltpu

F32 = jnp.float32
BF16 = jnp.bfloat16
I32 = jnp.int32

GRID_W = 64
EPS = 1e-6
CONV_W = 4
S5_GROUP = 16
S5_STATE = 64
M2_HEAD_DIM = 64
M2_GROUPS = 4
M2_STATE = 64
M2_CHUNK = 128
LRU_BLOCKS = 8
LRU_C = 8.0
N_EXPERTS = 16
CAPACITY = 2
N_BRANCH = 3

LANE = 128
SUBLANE = 8
S5_CHUNK = 128
LRU_TILE = 64
VMEM_LIMIT = 56 * 1024 * 1024


def _cparams(sem):
    return pltpu.CompilerParams(dimension_semantics=sem, vmem_limit_bytes=VMEM_LIMIT)


def _dot(a, b):
    return jnp.dot(a.astype(BF16), b.astype(BF16), preferred_element_type=F32)


def _dot_nt(a, b):
    return lax.dot_general(a.astype(BF16), b.astype(BF16), (((1,), (1,)), ((), ())), preferred_element_type=F32)


def _split2(a):
    hi = a.astype(BF16)
    lo = (a - hi.astype(F32)).astype(BF16)
    return hi, lo


def _split3(a):
    hi = a.astype(BF16)
    r = a - hi.astype(F32)
    mid = r.astype(BF16)
    lo = (r - mid.astype(F32)).astype(BF16)
    return hi, mid, lo


def _dot3(a, b):
    ah, al = _split2(a)
    bh, bl = _split2(b)
    d = lambda x, y: jnp.dot(x, y, preferred_element_type=F32)
    return d(ah, bh) + (d(ah, bl) + d(al, bh))


def _dot_exact_lhs(m01, x):
    hi, mid, lo = _split3(x)
    d = lambda y: jnp.dot(m01, y, preferred_element_type=F32)
    return d(hi) + (d(mid) + d(lo))


def _silu(x):
    return x * jax.nn.sigmoid(x)


def _softplus(x):
    return jnp.maximum(x, 0.0) + jnp.log(1.0 + jnp.exp(-jnp.abs(x)))


def _rms_scale(x):
    return x * lax.rsqrt(jnp.mean(x * x, axis=-1, keepdims=True) + EPS)


def _mod_kernel(c_ref, w_ref, b_ref, o_ref):
    o_ref[...] = _dot3(_silu(c_ref[...]), w_ref[...]) + b_ref[...]


def _modulation(c_rows, w_mod, b_mod):
    depth, d, n6 = w_mod.shape
    rm = c_rows.shape[0]
    tn = min(1024, n6)
    return pl.pallas_call(
        _mod_kernel,
        out_shape=jax.ShapeDtypeStruct((depth, rm, n6), F32),
        grid=(depth, n6 // tn),
        in_specs=[pl.BlockSpec((rm, d), lambda l, j: (0, 0)),
                  pl.BlockSpec((None, d, tn), lambda l, j: (l, 0, j)),
                  pl.BlockSpec((None, 1, tn), lambda l, j: (l, 0, j))],
        out_specs=pl.BlockSpec((None, rm, tn), lambda l, j: (l, 0, j)),
        compiler_params=_cparams(("arbitrary", "arbitrary")),
        name="modulation",
    )(c_rows, w_mod, b_mod.reshape(depth, 1, n6))


def _inproj_kernel(x_ref, g_ref, sh_ref, sc_ref, w_ref, o_ref, *, nchunk):
    h = _rms_scale(x_ref[...]) * g_ref[...] * (1.0 + sc_ref[...]) + sh_ref[...]
    hb = h.astype(BF16)
    npad = o_ref.shape[1]
    for n0 in range(0, npad, nchunk):
        o_ref[:, n0:n0 + nchunk] = jnp.dot(hb, w_ref[:, n0:n0 + nchunk], preferred_element_type=F32)


def _mod_index(bm, tm, seq):
    if bm == 1:
        return lambda i: (0, 0, 0)
    return lambda i: ((i * tm) // seq, 0, 0)


def _inproj(x2d, seq, g, shift, scale, w):
    t, d = x2d.shape
    npad = w.shape[1]
    tm = min(256, seq)
    mi = _mod_index(shift.shape[0], tm, seq)
    return pl.pallas_call(
        functools.partial(_inproj_kernel, nchunk=512),
        out_shape=jax.ShapeDtypeStruct((t, npad), F32),
        grid=(t // tm,),
        in_specs=[pl.BlockSpec((tm, d), lambda i: (i, 0)),
                  pl.BlockSpec((1, d), lambda i: (0, 0)),
                  pl.BlockSpec((None, 1, d), mi),
                  pl.BlockSpec((None, 1, d), mi),
                  pl.BlockSpec((d, npad), lambda i: (0, 0), pipeline_mode=pl.Buffered(1))],
        out_specs=pl.BlockSpec((tm, npad), lambda i: (i, 0)),
        compiler_params=_cparams(("arbitrary",)),
        name="inproj",
    )(x2d, g.reshape(1, d), shift, scale, w)


def _build_toeplitz(cp_ref, cn_ref, t_ref, q):
    trow = lax.broadcasted_iota(I32, (q, q), 0)
    tcol = lax.broadcasted_iota(I32, (q, q), 1)
    causal = tcol >= trow

    def body(j, carry):
        cp = cp_ref[j]
        cn = cn_ref[j]
        r0 = pl.multiple_of(j * q, q)
        for i in range(S5_GROUP):
            a = pltpu.roll(jnp.broadcast_to(cp[i:i + 1, :], (q, q)), 0, 1, stride=1, stride_axis=0)
            b = pltpu.roll(jnp.broadcast_to(cn[i:i + 1, :], (q, q)), 0, 1, stride=1, stride_axis=0)
            t_ref[pl.ds(r0, q), i * q:(i + 1) * q] = jnp.where(causal, a, b).astype(BF16)
        return carry

    lax.fori_loop(0, S5_GROUP, body, 0)


def _s5_kernel(u_ref, cp_ref, cn_ref, wst_ref, wout_ref, dec_ref, h0_ref, y_ref, fin_ref, ef_ref, eb_ref, t_ref, *, nc, bs):
    q = u_ref.shape[2]
    _build_toeplitz(cp_ref, cn_ref, t_ref, q)
    a = jnp.concatenate([u_ref[j] for j in range(S5_GROUP)], axis=-1).astype(BF16)
    loc = jnp.dot(a, wst_ref[...], preferred_element_type=F32)
    dec = dec_ref[...]
    half = S5_STATE

    def step(cur, da, db, add):
        return da * cur + db * pltpu.roll(cur, half, 1) + add

    cur = h0_ref[0]
    for c in range(nc):
        ef_ref[c * bs:(c + 1) * bs, :] = cur
        cur = step(cur, dec[0:1], dec[1:2], loc[c * bs:(c + 1) * bs, 0:LANE])
    fin_ref[0] = cur
    cur = h0_ref[1]
    for c in reversed(range(nc)):
        eb_ref[c * bs:(c + 1) * bs, :] = cur
        cur = step(cur, dec[2:3], dec[3:4], loc[c * bs:(c + 1) * bs, LANE:2 * LANE])
    fin_ref[1] = cur
    e = jnp.concatenate([ef_ref[...], eb_ref[...]], axis=-1).astype(BF16)
    acc = jnp.dot(a, t_ref[...], preferred_element_type=F32) + jnp.dot(e, wout_ref[...], preferred_element_type=F32)
    for i in range(S5_GROUP):
        y_ref[i] = acc[:, i * q:(i + 1) * q]


def _s5_scan(ut, s5w, h0, nc, bs):
    cpos, cneg, wst, wout, dec = s5w
    groups = cpos.shape[0]
    r, q = ut.shape[1], ut.shape[2]
    assert q == LANE, "the Toeplitz builder rotates one 128-lane tile per block"
    kq = S5_GROUP * q
    lagspec = pl.BlockSpec((None, S5_GROUP, S5_GROUP, q), lambda g: (g, 0, 0, 0))
    return pl.pallas_call(
        functools.partial(_s5_kernel, nc=nc, bs=bs),
        out_shape=(jax.ShapeDtypeStruct(ut.shape, F32), jax.ShapeDtypeStruct((groups, 2, bs, LANE), F32)),
        grid=(groups,),
        in_specs=[pl.BlockSpec((S5_GROUP, r, q), lambda g: (g, 0, 0)),
                  lagspec, lagspec,
                  pl.BlockSpec((None, kq, 2 * LANE), lambda g: (g, 0, 0)),
                  pl.BlockSpec((None, 2 * LANE, kq), lambda g: (g, 0, 0)),
                  pl.BlockSpec((None, SUBLANE, LANE), lambda g: (g, 0, 0)),
                  pl.BlockSpec((None, 2, bs, LANE), lambda g: (g, 0, 0, 0))],
        out_specs=(pl.BlockSpec((S5_GROUP, r, q), lambda g: (g, 0, 0)),
                   pl.BlockSpec((None, 2, bs, LANE), lambda g: (g, 0, 0, 0))),
        scratch_shapes=[pltpu.VMEM((r, LANE), F32), pltpu.VMEM((r, LANE), F32), pltpu.VMEM((kq, kq), BF16)],
        compiler_params=_cparams(("arbitrary",)),
        name="s5_scan",
    )(ut, cpos, cneg, wst, wout, dec, h0)


def _s5_weights(lam_re, lam_im, log_step, b_re, b_im, c_re, c_im, q):
    hp = lax.Precision.HIGHEST
    g, p = lam_re.shape[1], lam_re.shape[2]
    ii = S5_GROUP
    tau = jnp.arange(q + 1, dtype=F32)[:, None, None]
    ks, wsts, wouts, decs = [], [], [], []
    for d in range(2):
        lr, li = lam_re[d].astype(F32), lam_im[d].astype(F32)
        step = jnp.exp(log_step[d].astype(F32))[:, None]
        ar, ai = lr * step, li * step
        mag = jnp.exp(tau * ar)
        pre, pim = mag * jnp.cos(tau * ai), mag * jnp.sin(tau * ai)
        bar_re, bar_im = pre[1], pim[1]
        den = lr * lr + li * li
        nr, ni = bar_re - 1.0, bar_im
        coef_re = (nr * lr + ni * li) / den
        coef_im = (ni * lr - nr * li) / den
        bre, bim = b_re[d].astype(F32), b_im[d].astype(F32)
        bb_re = coef_re[..., None] * bre - coef_im[..., None] * bim
        bb_im = coef_re[..., None] * bim + coef_im[..., None] * bre
        cre, cim = c_re[d].astype(F32), c_im[d].astype(F32)
        cp_re = cre[None] * pre[:, :, None, :] - cim[None] * pim[:, :, None, :]
        cp_im = cre[None] * pim[:, :, None, :] + cim[None] * pre[:, :, None, :]
        k = (jnp.einsum('tgip,gpj->tgij', cp_re[:q], bb_re, precision=hp)
             - jnp.einsum('tgip,gpj->tgij', cp_im[:q], bb_im, precision=hp))
        ks.append(k)
        pw_re = pre[q - 1::-1][:q] if d == 0 else pre[:q]
        pw_im = pim[q - 1::-1][:q] if d == 0 else pim[:q]
        w_re = pw_re[:, :, :, None] * bb_re[None] - pw_im[:, :, :, None] * bb_im[None]
        w_im = pw_re[:, :, :, None] * bb_im[None] + pw_im[:, :, :, None] * bb_re[None]
        w = jnp.concatenate([w_re, w_im], axis=2)
        wsts.append(jnp.transpose(w, (1, 3, 0, 2)).reshape(g, ii * q, 2 * p))
        if d == 0:
            o_re, o_im = cp_re[1:q + 1], cp_im[1:q + 1]
        else:
            o_re, o_im = cp_re[q:0:-1], cp_im[q:0:-1]
        o = jnp.concatenate([o_re, -o_im], axis=3)
        wouts.append(jnp.transpose(o, (1, 3, 2, 0)).reshape(g, 2 * p, ii * q))
        dr, di = pre[q], pim[q]
        decs.append(jnp.concatenate([dr, dr], axis=-1))
        decs.append(jnp.concatenate([-di, di], axis=-1))
    cpos = jnp.transpose(ks[0].at[0].add(ks[1][0]), (1, 3, 2, 0))
    kb_rev = ks[1][::-1]
    cneg = jnp.concatenate([jnp.zeros_like(kb_rev[:1]), kb_rev[:q - 1]], axis=0)
    cneg = jnp.transpose(cneg, (1, 3, 2, 0))
    wst = jnp.concatenate(wsts, axis=-1).astype(BF16)
    wout = jnp.concatenate(wouts, axis=1).astype(BF16)
    dec = jnp.stack(decs + decs, axis=1)
    return cpos, cneg, wst, wout, dec


def _s5_in_kernel(x_ref, o_ref, *, cpc, colmajor):
    q = o_ref.shape[3]
    for b in range(x_ref.shape[0]):
        for cc in range(o_ref.shape[1]):
            if colmajor:
                x = jnp.concatenate([x_ref[b, :, cc * cpc + w, :] for w in range(cpc)], axis=0)
            else:
                x = x_ref[b, cc * q:(cc + 1) * q, :]
            o_ref[:, cc, b, :] = x.T


def _s5_out_kernel(y_ref, o_ref, *, cpc, colmajor):
    q = y_ref.shape[3]
    for b in range(o_ref.shape[0]):
        for cc in range(y_ref.shape[1]):
            t = y_ref[:, cc, b, :].T
            if colmajor:
                rows = q // cpc
                for w in range(cpc):
                    o_ref[b, :, cc * cpc + w, :] = t[w * rows:(w + 1) * rows]
            else:
                o_ref[b, cc * q:(cc + 1) * q, :] = t


def _s5_layout(bs, seq, colmajor, q):
    bb = min(SUBLANE, bs)
    while bs % bb:
        bb -= 1
    nc = seq // q
    if not colmajor:
        return bb, nc, 1, 1, None
    rows = seq // GRID_W
    assert q % rows == 0, "a scan chunk must cover whole grid columns"
    cpc = q // rows
    wb = max(SUBLANE, cpc)
    return bb, wb // cpc, GRID_W // wb, cpc, (rows, wb)


def _s5_branch(proj, bs, seq, colmajor, s5w, h0, cols):
    p_u, width = cols
    q = min(S5_CHUNK, seq)
    nc = seq // q
    bb, nch, nj, cpc, cm = _s5_layout(bs, seq, colmajor, q)
    cb = p_u // width
    cmaj = jax.ShapeDtypeStruct((width, nc, bs, q), F32)
    cspec = pl.BlockSpec((width, nch, bb, q), lambda i, j: (0, j, i, 0))
    if colmajor:
        rows, wb = cm
        tok_in = pl.BlockSpec((bb, rows, wb, width), lambda i, j: (i, 0, j, cb))
        tok_out = pl.BlockSpec((bb, rows, wb, width), lambda i, j: (i, 0, j, 0))
        x_in = proj.reshape(bs, rows, GRID_W, proj.shape[1])
        tok_shape = jax.ShapeDtypeStruct((bs, rows, GRID_W, width), F32)
    else:
        tok_in = pl.BlockSpec((bb, seq, width), lambda i, j: (i, 0, cb))
        tok_out = pl.BlockSpec((bb, seq, width), lambda i, j: (i, 0, 0))
        x_in = proj.reshape(bs, seq, proj.shape[1])
        tok_shape = jax.ShapeDtypeStruct((bs, seq, width), F32)
    ut = pl.pallas_call(
        functools.partial(_s5_in_kernel, cpc=cpc, colmajor=colmajor), out_shape=cmaj, grid=(bs // bb, nj),
        in_specs=[tok_in], out_specs=cspec, compiler_params=_cparams(("arbitrary", "arbitrary")), name="s5_to_channel_major",
    )(x_in)
    yt, fin = _s5_scan(ut.reshape(width, nc * bs, q), s5w, h0, nc, bs)
    y = pl.pallas_call(
        functools.partial(_s5_out_kernel, cpc=cpc, colmajor=colmajor), out_shape=tok_shape, grid=(bs // bb, nj),
        in_specs=[cspec], out_specs=tok_out, compiler_params=_cparams(("arbitrary", "arbitrary")), name="s5_to_token_major",
    )(yt.reshape(width, nc, bs, q))
    return y.reshape(bs * seq, width), fin


def _conv_rows(pad_ref, base, n, taps, bias):
    w = pad_ref[pl.ds(base, n + 2 * SUBLANE), :]
    tot = n + 2 * SUBLANE
    xm2 = pltpu.roll(w, 2, 0)[SUBLANE:SUBLANE + n]
    xm1 = pltpu.roll(w, 1, 0)[SUBLANE:SUBLANE + n]
    x0 = w[SUBLANE:SUBLANE + n]
    xp1 = pltpu.roll(w, tot - 1, 0)[SUBLANE:SUBLANE + n]
    return taps[0:1] * xm2 + taps[1:2] * xm1 + taps[2:3] * x0 + taps[3:4] * xp1 + bias


def _fill_padded(pad_ref, x_ref, seq):
    zeros = jnp.zeros((SUBLANE, pad_ref.shape[1]), F32)
    pad_ref[0:SUBLANE, :] = zeros
    pad_ref[SUBLANE + seq:2 * SUBLANE + seq, :] = zeros
    pad_ref[SUBLANE:SUBLANE + seq, :] = x_ref[...]


def _scan_tile(a, v, h, reverse):
    s = a.shape[0]
    row = lax.broadcasted_iota(I32, a.shape, 0)
    k = 1
    while k < s:
        sh = s - k if reverse else k
        ok = (row < s - k) if reverse else (row >= k)
        a_sh = pltpu.roll(a, sh, 0)
        v_sh = pltpu.roll(v, sh, 0)
        v = v + a * jnp.where(ok, v_sh, 0.0)
        a = a * jnp.where(ok, a_sh, 1.0)
        k *= 2
    return v + a * h


def _lru_kernel(x_ref, g_ref, w_ref, pp_ref, h0_ref, o_ref, fin_ref, pad_ref, a0_ref, v0_ref, a1_ref, v1_ref, *, seq, ch):
    pp = pp_ref[...]
    bias = pp[0:1]
    lam = pp[1:2]
    cb = pp[2:3, 0:LANE]
    taps = pp[3:7, 0:LANE]
    _fill_padded(pad_ref, x_ref, seq)
    av = ((a0_ref, v0_ref), (a1_ref, v1_ref))

    def gates(i, carry):
        base = pl.multiple_of(i * ch, ch)
        xc = _conv_rows(pad_ref, base, ch, taps, cb)
        gt = _dot(xc, w_ref[...]) + bias
        for d in range(2):
            r = jax.nn.sigmoid(gt[:, 2 * d * LANE:(2 * d + 1) * LANE])
            ig = jax.nn.sigmoid(gt[:, (2 * d + 1) * LANE:(2 * d + 2) * LANE])
            log_a = -LRU_C * r * _softplus(-lam[:, d * LANE:(d + 1) * LANE])
            av[d][0][pl.ds(base, ch), :] = jnp.exp(log_a)
            av[d][1][pl.ds(base, ch), :] = jnp.sqrt(jnp.maximum(1.0 - jnp.exp(2.0 * log_a), EPS)) * (ig * xc)
        return carry

    lax.fori_loop(0, seq // ch, gates, 0)
    tile = min(LRU_TILE, seq)
    nt = seq // tile

    def fwd(i, h):
        r0 = pl.multiple_of(i * tile, tile)
        hall = _scan_tile(a0_ref[pl.ds(r0, tile), :], v0_ref[pl.ds(r0, tile), :], h, False)
        o_ref[pl.ds(r0, tile), :] = hall
        return hall[tile - 1:tile, :]

    h0 = h0_ref[...]
    hf = lax.fori_loop(0, nt, fwd, h0[0:1])

    def bwd(i, h):
        r0 = pl.multiple_of((nt - 1 - i) * tile, tile)
        hall = _scan_tile(a1_ref[pl.ds(r0, tile), :], v1_ref[pl.ds(r0, tile), :], h, True)
        o_ref[pl.ds(r0, tile), :] = (o_ref[pl.ds(r0, tile), :] + hall) * jax.nn.gelu(g_ref[pl.ds(r0, tile), :])
        return hall[0:1, :]

    hb = lax.fori_loop(0, nt, bwd, h0[1:2])
    fin_ref[...] = jnp.concatenate([hf, hb], axis=0)


def _lru_branch(proj, bs, seq, lw, h0, cols):
    p_xl, p_gl, width = cols
    ncb = width // LANE
    bx, bg = p_xl // LANE, p_gl // LANE
    ch = min(256, seq)
    sc = pltpu.VMEM((seq, LANE), F32)
    hy, fin = pl.pallas_call(
        functools.partial(_lru_kernel, seq=seq, ch=ch),
        out_shape=(jax.ShapeDtypeStruct((bs * seq, width), F32), jax.ShapeDtypeStruct((bs, ncb, 2, LANE), F32)),
        grid=(bs, ncb),
        in_specs=[pl.BlockSpec((seq, LANE), lambda b, c: (b, bx + c)),
                  pl.BlockSpec((seq, LANE), lambda b, c: (b, bg + c)),
                  pl.BlockSpec((None, LANE, 4 * LANE), lambda b, c: (c, 0, 0)),
                  pl.BlockSpec((None, SUBLANE, 4 * LANE), lambda b, c: (c, 0, 0)),
                  pl.BlockSpec((None, None, 2, LANE), lambda b, c: (b, c, 0, 0))],
        out_specs=(pl.BlockSpec((seq, LANE), lambda b, c: (b, c)),
                   pl.BlockSpec((None, None, 2, LANE), lambda b, c: (b, c, 0, 0))),
        scratch_shapes=[pltpu.VMEM((seq + 2 * SUBLANE, LANE), F32), sc, sc, sc, sc],
        compiler_params=_cparams(("arbitrary", "arbitrary")),
        name="rglru",
    )(proj, proj, lw["lru_w"], lw["lru_pp"], h0)
    return hy, fin


def _ssd_kernel(xm_ref, bc_ref, dt_ref, cwx_ref, cwb_ref, pp_ref, h0_ref, y_ref, fin_ref,
                xpad_ref, bpad_ref, xa_ref, ba_ref, ex_ref, cb_ref, ext_ref, tot_ref,
                sf_ref, sb_ref, df_ref, db_ref, *, seq):
    cq = M2_CHUNK
    nc = seq // cq
    hd = M2_HEAD_DIM
    ns = M2_STATE
    npair = xm_ref.shape[1] // LANE
    hpg = 2 * npair
    pp = pp_ref[...]
    dt_bias, a_neg, dskip = pp[0:1], -jnp.exp(pp[1:2]), pp[2:3]
    cwx, cwb = cwx_ref[...], cwb_ref[...]
    _fill_padded(xpad_ref, xm_ref, seq)
    _fill_padded(bpad_ref, bc_ref, seq)

    rowi = lax.broadcasted_iota(I32, (cq, cq), 0)
    coli = lax.broadcasted_iota(I32, (cq, cq), 1)
    tri = (coli <= rowi).astype(BF16)
    lane = lax.broadcasted_iota(I32, (cq, LANE), 1)
    first_half = lane < hd
    first_half_s = lax.broadcasted_iota(I32, (ns, LANE), 1) < hd
    first_half1 = lax.broadcasted_iota(I32, (1, LANE), 1) < hd
    srefs, drefs = (sf_ref, sb_ref), (df_ref, db_ref)

    def local(c, carry):
        r0 = pl.multiple_of(c * cq, cq)
        r8 = pl.multiple_of(c * SUBLANE, SUBLANE)
        xa_all = _silu(_conv_rows(xpad_ref, r0, cq, cwx[0:4], cwx[4:5]))
        xa_ref[pl.ds(r0, cq), :] = xa_all
        bc = _silu(_conv_rows(bpad_ref, r0, cq, cwb[0:4], cwb[4:5]))
        ba_ref[pl.ds(r0, cq), :] = bc
        dtv = _softplus(dt_ref[pl.ds(r0, cq), :] + dt_bias)
        la = dtv * a_neg
        cum = _dot_exact_lhs(tri, la)
        tot = cum[cq - 1:cq, :]
        ex = jnp.where(lane < hpg, cum, cum - la)
        ex_ref[pl.ds(r0, cq), :] = ex
        ext = ex.T[0:SUBLANE, :]
        dtt = dtv.T[0:SUBLANE, :]
        ldt = jnp.log(dtt)
        fwd_rows = lax.broadcasted_iota(I32, (SUBLANE, LANE), 0) < hpg
        ext_ref[pl.ds(r8, SUBLANE), :] = jnp.where(fwd_rows, ext - ldt, ext + ldt)
        tot_ref[pl.ds(r8, SUBLANE), :] = jnp.broadcast_to(tot, (SUBLANE, LANE))
        bt = bc.T[ns:2 * ns, :]
        cb_ref[pl.ds(r0, cq), :] = _dot(bc[:, 0:ns], bt)
        for d in range(2):
            for pr in range(npair):
                xa = xa_all[:, pr * LANE:(pr + 1) * LANE]
                rs, ts = [], []
                for hh in range(2):
                    col = d * hpg + 2 * pr + hh
                    th = tot[:, col:col + 1]
                    erow = ext[col:col + 1, :]
                    din = jnp.exp(th - erow) if d == 0 else jnp.exp(erow)
                    rs.append(_dot(bt * (din * dtt[col:col + 1, :]), xa))
                    ts.append(jnp.exp(th))
                srefs[d][c, pr] = jnp.where(first_half_s, rs[0], rs[1])
                r8p = pl.multiple_of((c * npair + pr) * SUBLANE, SUBLANE)
                drefs[d][pl.ds(r8p, SUBLANE), :] = jnp.broadcast_to(jnp.where(first_half1, ts[0], ts[1]), (SUBLANE, LANE))
        return carry

    lax.fori_loop(0, nc, local, 0, unroll=4 if nc % 4 == 0 else 1)

    def recur(d):
        def step(i, cur):
            c = i if d == 0 else nc - 1 - i
            out = []
            for pr in range(npair):
                r8p = pl.multiple_of((c * npair + pr) * SUBLANE, SUBLANE)
                loc = srefs[d][c, pr]
                srefs[d][c, pr] = cur[pr]
                out.append(drefs[d][pl.ds(r8p, 1), :] * cur[pr] + loc)
            return tuple(out)
        return lax.fori_loop(0, nc, step, tuple(h0_ref[d, pr] for pr in range(npair)))

    for d in range(2):
        fin = recur(d)
        for pr in range(npair):
            fin_ref[d, pr] = fin[pr]

    lower = coli <= rowi
    upper = coli >= rowi

    def output(c, carry):
        r0 = pl.multiple_of(c * cq, cq)
        r8 = pl.multiple_of(c * SUBLANE, SUBLANE)
        ex = ex_ref[pl.ds(r0, cq), :]
        ext = ext_ref[pl.ds(r8, SUBLANE), :]
        tot = tot_ref[pl.ds(r8, 1), :]
        cb = cb_ref[pl.ds(r0, cq), :]
        cm = ba_ref[pl.ds(r0, cq), 0:ns]
        for pr in range(npair):
            lanes = slice(pr * LANE, (pr + 1) * LANE)
            xa = xa_ref[pl.ds(r0, cq), lanes]
            so = _dot(cm, jnp.concatenate([sf_ref[c, pr], sb_ref[c, pr]], axis=1))
            dsk = jnp.where(first_half1, dskip[:, 2 * pr:2 * pr + 1], dskip[:, 2 * pr + 1:2 * pr + 2])
            ydiag, ecfs, ecbs = [], [], []
            for hh in range(2):
                cf, cbk = 2 * pr + hh, hpg + 2 * pr + hh
                ecf = jnp.broadcast_to(ex[:, cf:cf + 1], (cq, cq))
                ecb = jnp.broadcast_to(ex[:, cbk:cbk + 1], (cq, cq))
                lf = jnp.where(lower, jnp.exp(ecf - ext[cf:cf + 1, :]), 0.0)
                lb = jnp.where(upper, jnp.exp(ext[cbk:cbk + 1, :] - ecb), 0.0)
                ydiag.append(_dot(cb * (lf + lb), xa))
                ecfs.append(ecf)
                ecbs.append(tot[:, cbk:cbk + 1] - ecb)
            dout_f = jnp.exp(jnp.where(first_half, ecfs[0], ecfs[1]))
            dout_b = jnp.exp(jnp.where(first_half, ecbs[0], ecbs[1]))
            yoff = so[:, 0:LANE] * dout_f + so[:, LANE:2 * LANE] * dout_b
            y_ref[pl.ds(r0, cq), lanes] = jnp.where(first_half, ydiag[0], ydiag[1]) + yoff + dsk * xa
        return carry

    lax.fori_loop(0, nc, output, 0, unroll=2 if nc % 2 == 0 else 1)


def _ssd_branch(proj, bs, seq, lw, h0, cols):
    p_xm, p_bc, p_dt, inner = cols
    gw = inner // M2_GROUPS
    npair = gw // LANE
    nc = seq // M2_CHUNK
    bxm, bbc, bdt = p_xm // gw, p_bc // LANE, p_dt // LANE
    y, fin = pl.pallas_call(
        functools.partial(_ssd_kernel, seq=seq),
        out_shape=(jax.ShapeDtypeStruct((bs * seq, inner), F32),
                   jax.ShapeDtypeStruct((bs, M2_GROUPS, 2, npair, M2_STATE, LANE), F32)),
        grid=(bs, M2_GROUPS),
        in_specs=[pl.BlockSpec((seq, gw), lambda b, g: (b, bxm + g)),
                  pl.BlockSpec((seq, LANE), lambda b, g: (b, bbc + g)),
                  pl.BlockSpec((seq, LANE), lambda b, g: (b, bdt + g)),
                  pl.BlockSpec((None, SUBLANE, gw), lambda b, g: (g, 0, 0)),
                  pl.BlockSpec((None, SUBLANE, LANE), lambda b, g: (g, 0, 0)),
                  pl.BlockSpec((None, SUBLANE, LANE), lambda b, g: (g, 0, 0)),
                  pl.BlockSpec((None, None, 2, npair, M2_STATE, LANE), lambda b, g: (b, g, 0, 0, 0, 0))],
        out_specs=(pl.BlockSpec((seq, gw), lambda b, g: (b, g)),
                   pl.BlockSpec((None, None, 2, npair, M2_STATE, LANE), lambda b, g: (b, g, 0, 0, 0, 0))),
        scratch_shapes=[pltpu.VMEM((seq + 2 * SUBLANE, gw), F32), pltpu.VMEM((seq + 2 * SUBLANE, LANE), F32),
                        pltpu.VMEM((seq, gw), F32), pltpu.VMEM((seq, LANE), F32),
                        pltpu.VMEM((seq, LANE), F32), pltpu.VMEM((seq, LANE), F32),
                        pltpu.VMEM((nc * SUBLANE, LANE), F32), pltpu.VMEM((nc * SUBLANE, LANE), F32),
                        pltpu.VMEM((nc, npair, M2_STATE, LANE), F32), pltpu.VMEM((nc, npair, M2_STATE, LANE), F32),
                        pltpu.VMEM((nc * npair * SUBLANE, LANE), F32), pltpu.VMEM((nc * npair * SUBLANE, LANE), F32)],
        compiler_params=_cparams(("arbitrary", "arbitrary")),
        name="ssd",
    )(proj, proj, proj, lw["m2_cwx"], lw["m2_cwb"], lw["m2_pp"], h0)
    return y, fin


def _final_kernel(x_ref, y5_ref, u_ref, ys_ref, z_ref, hy_ref, g0_ref, g1_ref, g2_ref,
                  al_ref, sh_ref, sc_ref, d5_ref, wglu_ref, ng_ref, wm2_ref, wlru_ref, wo_ref, n2g_ref, wr_ref,
                  x1_ref, h2_ref, aff_ref):
    d = x_ref.shape[1]
    t5 = jax.nn.gelu(y5_ref[...] + d5_ref[...] * u_ref[...])
    vg = _dot(t5, wglu_ref[...])
    ya = vg[:, :d] * jax.nn.sigmoid(vg[:, d:])
    tb = _rms_scale(ys_ref[...] * _silu(z_ref[...])) * ng_ref[...]
    yb = _dot(tb, wm2_ref[...])
    yc = _dot(hy_ref[...], wlru_ref[...])
    merged = (jax.nn.sigmoid(g0_ref[...]) * ya + jax.nn.sigmoid(g1_ref[...]) * yb) + jax.nn.sigmoid(g2_ref[...]) * yc
    x1 = x_ref[...] + al_ref[...] * _dot(merged, wo_ref[...])
    x1_ref[...] = x1
    h2 = _rms_scale(x1) * n2g_ref[...] * (1.0 + sc_ref[...]) + sh_ref[...]
    h2_ref[...] = h2.astype(BF16)
    logits = _dot3(h2, wr_ref[...])
    valid = lax.broadcasted_iota(I32, logits.shape, 1) < N_EXPERTS
    logits = jnp.where(valid, logits, -jnp.inf)
    m = jnp.max(logits, axis=-1, keepdims=True)
    e = jnp.where(valid, jnp.exp(logits - m), 0.0)
    aff_ref[...] = e / jnp.sum(e, axis=-1, keepdims=True)


def _final(x2d, proj, y5, yssd, hy, seq, mods, lw, cols):
    t, d = x2d.shape
    tm = min(256, seq)
    alpha, shift2, scale2 = mods
    mi = _mod_index(alpha.shape[0], tm, seq)
    p_z, p_g, p_u, s5w = cols
    row = lambda w: pl.BlockSpec((tm, w), lambda i: (i, 0))
    pcol = lambda w, off: pl.BlockSpec((tm, w), lambda i: (i, off // w))
    full = lambda a: pl.BlockSpec(a.shape, lambda i: (0,) * a.ndim, pipeline_mode=pl.Buffered(1))
    mspec = pl.BlockSpec((None, 1, d), mi)
    weights = [lw["s5_d"], lw["s5_w_glu"], lw["m2_norm_g"], lw["m2_w_out"], lw["lru_w_out"], lw["w_o"],
               lw["norm2_g"], lw["w_router"]]
    return pl.pallas_call(
        _final_kernel,
        out_shape=(jax.ShapeDtypeStruct((t, d), F32), jax.ShapeDtypeStruct((t, d), BF16),
                   jax.ShapeDtypeStruct((t, LANE), F32)),
        grid=(t // tm,),
        in_specs=[row(d), row(s5w), pcol(s5w, p_u), row(d), pcol(d, p_z), row(hy.shape[1]),
                  pcol(d, p_g), pcol(d, p_g + d), pcol(d, p_g + 2 * d), mspec, mspec, mspec]
                 + [full(w) for w in weights],
        out_specs=(row(d), row(d), row(LANE)),
        compiler_params=_cparams(("arbitrary",)),
        name="merge_out",
    )(x2d, y5, proj, yssd, proj, hy, proj, proj, proj, alpha, shift2, scale2, *weights)


def _topk_kernel(aff_ref, slot_ref, start_ref, *, cap):
    a = aff_ref[...]
    nblk, ne, _ = a.shape
    key = lax.bitcast_convert_type(a, I32)

    def count(m):
        return jnp.sum(jnp.sum(m.astype(I32), axis=0, keepdims=True), axis=2, keepdims=True)

    def body(i, lo):
        cand = lo | (jnp.int32(1) << (30 - i))
        return jnp.where(count(key >= cand) >= cap, cand, lo)

    kth = lax.fori_loop(0, 31, body, jnp.zeros((1, ne, 1), I32))
    gt = key > kth
    eq = key == kth
    need = cap - count(gt)
    rowi = lax.broadcasted_iota(I32, (LANE, LANE), 0)
    coli = lax.broadcasted_iota(I32, (LANE, LANE), 1)
    upper = (rowi <= coli).astype(BF16)

    def exclusive_rank(m):
        mf = m.astype(F32)
        incl = jnp.dot(mf.reshape(nblk * ne, LANE).astype(BF16), upper, preferred_element_type=F32).reshape(nblk, ne, LANE)
        offs, run = [], jnp.zeros((1, ne, 1), F32)
        for k in range(nblk):
            offs.append(run)
            run = run + incl[k:k + 1, :, LANE - 1:LANE]
        offs = jnp.concatenate(offs, axis=0)
        return (incl - mf + offs).astype(I32), offs.astype(I32)

    sel = gt | (eq & (exclusive_rank(eq)[0] < need))
    rank, offs = exclusive_rank(sel)
    slot_ref[...] = jnp.where(sel, rank, -1)
    start_ref[...] = jnp.broadcast_to(offs, start_ref.shape)


def _topk_slots(aff_t, cap):
    bs, nblk, ne, _ = aff_t.shape
    spec = pl.BlockSpec((None, nblk, ne, LANE), lambda b: (b, 0, 0, 0))
    return pl.pallas_call(
        functools.partial(_topk_kernel, cap=cap),
        out_shape=(jax.ShapeDtypeStruct(aff_t.shape, I32), jax.ShapeDtypeStruct(aff_t.shape, I32)),
        grid=(bs,),
        in_specs=[spec],
        out_specs=(spec, spec),
        compiler_params=_cparams(("arbitrary",)),
        name="route_topk",
    )(aff_t)


def _moe_ffn_kernel(start_ref, h_ref, slot_ref, w1_ref, w3_ref, w2_ref, y_ref, xs_ref, *, cap, tw, sb):
    e, bstep = pl.program_id(0), pl.program_id(1)
    bb, nblk = slot_ref.shape[0], slot_ref.shape[1]
    bpw = tw // LANE
    nw = nblk // bpw
    sidx = lax.broadcasted_iota(I32, (sb, LANE), 0)
    for bi in range(bb):
        base = ((bstep * bb + bi) * pl.num_programs(0) + e) * nblk
        begins = [start_ref[base + k * bpw] for k in range(nw)]
        ends = begins[1:] + [cap]
        for j in range(cap // sb):
            lo, hi = j * sb, (j + 1) * sb
            rows = slice(bi * cap + lo, bi * cap + hi)
            k_lo = sum(jnp.asarray(en <= lo, I32) for en in ends)
            k_hi = sum(jnp.asarray(bg < hi, I32) for bg in begins)
            xs_ref[rows, :] = jnp.zeros((sb, xs_ref.shape[1]), F32)

            def body(k, carry, lo=lo, rows=rows, bi=bi):
                t0 = pl.multiple_of(k * tw, tw)
                p = jnp.concatenate([(slot_ref[bi, pl.ds(k * bpw + i, 1), :] == sidx + lo).astype(BF16)
                                     for i in range(bpw)], axis=1)
                xs_ref[rows, :] += jnp.dot(p, h_ref[bi, pl.ds(t0, tw), :], preferred_element_type=F32)
                return carry

            lax.fori_loop(k_lo, k_hi, body, 0)
    xs = xs_ref[...].astype(BF16)
    hid = _silu(jnp.dot(xs, w1_ref[...], preferred_element_type=F32)) * jnp.dot(xs, w3_ref[...], preferred_element_type=F32)
    y = jnp.dot(hid.astype(BF16), w2_ref[...], preferred_element_type=F32).astype(BF16)
    for bi in range(bb):
        y_ref[bi] = y[bi * cap:(bi + 1) * cap]


def _moe_ffn(h2, slot_e, starts, w1, w3, w2, layer, cap):
    bs, n, d = h2.shape
    ne, nblk = slot_e.shape[1], slot_e.shape[2]
    tw = min(512, n)
    sb = min(LANE, cap)
    bb = max(1, min(bs, 512 // cap))
    while bs % bb:
        bb -= 1
    wspec = lambda w: pl.BlockSpec((None, None) + w.shape[2:], lambda e, b, s: (layer, e, 0, 0))
    return pl.pallas_call(
        functools.partial(_moe_ffn_kernel, cap=cap, tw=tw, sb=sb),
        out_shape=jax.ShapeDtypeStruct((bs, ne, cap, d), BF16),
        grid_spec=pltpu.PrefetchScalarGridSpec(
            num_scalar_prefetch=1,
            grid=(ne, bs // bb),
            in_specs=[pl.BlockSpec((bb, n, d), lambda e, b, s: (b, 0, 0)),
                      pl.BlockSpec((bb, None, nblk, LANE), lambda e, b, s: (b, e, 0, 0)),
                      wspec(w1), wspec(w3), wspec(w2)],
            out_specs=pl.BlockSpec((bb, None, cap, d), lambda e, b, s: (b, e, 0, 0)),
            scratch_shapes=[pltpu.VMEM((bb * cap, d), F32)]),
        compiler_params=_cparams(("arbitrary", "arbitrary")),
        name="moe_ffn",
    )(starts, h2, slot_e, w1, w3, w2)


def _combine_kernel(x_ref, y_ref, slot_ref, aff_ref, al_ref, fg_ref, o_ref, *, cap, final_norm):
    slot = slot_ref[...]
    aff = aff_ref[...]
    tq = slot.shape[0]
    sidx = lax.broadcasted_iota(I32, (tq, cap), 1)
    acc = jnp.zeros(x_ref.shape, F32)
    for e in range(N_EXPERTS):
        pt = (slot[:, e:e + 1] == sidx).astype(BF16)
        acc = acc + aff[:, e:e + 1] * jnp.dot(pt, y_ref[e], preferred_element_type=F32)
    x2 = x_ref[...] + al_ref[...] * acc
    if final_norm:
        x2 = _rms_scale(x2) * fg_ref[...]
    o_ref[...] = x2


def _combine(x2d, y, slot_t, aff, seq, alpha, final_g, final_norm):
    t, d = x2d.shape
    bs, ne, cap, _ = y.shape
    tq = min(512, seq)
    nq = seq // tq
    mi = (lambda i: (0, 0, 0)) if alpha.shape[0] == 1 else (lambda i: (i // nq, 0, 0))
    return pl.pallas_call(
        functools.partial(_combine_kernel, cap=cap, final_norm=final_norm),
        out_shape=jax.ShapeDtypeStruct((t, d), F32),
        grid=(t // tq,),
        in_specs=[pl.BlockSpec((tq, d), lambda i: (i, 0)),
                  pl.BlockSpec((None, ne, cap, d), lambda i: (i // nq, 0, 0, 0)),
                  pl.BlockSpec((tq, LANE), lambda i: (i, 0)),
                  pl.BlockSpec((tq, LANE), lambda i: (i, 0)),
                  pl.BlockSpec((None, 1, d), mi),
                  pl.BlockSpec((1, d), lambda i: (0, 0))],
        out_specs=pl.BlockSpec((tq, d), lambda i: (i, 0)),
        compiler_params=_cparams(("arbitrary",)),
        name="moe_combine",
    )(x2d, y, slot_t, aff, alpha, final_g.reshape(1, d))


def _moe(x1, h2, aff, bs, seq, alpha, moe_w, layer, final_g, final_norm):
    d = x1.shape[1]
    cap = CAPACITY * seq // N_EXPERTS
    nblk = seq // LANE
    aff_t = aff[:, :N_EXPERTS].reshape(bs, nblk, LANE, N_EXPERTS).transpose(0, 1, 3, 2)
    slot, start = _topk_slots(aff_t, cap)
    slot_e = slot.transpose(0, 2, 1, 3)
    slot_t = slot.transpose(0, 1, 3, 2).reshape(bs * seq, N_EXPERTS)
    slot_t = jnp.pad(slot_t, ((0, 0), (0, LANE - N_EXPERTS)), constant_values=-1)
    starts = start[:, :, :, 0].transpose(0, 2, 1).reshape(-1)
    y = _moe_ffn(h2.reshape(bs, seq, d), slot_e, starts, *moe_w, layer, cap)
    return _combine(x1, y, slot_t, aff, seq, alpha, final_g, final_norm)


def _layout(d):
    s5w, inner, lruw = d // 2, d, d // 2
    gn = M2_GROUPS * M2_STATE
    heads = inner // M2_HEAD_DIM
    o_u = 0
    o_z = o_u + s5w
    o_xbc = o_z + inner
    o_dt = o_xbc + inner + 2 * gn
    o_xl = o_dt + 2 * heads
    o_gl = o_xl + lruw
    o_g = o_gl + lruw
    d_in = o_g + N_BRANCH * d
    p_z = 0
    p_g = p_z + inner
    p_xm = p_g + N_BRANCH * d
    p_bc = p_xm + inner
    p_u = p_bc + M2_GROUPS * LANE
    p_xl = p_u + s5w
    p_gl = p_xl + lruw
    p_dt = p_gl + lruw
    npad = p_dt + M2_GROUPS * LANE
    hpg = heads // M2_GROUPS
    assert hpg == 4 and M2_STATE == 64 and M2_HEAD_DIM == 64, "SSD kernel packs two 64-wide heads per lane tile"
    perm = np.full((npad,), d_in, np.int32)
    perm[p_z:p_z + inner] = o_z + np.arange(inner)
    perm[p_g:p_g + N_BRANCH * d] = o_g + np.arange(N_BRANCH * d)
    perm[p_xm:p_xm + inner] = o_xbc + np.arange(inner)
    bcp = np.zeros((M2_GROUPS * LANE,), np.int32)
    for g in range(M2_GROUPS):
        bcp[g * LANE:g * LANE + M2_STATE] = inner + gn + g * M2_STATE + np.arange(M2_STATE)
        bcp[g * LANE + M2_STATE:(g + 1) * LANE] = inner + g * M2_STATE + np.arange(M2_STATE)
    perm[p_bc:p_bc + M2_GROUPS * LANE] = o_xbc + bcp
    perm[p_u:p_u + s5w] = o_u + np.arange(s5w)
    perm[p_xl:p_xl + lruw] = o_xl + np.arange(lruw)
    perm[p_gl:p_gl + lruw] = o_gl + np.arange(lruw)
    for g in range(M2_GROUPS):
        for dd in range(2):
            for j in range(hpg):
                perm[p_dt + g * LANE + dd * hpg + j] = o_dt + dd * heads + g * hpg + j
    return dict(s5w=s5w, inner=inner, lruw=lruw, heads=heads, hpg=hpg, perm=perm, bcp=bcp, npad=npad,
                p_z=p_z, p_g=p_g, p_xm=p_xm, p_bc=p_bc, p_u=p_u, p_xl=p_xl, p_gl=p_gl, p_dt=p_dt)


def _rows8(rows, width):
    out = jnp.zeros((SUBLANE, width), F32)
    for i, r in enumerate(rows):
        out = out.at[i, :r.shape[0]].set(r.astype(F32))
    return out


def _pack_layer(lay, p, q_lat, q_ctx):
    d = p["w_in"].shape[0]
    inner, lruw, hpg, heads = lay["inner"], lay["lruw"], lay["hpg"], lay["heads"]
    lw = {}
    w_ext = jnp.concatenate([p["w_in"], jnp.zeros((d, 1), p["w_in"].dtype)], axis=1)
    lw["w_in"] = w_ext[:, lay["perm"]].astype(BF16)
    lw["s5_lat"] = _s5_weights(p["s5_lam_re"], p["s5_lam_im"], p["s5_log_step"], p["s5_b_re"], p["s5_b_im"],
                               p["s5_c_re"], p["s5_c_im"], q_lat)
    lw["s5_ctx"] = lw["s5_lat"] if q_ctx == q_lat else _s5_weights(
        p["s5_lam_re"], p["s5_lam_im"], p["s5_log_step"], p["s5_b_re"], p["s5_b_im"], p["s5_c_re"], p["s5_c_im"], q_ctx)
    gw = inner // M2_GROUPS
    cw, cb = p["m2_conv_w"], p["m2_conv_b"]
    lw["m2_cwx"] = jnp.stack([_rows8([cw[k, g * gw:(g + 1) * gw] for k in range(CONV_W)] + [cb[g * gw:(g + 1) * gw]], gw)
                              for g in range(M2_GROUPS)])
    cwb, cbb = cw[:, lay["bcp"]], cb[lay["bcp"]]
    lw["m2_cwb"] = jnp.stack([_rows8([cwb[k, g * LANE:(g + 1) * LANE] for k in range(CONV_W)] + [cbb[g * LANE:(g + 1) * LANE]], LANE)
                              for g in range(M2_GROUPS)])
    dtb = p["m2_dt_bias"].reshape(2, M2_GROUPS, hpg)
    alog = p["m2_a_log"].reshape(2, M2_GROUPS, hpg)
    dsk = p["m2_d"].reshape(M2_GROUPS, hpg)
    lw["m2_pp"] = jnp.stack([_rows8([dtb[:, g].reshape(-1), alog[:, g].reshape(-1), dsk[g]], LANE) for g in range(M2_GROUPS)])
    blk = lruw // LRU_BLOCKS
    eye = jnp.eye(LRU_BLOCKS, dtype=F32)

    def dense(w):
        return jnp.einsum('hij,hk->hikj', w.astype(F32), eye).reshape(lruw, lruw)

    ncb = lruw // LANE
    assert LANE % blk == 0
    mats = [dense(p["lru_w_a"][0]), dense(p["lru_w_x"][0]), dense(p["lru_w_a"][1]), dense(p["lru_w_x"][1])]
    lw["lru_w"] = jnp.stack([jnp.concatenate([m[c * LANE:(c + 1) * LANE, c * LANE:(c + 1) * LANE] for m in mats], axis=1)
                             for c in range(ncb)]).astype(BF16)
    sl = lambda v, c: v[c * LANE:(c + 1) * LANE]
    pps = []
    for c in range(ncb):
        bias = jnp.concatenate([sl(p["lru_b_a"][0], c), sl(p["lru_b_x"][0], c), sl(p["lru_b_a"][1], c), sl(p["lru_b_x"][1], c)])
        lam = jnp.concatenate([sl(p["lru_lam"][0], c), sl(p["lru_lam"][1], c)])
        pps.append(_rows8([bias, lam, sl(p["lru_conv_b"], c)] + [sl(p["lru_conv_w"][k], c) for k in range(CONV_W)], 4 * LANE))
    lw["lru_pp"] = jnp.stack(pps)
    lw["s5_d"] = p["s5_d"].reshape(1, -1).astype(F32)
    lw["s5_w_glu"] = p["s5_w_glu"].astype(BF16)
    lw["m2_norm_g"] = p["m2_norm_g"].reshape(1, -1).astype(F32)
    lw["m2_w_out"] = p["m2_w_out"].astype(BF16)
    lw["lru_w_out"] = p["lru_w_out"].astype(BF16)
    lw["w_o"] = p["w_o"].astype(BF16)
    lw["norm2_g"] = p["norm2_g"].reshape(1, -1).astype(F32)
    lw["w_router"] = jnp.pad(p["moe_w_router"].astype(F32), ((0, 0), (0, LANE - N_EXPERTS)))
    return lw


def _mixer(x2d, bs, seq, colmajor, norm_g, shift, scale, lw, lay, h0, s5w):
    proj = _inproj(x2d, seq, norm_g, shift, scale, lw["w_in"])
    y5, f5 = _s5_branch(proj, bs, seq, colmajor, s5w, h0[0], (lay["p_u"], lay["s5w"]))
    ys, fm = _ssd_branch(proj, bs, seq, lw, h0[1], (lay["p_xm"], lay["p_bc"], lay["p_dt"], lay["inner"]))
    hy, fl = _lru_branch(proj, bs, seq, lw, h0[2], (lay["p_xl"], lay["p_gl"], lay["lruw"]))
    return proj, y5, ys, hy, (f5, fm, fl)


def kernel(x, c, ctx, c_ctx, w_mod, b_mod, norm1_g, norm2_g, w_in, s5_lam_re, s5_lam_im, s5_log_step, s5_b_re, s5_b_im, s5_c_re, s5_c_im, s5_d, s5_w_glu, m2_conv_w, m2_conv_b, m2_dt_bias, m2_a_log, m2_d, m2_norm_g, m2_w_out, lru_conv_w, lru_conv_b, lru_w_a, lru_b_a, lru_w_x, lru_b_x, lru_lam, lru_w_out, w_o, moe_w_router, moe_w1, moe_w3, moe_w2, final_norm_g):
    bsz, seq, d = x.shape
    cl = ctx.shape[1]
    depth = w_mod.shape[0]
    lay = _layout(d)
    stacked = dict(norm2_g=norm2_g, w_in=w_in, s5_lam_re=s5_lam_re, s5_lam_im=s5_lam_im, s5_log_step=s5_log_step,
                   s5_b_re=s5_b_re, s5_b_im=s5_b_im, s5_c_re=s5_c_re, s5_c_im=s5_c_im, s5_d=s5_d, s5_w_glu=s5_w_glu,
                   m2_conv_w=m2_conv_w, m2_conv_b=m2_conv_b, m2_dt_bias=m2_dt_bias, m2_a_log=m2_a_log, m2_d=m2_d,
                   m2_norm_g=m2_norm_g, m2_w_out=m2_w_out, lru_conv_w=lru_conv_w, lru_conv_b=lru_conv_b,
                   lru_w_a=lru_w_a, lru_b_a=lru_b_a, lru_w_x=lru_w_x, lru_b_x=lru_b_x, lru_lam=lru_lam,
                   lru_w_out=lru_w_out, w_o=w_o, moe_w_router=moe_w_router)
    moe_w = (moe_w1.astype(BF16), moe_w3.astype(BF16), moe_w2.astype(BF16))
    rm = -(-(bsz + 1) // SUBLANE) * SUBLANE
    c_rows = jnp.zeros((rm, d), F32).at[:bsz].set(c.astype(F32)).at[bsz].set(c_ctx.astype(F32))
    mods = _modulation(c_rows, w_mod.astype(F32), b_mod.astype(F32))

    groups = lay["s5w"] // S5_GROUP
    ncb = lay["lruw"] // LANE
    npair = lay["inner"] // M2_GROUPS // LANE
    zero_h0 = (jnp.zeros((groups, 2, bsz, LANE), F32),
               jnp.zeros((bsz, M2_GROUPS, 2, npair, M2_STATE, LANE), F32),
               jnp.zeros((bsz, ncb, 2, LANE), F32))

    xs = x.reshape(bsz * seq, d).astype(F32)
    cs = ctx.reshape(bsz * cl, d).astype(F32)
    q_lat, q_ctx = min(S5_CHUNK, seq), min(S5_CHUNK, cl)
    packed = jax.vmap(lambda p: _pack_layer(lay, p, q_lat, q_ctx))(stacked)
    for i in range(depth):
        lw = jax.tree_util.tree_map(lambda v: v[i], packed)
        mx = [mods[i, :bsz, k * d:(k + 1) * d].reshape(bsz, 1, d) for k in range(6)]
        mc = [mods[i, bsz:bsz + 1, k * d:(k + 1) * d].reshape(1, 1, d) for k in range(6)]
        fcols = (lay["p_z"], lay["p_g"], lay["p_u"], lay["s5w"])
        cproj, cy5, cys, chy, cstates = _mixer(cs, bsz, cl, False, norm1_g[i], mc[0], mc[1], lw, lay, zero_h0, lw["s5_ctx"])
        if i < depth - 1:
            c1, ch2, caff = _final(cs, cproj, cy5, cys, chy, cl, (mc[2], mc[3], mc[4]), lw, fcols)
            cs = _moe(c1, ch2, caff, bsz, cl, mc[5], moe_w, i, final_norm_g, False)
        xproj, y5, ys, hy, _ = _mixer(xs, bsz, seq, True, norm1_g[i], mx[0], mx[1], lw, lay, cstates, lw["s5_lat"])
        x1, h2, aff = _final(xs, xproj, y5, ys, hy, seq, (mx[2], mx[3], mx[4]), lw, fcols)
        xs = _moe(x1, h2, aff, bsz, seq, mx[5], moe_w, i, final_norm_g, i == depth - 1)
    return xs.reshape(bsz, seq, d).astype(x.dtype)
```

```python
import functools
import math

import numpy as np
import jax
import jax.numpy as jnp
from jax import lax
from jax.experimental import pallas as pl
from jax.experimental.pallas import tpu as pltpu

F32 = jnp.float32
BF16 = jnp.bfloat16
I32 = jnp.int32

GRID_W = 64
EPS = 1e-6
CONV_W = 4
S5_GROUP = 16
S5_STATE = 64
M2_HEAD_DIM = 64
M2_GROUPS = 4
M2_STATE = 64
M2_CHUNK = 128
LRU_BLOCKS = 8
LRU_C = 8.0
N_EXPERTS = 16
CAPACITY = 2
N_BRANCH = 3

LANE = 128
SUBLANE = 8
S5_CHUNK = 128
LRU_TILE = 64
VMEM_LIMIT = 56 * 1024 * 1024


def _cparams(sem):
    return pltpu.CompilerParams(dimension_semantics=sem, vmem_limit_bytes=VMEM_LIMIT)


def _dot(a, b):
    return jnp.dot(a.astype(BF16), b.astype(BF16), preferred_element_type=F32)


def _dot_nt(a, b):
    return lax.dot_general(a.astype(BF16), b.astype(BF16), (((1,), (1,)), ((), ())), preferred_element_type=F32)


def _split2(a):
    hi = a.astype(BF16)
    lo = (a - hi.astype(F32)).astype(BF16)
    return hi, lo


def _split3(a):
    hi = a.astype(BF16)
    r = a - hi.astype(F32)
    mid = r.astype(BF16)
    lo = (r - mid.astype(F32)).astype(BF16)
    return hi, mid, lo


def _dot3(a, b):
    ah, al = _split2(a)
    bh, bl = _split2(b)
    d = lambda x, y: jnp.dot(x, y, preferred_element_type=F32)
    return d(ah, bh) + (d(ah, bl) + d(al, bh))


def _dot_exact_lhs(m01, x):
    hi, mid, lo = _split3(x)
    d = lambda y: jnp.dot(m01, y, preferred_element_type=F32)
    return d(hi) + (d(mid) + d(lo))


def _silu(x):
    return x * jax.nn.sigmoid(x)


def _softplus(x):
    return jnp.maximum(x, 0.0) + jnp.log(1.0 + jnp.exp(-jnp.abs(x)))


def _rms_scale(x):
    return x * lax.rsqrt(jnp.mean(x * x, axis=-1, keepdims=True) + EPS)


def _mod_kernel(c_ref, w_ref, b_ref, o_ref):
    o_ref[...] = _dot3(_silu(c_ref[...]), w_ref[...]) + b_ref[...]


def _modulation(c_rows, w_mod, b_mod):
    depth, d, n6 = w_mod.shape
    rm = c_rows.shape[0]
    tn = min(1024, n6)
    return pl.pallas_call(
        _mod_kernel,
        out_shape=jax.ShapeDtypeStruct((depth, rm, n6), F32),
        grid=(depth, n6 // tn),
        in_specs=[pl.BlockSpec((rm, d), lambda l, j: (0, 0)),
                  pl.BlockSpec((None, d, tn), lambda l, j: (l, 0, j)),
                  pl.BlockSpec((None, 1, tn), lambda l, j: (l, 0, j))],
        out_specs=pl.BlockSpec((None, rm, tn), lambda l, j: (l, 0, j)),
        compiler_params=_cparams(("arbitrary", "arbitrary")),
        name="modulation",
    )(c_rows, w_mod, b_mod.reshape(depth, 1, n6))


def _inproj_kernel(x_ref, g_ref, sh_ref, sc_ref, w_ref, o_ref, *, nchunk):
    h = _rms_scale(x_ref[...]) * g_ref[...] * (1.0 + sc_ref[...]) + sh_ref[...]
    hb = h.astype(BF16)
    npad = o_ref.shape[1]
    for n0 in range(0, npad, nchunk):
        o_ref[:, n0:n0 + nchunk] = jnp.dot(hb, w_ref[:, n0:n0 + nchunk], preferred_element_type=F32)


def _mod_index(bm, tm, seq):
    if bm == 1:
        return lambda i: (0, 0, 0)
    return lambda i: ((i * tm) // seq, 0, 0)


def _inproj(x2d, seq, g, shift, scale, w):
    t, d = x2d.shape
    npad = w.shape[1]
    tm = min(256, seq)
    mi = _mod_index(shift.shape[0], tm, seq)
    return pl.pallas_call(
        functools.partial(_inproj_kernel, nchunk=512),
        out_shape=jax.ShapeDtypeStruct((t, npad), F32),
        grid=(t // tm,),
        in_specs=[pl.BlockSpec((tm, d), lambda i: (i, 0)),
                  pl.BlockSpec((1, d), lambda i: (0, 0)),
                  pl.BlockSpec((None, 1, d), mi),
                  pl.BlockSpec((None, 1, d), mi),
                  pl.BlockSpec((d, npad), lambda i: (0, 0), pipeline_mode=pl.Buffered(1))],
        out_specs=pl.BlockSpec((tm, npad), lambda i: (i, 0)),
        compiler_params=_cparams(("arbitrary",)),
        name="inproj",
    )(x2d, g.reshape(1, d), shift, scale, w)


def _build_toeplitz(cp_ref, cn_ref, t_ref, q):
    trow = lax.broadcasted_iota(I32, (q, q), 0)
    tcol = lax.broadcasted_iota(I32, (q, q), 1)
    causal = tcol >= trow

    def body(j, carry):
        cp = cp_ref[j]
        cn = cn_ref[j]
        r0 = pl.multiple_of(j * q, q)
        for i in range(S5_GROUP):
            a = pltpu.roll(jnp.broadcast_to(cp[i:i + 1, :], (q, q)), 0, 1, stride=1, stride_axis=0)
            b = pltpu.roll(jnp.broadcast_to(cn[i:i + 1, :], (q, q)), 0, 1, stride=1, stride_axis=0)
            t_ref[pl.ds(r0, q), i * q:(i + 1) * q] = jnp.where(causal, a, b).astype(BF16)
        return carry

    lax.fori_loop(0, S5_GROUP, body, 0)


def _s5_kernel(u_ref, cp_ref, cn_ref, wst_ref, wout_ref, dec_ref, h0_ref, y_ref, fin_ref, ef_ref, eb_ref, t_ref, *, nc, bs):
    q = u_ref.shape[2]
    _build_toeplitz(cp_ref, cn_ref, t_ref, q)
    a = jnp.concatenate([u_ref[j] for j in range(S5_GROUP)], axis=-1).astype(BF16)
    loc = jnp.dot(a, wst_ref[...], preferred_element_type=F32)
    dec = dec_ref[...]
    half = S5_STATE

    def step(cur, da, db, add):
        return da * cur + db * pltpu.roll(cur, half, 1) + add

    cur = h0_ref[0]
    for c in range(nc):
        ef_ref[c * bs:(c + 1) * bs, :] = cur
        cur = step(cur, dec[0:1], dec[1:2], loc[c * bs:(c + 1) * bs, 0:LANE])
    fin_ref[0] = cur
    cur = h0_ref[1]
    for c in reversed(range(nc)):
        eb_ref[c * bs:(c + 1) * bs, :] = cur
        cur = step(cur, dec[2:3], dec[3:4], loc[c * bs:(c + 1) * bs, LANE:2 * LANE])
    fin_ref[1] = cur
    e = jnp.concatenate([ef_ref[...], eb_ref[...]], axis=-1).astype(BF16)
    acc = jnp.dot(a, t_ref[...], preferred_element_type=F32) + jnp.dot(e, wout_ref[...], preferred_element_type=F32)
    for i in range(S5_GROUP):
        y_ref[i] = acc[:, i * q:(i + 1) * q]


def _s5_scan(ut, s5w, h0, nc, bs):
    cpos, cneg, wst, wout, dec = s5w
    groups = cpos.shape[0]
    r, q = ut.shape[1], ut.shape[2]
    assert q == LANE, "the Toeplitz builder rotates one 128-lane tile per block"
    kq = S5_GROUP * q
    lagspec = pl.BlockSpec((None, S5_GROUP, S5_GROUP, q), lambda g: (g, 0, 0, 0))
    return pl.pallas_call(
        functools.partial(_s5_kernel, nc=nc, bs=bs),
        out_shape=(jax.ShapeDtypeStruct(ut.shape, F32), jax.ShapeDtypeStruct((groups, 2, bs, LANE), F32)),
        grid=(groups,),
        in_specs=[pl.BlockSpec((S5_GROUP, r, q), lambda g: (g, 0, 0)),
                  lagspec, lagspec,
                  pl.BlockSpec((None, kq, 2 * LANE), lambda g: (g, 0, 0)),
                  pl.BlockSpec((None, 2 * LANE, kq), lambda g: (g, 0, 0)),
                  pl.BlockSpec((None, SUBLANE, LANE), lambda g: (g, 0, 0)),
                  pl.BlockSpec((None, 2, bs, LANE), lambda g: (g, 0, 0, 0))],
        out_specs=(pl.BlockSpec((S5_GROUP, r, q), lambda g: (g, 0, 0)),
                   pl.BlockSpec((None, 2, bs, LANE), lambda g: (g, 0, 0, 0))),
        scratch_shapes=[pltpu.VMEM((r, LANE), F32), pltpu.VMEM((r, LANE), F32), pltpu.VMEM((kq, kq), BF16)],
        compiler_params=_cparams(("arbitrary",)),
        name="s5_scan",
    )(ut, cpos, cneg, wst, wout, dec, h0)


def _s5_weights(lam_re, lam_im, log_step, b_re, b_im, c_re, c_im, q):
    hp = lax.Precision.HIGHEST
    g, p = lam_re.shape[1], lam_re.shape[2]
    ii = S5_GROUP
    tau = jnp.arange(q + 1, dtype=F32)[:, None, None]
    ks, wsts, wouts, decs = [], [], [], []
    for d in range(2):
        lr, li = lam_re[d].astype(F32), lam_im[d].astype(F32)
        step = jnp.exp(log_step[d].astype(F32))[:, None]
        ar, ai = lr * step, li * step
        mag = jnp.exp(tau * ar)
        pre, pim = mag * jnp.cos(tau * ai), mag * jnp.sin(tau * ai)
        bar_re, bar_im = pre[1], pim[1]
        den = lr * lr + li * li
        nr, ni = bar_re - 1.0, bar_im
        coef_re = (nr * lr + ni * li) / den
        coef_im = (ni * lr - nr * li) / den
        bre, bim = b_re[d].astype(F32), b_im[d].astype(F32)
        bb_re = coef_re[..., None] * bre - coef_im[..., None] * bim
        bb_im = coef_re[..., None] * bim + coef_im[..., None] * bre
        cre, cim = c_re[d].astype(F32), c_im[d].astype(F32)
        cp_re = cre[None] * pre[:, :, None, :] - cim[None] * pim[:, :, None, :]
        cp_im = cre[None] * pim[:, :, None, :] + cim[None] * pre[:, :, None, :]
        k = (jnp.einsum('tgip,gpj->tgij', cp_re[:q], bb_re, precision=hp)
             - jnp.einsum('tgip,gpj->tgij', cp_im[:q], bb_im, precision=hp))
        ks.append(k)
        pw_re = pre[q - 1::-1][:q] if d == 0 else pre[:q]
        pw_im = pim[q - 1::-1][:q] if d == 0 else pim[:q]
        w_re = pw_re[:, :, :, None] * bb_re[None] - pw_im[:, :, :, None] * bb_im[None]
        w_im = pw_re[:, :, :, None] * bb_im[None] + pw_im[:, :, :, None] * bb_re[None]
        w = jnp.concatenate([w_re, w_im], axis=2)
        wsts.append(jnp.transpose(w, (1, 3, 0, 2)).reshape(g, ii * q, 2 * p))
        if d == 0:
            o_re, o_im = cp_re[1:q + 1], cp_im[1:q + 1]
        else:
            o_re, o_im = cp_re[q:0:-1], cp_im[q:0:-1]
        o = jnp.concatenate([o_re, -o_im], axis=3)
        wouts.append(jnp.transpose(o, (1, 3, 2, 0)).reshape(g, 2 * p, ii * q))
        dr, di = pre[q], pim[q]
        decs.append(jnp.concatenate([dr, dr], axis=-1))
        decs.append(jnp.concatenate([-di, di], axis=-1))
    cpos = jnp.transpose(ks[0].at[0].add(ks[1][0]), (1, 3, 2, 0))
    kb_rev = ks[1][::-1]
    cneg = jnp.concatenate([jnp.zeros_like(kb_rev[:1]), kb_rev[:q - 1]], axis=0)
    cneg = jnp.transpose(cneg, (1, 3, 2, 0))
    wst = jnp.concatenate(wsts, axis=-1).astype(BF16)
    wout = jnp.concatenate(wouts, axis=1).astype(BF16)
    dec = jnp.stack(decs + decs, axis=1)
    return cpos, cneg, wst, wout, dec


def _s5_in_kernel(x_ref, o_ref, *, cpc, colmajor):
    q = o_ref.shape[3]
    for b in range(x_ref.shape[0]):
        for cc in range(o_ref.shape[1]):
            if colmajor:
                x = jnp.concatenate([x_ref[b, :, cc * cpc + w, :] for w in range(cpc)], axis=0)
            else:
                x = x_ref[b, cc * q:(cc + 1) * q, :]
            o_ref[:, cc, b, :] = x.T


def _s5_out_kernel(y_ref, o_ref, *, cpc, colmajor):
    q = y_ref.shape[3]
    for b in range(o_ref.shape[0]):
        for cc in range(y_ref.shape[1]):
            t = y_ref[:, cc, b, :].T
            if colmajor:
                rows = q // cpc
                for w in range(cpc):
                    o_ref[b, :, cc * cpc + w, :] = t[w * rows:(w + 1) * rows]
            else:
                o_ref[b, cc * q:(cc + 1) * q, :] = t


def _s5_layout(bs, seq, colmajor, q):
    bb = min(SUBLANE, bs)
    while bs % bb:
        bb -= 1
    nc = seq // q
    if not colmajor:
        return bb, nc, 1, 1, None
    rows = seq // GRID_W
    assert q % rows == 0, "a scan chunk must cover whole grid columns"
    cpc = q // rows
    wb = max(SUBLANE, cpc)
    return bb, wb // cpc, GRID_W // wb, cpc, (rows, wb)


def _s5_branch(proj, bs, seq, colmajor, s5w, h0, cols):
    p_u, width = cols
    q = min(S5_CHUNK, seq)
    nc = seq // q
    bb, nch, nj, cpc, cm = _s5_layout(bs, seq, colmajor, q)
    cb = p_u // width
    cmaj = jax.ShapeDtypeStruct((width, nc, bs, q), F32)
    cspec = pl.BlockSpec((width, nch, bb, q), lambda i, j: (0, j, i, 0))
    if colmajor:
        rows, wb = cm
        tok_in = pl.BlockSpec((bb, rows, wb, width), lambda i, j: (i, 0, j, cb))
        tok_out = pl.BlockSpec((bb, rows, wb, width), lambda i, j: (i, 0, j, 0))
        x_in = proj.reshape(bs, rows, GRID_W, proj.shape[1])
        tok_shape = jax.ShapeDtypeStruct((bs, rows, GRID_W, width), F32)
    else:
        tok_in = pl.BlockSpec((bb, seq, width), lambda i, j: (i, 0, cb))
        tok_out = pl.BlockSpec((bb, seq, width), lambda i, j: (i, 0, 0))
        x_in = proj.reshape(bs, seq, proj.shape[1])
        tok_shape = jax.ShapeDtypeStruct((bs, seq, width), F32)
    ut = pl.pallas_call(
        functools.partial(_s5_in_kernel, cpc=cpc, colmajor=colmajor), out_shape=cmaj, grid=(bs // bb, nj),
        in_specs=[tok_in], out_specs=cspec, compiler_params=_cparams(("arbitrary", "arbitrary")), name="s5_to_channel_major",
    )(x_in)
    yt, fin = _s5_scan(ut.reshape(width, nc * bs, q), s5w, h0, nc, bs)
    y = pl.pallas_call(
        functools.partial(_s5_out_kernel, cpc=cpc, colmajor=colmajor), out_shape=tok_shape, grid=(bs // bb, nj),
        in_specs=[cspec], out_specs=tok_out, compiler_params=_cparams(("arbitrary", "arbitrary")), name="s5_to_token_major",
    )(yt.reshape(width, nc, bs, q))
    return y.reshape(bs * seq, width), fin


def _conv_rows(pad_ref, base, n, taps, bias):
    w = pad_ref[pl.ds(base, n + 2 * SUBLANE), :]
    tot = n + 2 * SUBLANE
    xm2 = pltpu.roll(w, 2, 0)[SUBLANE:SUBLANE + n]
    xm1 = pltpu.roll(w, 1, 0)[SUBLANE:SUBLANE + n]
    x0 = w[SUBLANE:SUBLANE + n]
    xp1 = pltpu.roll(w, tot - 1, 0)[SUBLANE:SUBLANE + n]
    return taps[0:1] * xm2 + taps[1:2] * xm1 + taps[2:3] * x0 + taps[3:4] * xp1 + bias


def _fill_padded(pad_ref, x_ref, seq):
    zeros = jnp.zeros((SUBLANE, pad_ref.shape[1]), F32)
    pad_ref[0:SUBLANE, :] = zeros
    pad_ref[SUBLANE + seq:2 * SUBLANE + seq, :] = zeros
    pad_ref[SUBLANE:SUBLANE + seq, :] = x_ref[...]


def _scan_tile(a, v, h, reverse):
    s = a.shape[0]
    row = lax.broadcasted_iota(I32, a.shape, 0)
    k = 1
    while k < s:
        sh = s - k if reverse else k
        ok = (row < s - k) if reverse else (row >= k)
        a_sh = pltpu.roll(a, sh, 0)
        v_sh = pltpu.roll(v, sh, 0)
        v = v + a * jnp.where(ok, v_sh, 0.0)
        a = a * jnp.where(ok, a_sh, 1.0)
        k *= 2
    return v + a * h


def _lru_kernel(x_ref, g_ref, w_ref, pp_ref, h0_ref, o_ref, fin_ref, pad_ref, a0_ref, v0_ref, a1_ref, v1_ref, *, seq, ch):
    pp = pp_ref[...]
    bias = pp[0:1]
    lam = pp[1:2]
    cb = pp[2:3, 0:LANE]
    taps = pp[3:7, 0:LANE]
    _fill_padded(pad_ref, x_ref, seq)
    av = ((a0_ref, v0_ref), (a1_ref, v1_ref))

    def gates(i, carry):
        base = pl.multiple_of(i * ch, ch)
        xc = _conv_rows(pad_ref, base, ch, taps, cb)
        gt = _dot(xc, w_ref[...]) + bias
        for d in range(2):
            r = jax.nn.sigmoid(gt[:, 2 * d * LANE:(2 * d + 1) * LANE])
            ig = jax.nn.sigmoid(gt[:, (2 * d + 1) * LANE:(2 * d + 2) * LANE])
            log_a = -LRU_C * r * _softplus(-lam[:, d * LANE:(d + 1) * LANE])
            a = jnp.exp(log_a)
            av[d][0][pl.ds(base, ch), :] = a
            av[d][1][pl.ds(base, ch), :] = jnp.sqrt(jnp.maximum(1.0 - a * a, EPS)) * (ig * xc)
        return carry

    lax.fori_loop(0, seq // ch, gates, 0, unroll=2 if (seq // ch) % 2 == 0 else 1)
    tile = min(LRU_TILE, seq)
    nt = seq // tile

    def fwd(i, h):
        r0 = pl.multiple_of(i * tile, tile)
        hall = _scan_tile(a0_ref[pl.ds(r0, tile), :], v0_ref[pl.ds(r0, tile), :], h, False)
        o_ref[pl.ds(r0, tile), :] = hall
        return hall[tile - 1:tile, :]

    h0 = h0_ref[...]
    hf = lax.fori_loop(0, nt, fwd, h0[0:1])

    def bwd(i, h):
        r0 = pl.multiple_of((nt - 1 - i) * tile, tile)
        hall = _scan_tile(a1_ref[pl.ds(r0, tile), :], v1_ref[pl.ds(r0, tile), :], h, True)
        o_ref[pl.ds(r0, tile), :] = (o_ref[pl.ds(r0, tile), :] + hall) * jax.nn.gelu(g_ref[pl.ds(r0, tile), :])
        return hall[0:1, :]

    hb = lax.fori_loop(0, nt, bwd, h0[1:2])
    fin_ref[...] = jnp.concatenate([hf, hb], axis=0)


def _lru_branch(proj, bs, seq, lw, h0, cols):
    p_xl, p_gl, width = cols
    ncb = width // LANE
    bx, bg = p_xl // LANE, p_gl // LANE
    ch = min(256, seq)
    sc = pltpu.VMEM((seq, LANE), F32)
    hy, fin = pl.pallas_call(
        functools.partial(_lru_kernel, seq=seq, ch=ch),
        out_shape=(jax.ShapeDtypeStruct((bs * seq, width), F32), jax.ShapeDtypeStruct((bs, ncb, 2, LANE), F32)),
        grid=(bs, ncb),
        in_specs=[pl.BlockSpec((seq, LANE), lambda b, c: (b, bx + c)),
                  pl.BlockSpec((seq, LANE), lambda b, c: (b, bg + c)),
                  pl.BlockSpec((None, LANE, 4 * LANE), lambda b, c: (c, 0, 0)),
                  pl.BlockSpec((None, SUBLANE, 4 * LANE), lambda b, c: (c, 0, 0)),
                  pl.BlockSpec((None, None, 2, LANE), lambda b, c: (b, c, 0, 0))],
        out_specs=(pl.BlockSpec((seq, LANE), lambda b, c: (b, c)),
                   pl.BlockSpec((None, None, 2, LANE), lambda b, c: (b, c, 0, 0))),
        scratch_shapes=[pltpu.VMEM((seq + 2 * SUBLANE, LANE), F32), sc, sc, sc, sc],
        compiler_params=_cparams(("arbitrary", "arbitrary")),
        name="rglru",
    )(proj, proj, lw["lru_w"], lw["lru_pp"], h0)
    return hy, fin


def _ssd_kernel(xm_ref, bc_ref, dt_ref, cwx_ref, cwb_ref, pp_ref, h0_ref, y_ref, fin_ref,
                xpad_ref, bpad_ref, xa_ref, ba_ref, ex_ref, cb_ref, ext_ref, tot_ref,
                sf_ref, sb_ref, df_ref, db_ref, *, seq):
    cq = M2_CHUNK
    nc = seq // cq
    hd = M2_HEAD_DIM
    ns = M2_STATE
    npair = xm_ref.shape[1] // LANE
    hpg = 2 * npair
    pp = pp_ref[...]
    dt_bias, a_neg, dskip = pp[0:1], -jnp.exp(pp[1:2]), pp[2:3]
    cwx, cwb = cwx_ref[...], cwb_ref[...]
    _fill_padded(xpad_ref, xm_ref, seq)
    _fill_padded(bpad_ref, bc_ref, seq)

    rowi = lax.broadcasted_iota(I32, (cq, cq), 0)
    coli = lax.broadcasted_iota(I32, (cq, cq), 1)
    tri = (coli <= rowi).astype(BF16)
    lane = lax.broadcasted_iota(I32, (cq, LANE), 1)
    first_half = lane < hd
    first_half_s = lax.broadcasted_iota(I32, (ns, LANE), 1) < hd
    first_half1 = lax.broadcasted_iota(I32, (1, LANE), 1) < hd
    srefs, drefs = (sf_ref, sb_ref), (df_ref, db_ref)

    def activations(c):
        r0 = pl.multiple_of(c * cq, cq)
        r8 = pl.multiple_of(c * SUBLANE, SUBLANE)
        xa_ref[pl.ds(r0, cq), :] = _silu(_conv_rows(xpad_ref, r0, cq, cwx[0:4], cwx[4:5]))
        ba_ref[pl.ds(r0, cq), :] = _silu(_conv_rows(bpad_ref, r0, cq, cwb[0:4], cwb[4:5]))
        dtv = _softplus(dt_ref[pl.ds(r0, cq), :] + dt_bias)
        la = dtv * a_neg
        cum = _dot_exact_lhs(tri, la)
        tot = cum[cq - 1:cq, :]
        ex = jnp.where(lane < hpg, cum, cum - la)
        ex_ref[pl.ds(r0, cq), :] = ex
        ldt = jnp.log(dtv.T[0:SUBLANE, :])
        ext = ex.T[0:SUBLANE, :]
        fwd_rows = lax.broadcasted_iota(I32, (SUBLANE, LANE), 0) < hpg
        ext_ref[pl.ds(r8, SUBLANE), :] = jnp.where(fwd_rows, ext - ldt, ext + ldt)
        tot_ref[pl.ds(r8, SUBLANE), :] = jnp.broadcast_to(tot, (SUBLANE, LANE))

    def local_states(c):
        r0 = pl.multiple_of(c * cq, cq)
        r8 = pl.multiple_of(c * SUBLANE, SUBLANE)
        bc = ba_ref[pl.ds(r0, cq), :]
        ext = ext_ref[pl.ds(r8, SUBLANE), :]
        tot = tot_ref[pl.ds(r8, 1), :]
        bt = bc.T[ns:2 * ns, :]
        cb_ref[pl.ds(r0, cq), :] = _dot(bc[:, 0:ns], bt)
        for d in range(2):
            for pr in range(npair):
                xa = xa_ref[pl.ds(r0, cq), pr * LANE:(pr + 1) * LANE]
                rs, ts = [], []
                for hh in range(2):
                    col = d * hpg + 2 * pr + hh
                    th = tot[:, col:col + 1]
                    erow = ext[col:col + 1, :]
                    din_dt = jnp.exp(th - erow) if d == 0 else jnp.exp(erow)
                    rs.append(_dot(bt * din_dt, xa))
                    ts.append(jnp.exp(th))
                srefs[d][c, pr] = jnp.where(first_half_s, rs[0], rs[1])
                r8p = pl.multiple_of((c * npair + pr) * SUBLANE, SUBLANE)
                drefs[d][pl.ds(r8p, SUBLANE), :] = jnp.broadcast_to(jnp.where(first_half1, ts[0], ts[1]), (SUBLANE, LANE))

    lax.fori_loop(0, nc, lambda i, carry: (activations(i), carry)[1], 0, unroll=4 if nc % 4 == 0 else 1)
    lax.fori_loop(0, nc, lambda i, carry: (local_states(i), carry)[1], 0, unroll=8 if nc % 8 == 0 else 1)

    def recur(d):
        def step(i, cur):
            c = i if d == 0 else nc - 1 - i
            out = []
            for pr in range(npair):
                r8p = pl.multiple_of((c * npair + pr) * SUBLANE, SUBLANE)
                loc = srefs[d][c, pr]
                srefs[d][c, pr] = cur[pr]
                out.append(drefs[d][pl.ds(r8p, 1), :] * cur[pr] + loc)
            return tuple(out)
        return lax.fori_loop(0, nc, step, tuple(h0_ref[d, pr] for pr in range(npair)))

    for d in range(2):
        fin = recur(d)
        for pr in range(npair):
            fin_ref[d, pr] = fin[pr]

    lower = coli <= rowi
    upper = coli >= rowi

    def output(c, carry):
        r0 = pl.multiple_of(c * cq, cq)
        r8 = pl.multiple_of(c * SUBLANE, SUBLANE)
        ex = ex_ref[pl.ds(r0, cq), :]
        ext = ext_ref[pl.ds(r8, SUBLANE), :]
        tot = tot_ref[pl.ds(r8, 1), :]
        cb = cb_ref[pl.ds(r0, cq), :]
        cm = ba_ref[pl.ds(r0, cq), 0:ns]
        for pr in range(npair):
            lanes = slice(pr * LANE, (pr + 1) * LANE)
            xa = xa_ref[pl.ds(r0, cq), lanes]
            so = _dot(cm, jnp.concatenate([sf_ref[c, pr], sb_ref[c, pr]], axis=1))
            dsk = jnp.where(first_half1, dskip[:, 2 * pr:2 * pr + 1], dskip[:, 2 * pr + 1:2 * pr + 2])
            ydiag, ecfs, ecbs = [], [], []
            for hh in range(2):
                cf, cbk = 2 * pr + hh, hpg + 2 * pr + hh
                ecf = jnp.broadcast_to(ex[:, cf:cf + 1], (cq, cq))
                ecb = jnp.broadcast_to(ex[:, cbk:cbk + 1], (cq, cq))
                lf = jnp.where(lower, jnp.exp(ecf - ext[cf:cf + 1, :]), 0.0)
                lb = jnp.where(upper, jnp.exp(ext[cbk:cbk + 1, :] - ecb), 0.0)
                ydiag.append(_dot(cb * (lf + lb), xa))
                ecfs.append(ecf)
                ecbs.append(tot[:, cbk:cbk + 1] - ecb)
            dout_f = jnp.exp(jnp.where(first_half, ecfs[0], ecfs[1]))
            dout_b = jnp.exp(jnp.where(first_half, ecbs[0], ecbs[1]))
            yoff = so[:, 0:LANE] * dout_f + so[:, LANE:2 * LANE] * dout_b
            y_ref[pl.ds(r0, cq), lanes] = jnp.where(first_half, ydiag[0], ydiag[1]) + yoff + dsk * xa
        return carry

    lax.fori_loop(0, nc, output, 0, unroll=2 if nc % 2 == 0 else 1)


def _ssd_branch(proj, bs, seq, lw, h0, cols):
    p_xm, p_bc, p_dt, inner = cols
    gw = inner // M2_GROUPS
    npair = gw // LANE
    nc = seq // M2_CHUNK
    bxm, bbc, bdt = p_xm // gw, p_bc // LANE, p_dt // LANE
    y, fin = pl.pallas_call(
        functools.partial(_ssd_kernel, seq=seq),
        out_shape=(jax.ShapeDtypeStruct((bs * seq, inner), F32),
                   jax.ShapeDtypeStruct((bs, M2_GROUPS, 2, npair, M2_STATE, LANE), F32)),
        grid=(bs, M2_GROUPS),
        in_specs=[pl.BlockSpec((seq, gw), lambda b, g: (b, bxm + g)),
                  pl.BlockSpec((seq, LANE), lambda b, g: (b, bbc + g)),
                  pl.BlockSpec((seq, LANE), lambda b, g: (b, bdt + g)),
                  pl.BlockSpec((None, SUBLANE, gw), lambda b, g: (g, 0, 0)),
                  pl.BlockSpec((None, SUBLANE, LANE), lambda b, g: (g, 0, 0)),
                  pl.BlockSpec((None, SUBLANE, LANE), lambda b, g: (g, 0, 0)),
                  pl.BlockSpec((None, None, 2, npair, M2_STATE, LANE), lambda b, g: (b, g, 0, 0, 0, 0))],
        out_specs=(pl.BlockSpec((seq, gw), lambda b, g: (b, g)),
                   pl.BlockSpec((None, None, 2, npair, M2_STATE, LANE), lambda b, g: (b, g, 0, 0, 0, 0))),
        scratch_shapes=[pltpu.VMEM((seq + 2 * SUBLANE, gw), F32), pltpu.VMEM((seq + 2 * SUBLANE, LANE), F32),
                        pltpu.VMEM((seq, gw), F32), pltpu.VMEM((seq, LANE), F32),
                        pltpu.VMEM((seq, LANE), F32), pltpu.VMEM((seq, LANE), F32),
                        pltpu.VMEM((nc * SUBLANE, LANE), F32), pltpu.VMEM((nc * SUBLANE, LANE), F32),
                        pltpu.VMEM((nc, npair, M2_STATE, LANE), F32), pltpu.VMEM((nc, npair, M2_STATE, LANE), F32),
                        pltpu.VMEM((nc * npair * SUBLANE, LANE), F32), pltpu.VMEM((nc * npair * SUBLANE, LANE), F32)],
        compiler_params=_cparams(("arbitrary", "arbitrary")),
        name="ssd",
    )(proj, proj, proj, lw["m2_cwx"], lw["m2_cwb"], lw["m2_pp"], h0)
    return y, fin


def _final_kernel(x_ref, y5_ref, u_ref, ys_ref, z_ref, hy_ref, g0_ref, g1_ref, g2_ref,
                  al_ref, sh_ref, sc_ref, d5_ref, wglu_ref, ng_ref, wm2_ref, wlru_ref, wo_ref, n2g_ref, wr_ref,
                  x1_ref, h2_ref, aff_ref):
    d = x_ref.shape[1]
    t5 = jax.nn.gelu(y5_ref[...] + d5_ref[...] * u_ref[...])
    vg = _dot(t5, wglu_ref[...])
    ya = vg[:, :d] * jax.nn.sigmoid(vg[:, d:])
    tb = _rms_scale(ys_ref[...] * _silu(z_ref[...])) * ng_ref[...]
    yb = _dot(tb, wm2_ref[...])
    yc = _dot(hy_ref[...], wlru_ref[...])
    merged = (jax.nn.sigmoid(g0_ref[...]) * ya + jax.nn.sigmoid(g1_ref[...]) * yb) + jax.nn.sigmoid(g2_ref[...]) * yc
    x1 = x_ref[...] + al_ref[...] * _dot(merged, wo_ref[...])
    x1_ref[...] = x1
    h2 = _rms_scale(x1) * n2g_ref[...] * (1.0 + sc_ref[...]) + sh_ref[...]
    h2_ref[...] = h2.astype(BF16)
    logits = _dot3(h2, wr_ref[...])
    valid = lax.broadcasted_iota(I32, logits.shape, 1) < N_EXPERTS
    logits = jnp.where(valid, logits, -jnp.inf)
    m = jnp.max(logits, axis=-1, keepdims=True)
    e = jnp.where(valid, jnp.exp(logits - m), 0.0)
    aff_ref[...] = e / jnp.sum(e, axis=-1, keepdims=True)


def _final(x2d, proj, y5, yssd, hy, seq, mods, lw, cols):
    t, d = x2d.shape
    tm = min(256, seq)
    alpha, shift2, scale2 = mods
    mi = _mod_index(alpha.shape[0], tm, seq)
    p_z, p_g, p_u, s5w = cols
    row = lambda w: pl.BlockSpec((tm, w), lambda i: (i, 0))
    pcol = lambda w, off: pl.BlockSpec((tm, w), lambda i: (i, off // w))
    full = lambda a: pl.BlockSpec(a.shape, lambda i: (0,) * a.ndim, pipeline_mode=pl.Buffered(1))
    mspec = pl.BlockSpec((None, 1, d), mi)
    weights = [lw["s5_d"], lw["s5_w_glu"], lw["m2_norm_g"], lw["m2_w_out"], lw["lru_w_out"], lw["w_o"],
               lw["norm2_g"], lw["w_router"]]
    return pl.pallas_call(
        _final_kernel,
        out_shape=(jax.ShapeDtypeStruct((t, d), F32), jax.ShapeDtypeStruct((t, d), BF16),
                   jax.ShapeDtypeStruct((t, LANE), F32)),
        grid=(t // tm,),
        in_specs=[row(d), row(s5w), pcol(s5w, p_u), row(d), pcol(d, p_z), row(hy.shape[1]),
                  pcol(d, p_g), pcol(d, p_g + d), pcol(d, p_g + 2 * d), mspec, mspec, mspec]
                 + [full(w) for w in weights],
        out_specs=(row(d), row(d), row(LANE)),
        compiler_params=_cparams(("arbitrary",)),
        name="merge_out",
    )(x2d, y5, proj, yssd, proj, hy, proj, proj, proj, alpha, shift2, scale2, *weights)


def _topk_kernel(aff_ref, slot_ref, start_ref, *, cap):
    a = aff_ref[...]
    nblk, ne, _ = a.shape
    key = lax.bitcast_convert_type(a, I32)

    def count(m):
        return jnp.sum(jnp.sum(m.astype(I32), axis=0, keepdims=True), axis=2, keepdims=True)

    def body(i, lo):
        cand = lo | (jnp.int32(1) << (30 - i))
        return jnp.where(count(key >= cand) >= cap, cand, lo)

    kth = lax.fori_loop(0, 31, body, jnp.zeros((1, ne, 1), I32))
    gt = key > kth
    eq = key == kth
    need = cap - count(gt)
    rowi = lax.broadcasted_iota(I32, (LANE, LANE), 0)
    coli = lax.broadcasted_iota(I32, (LANE, LANE), 1)
    upper = (rowi <= coli).astype(BF16)

    def exclusive_rank(m):
        mf = m.astype(F32)
        incl = jnp.dot(mf.reshape(nblk * ne, LANE).astype(BF16), upper, preferred_element_type=F32).reshape(nblk, ne, LANE)
        offs, run = [], jnp.zeros((1, ne, 1), F32)
        for k in range(nblk):
            offs.append(run)
            run = run + incl[k:k + 1, :, LANE - 1:LANE]
        offs = jnp.concatenate(offs, axis=0)
        return (incl - mf + offs).astype(I32), offs.astype(I32)

    sel = gt | (eq & (exclusive_rank(eq)[0] < need))
    rank, offs = exclusive_rank(sel)
    slot_ref[...] = jnp.where(sel, rank, -1)
    start_ref[...] = jnp.broadcast_to(offs, start_ref.shape)


def _topk_slots(aff_t, cap):
    bs, nblk, ne, _ = aff_t.shape
    spec = pl.BlockSpec((None, nblk, ne, LANE), lambda b: (b, 0, 0, 0))
    return pl.pallas_call(
        functools.partial(_topk_kernel, cap=cap),
        out_shape=(jax.ShapeDtypeStruct(aff_t.shape, I32), jax.ShapeDtypeStruct(aff_t.shape, I32)),
        grid=(bs,),
        in_specs=[spec],
        out_specs=(spec, spec),
        compiler_params=_cparams(("arbitrary",)),
        name="route_topk",
    )(aff_t)


def _moe_ffn_kernel(start_ref, h_ref, slot_ref, w1_ref, w3_ref, w2_ref, y_ref, xs_ref, *, cap, tw, sb):
    e, bstep = pl.program_id(0), pl.program_id(1)
    bb, nblk = slot_ref.shape[0], slot_ref.shape[1]
    bpw = tw // LANE
    nw = nblk // bpw
    sidx = lax.broadcasted_iota(I32, (sb, LANE), 0)
    for bi in range(bb):
        base = ((bstep * bb + bi) * pl.num_programs(0) + e) * nblk
        begins = [start_ref[base + k * bpw] for k in range(nw)]
        ends = begins[1:] + [cap]
        for j in range(cap // sb):
            lo, hi = j * sb, (j + 1) * sb
            rows = slice(bi * cap + lo, bi * cap + hi)
            k_lo = sum(jnp.asarray(en <= lo, I32) for en in ends)
            k_hi = sum(jnp.asarray(bg < hi, I32) for bg in begins)
            xs_ref[rows, :] = jnp.zeros((sb, xs_ref.shape[1]), F32)

            def body(k, carry, lo=lo, rows=rows, bi=bi):
                t0 = pl.multiple_of(k * tw, tw)
                p = jnp.concatenate([(slot_ref[bi, pl.ds(k * bpw + i, 1), :] == sidx + lo).astype(BF16)
                                     for i in range(bpw)], axis=1)
                xs_ref[rows, :] += jnp.dot(p, h_ref[bi, pl.ds(t0, tw), :], preferred_element_type=F32)
                return carry

            lax.fori_loop(k_lo, k_hi, body, 0)
    xs = xs_ref[...].astype(BF16)
    hid = _silu(jnp.dot(xs, w1_ref[...], preferred_element_type=F32)) * jnp.dot(xs, w3_ref[...], preferred_element_type=F32)
    y = jnp.dot(hid.astype(BF16), w2_ref[...], preferred_element_type=F32).astype(BF16)
    for bi in range(bb):
        y_ref[bi] = y[bi * cap:(bi + 1) * cap]


def _moe_ffn(h2, slot_e, starts, w1, w3, w2, layer, cap):
    bs, n, d = h2.shape
    ne, nblk = slot_e.shape[1], slot_e.shape[2]
    tw = min(512, n)
    sb = min(LANE, cap)
    bb = max(1, min(bs, 512 // cap))
    while bs % bb:
        bb -= 1
    wspec = lambda w: pl.BlockSpec((None, None) + w.shape[2:], lambda e, b, s: (layer, e, 0, 0))
    return pl.pallas_call(
        functools.partial(_moe_ffn_kernel, cap=cap, tw=tw, sb=sb),
        out_shape=jax.ShapeDtypeStruct((bs, ne, cap, d), BF16),
        grid_spec=pltpu.PrefetchScalarGridSpec(
            num_scalar_prefetch=1,
            grid=(ne, bs // bb),
            in_specs=[pl.BlockSpec((bb, n, d), lambda e, b, s: (b, 0, 0)),
                      pl.BlockSpec((bb, None, nblk, LANE), lambda e, b, s: (b, e, 0, 0)),
                      wspec(w1), wspec(w3), wspec(w2)],
            out_specs=pl.BlockSpec((bb, None, cap, d), lambda e, b, s: (b, e, 0, 0)),
            scratch_shapes=[pltpu.VMEM((bb * cap, d), F32)]),
        compiler_params=_cparams(("arbitrary", "arbitrary")),
        name="moe_ffn",
    )(starts, h2, slot_e, w1, w3, w2)


def _combine_kernel(x_ref, y_ref, slot_ref, aff_ref, al_ref, fg_ref, o_ref, *, cap, final_norm):
    slot = slot_ref[...]
    aff = aff_ref[...]
    tq = slot.shape[0]
    sidx = lax.broadcasted_iota(I32, (tq, cap), 1)
    acc = jnp.zeros(x_ref.shape, F32)
    for e in range(N_EXPERTS):
        pt = (slot[:, e:e + 1] == sidx).astype(BF16)
        acc = acc + aff[:, e:e + 1] * jnp.dot(pt, y_ref[e], preferred_element_type=F32)
    x2 = x_ref[...] + al_ref[...] * acc
    if final_norm:
        x2 = _rms_scale(x2) * fg_ref[...]
    o_ref[...] = x2


def _combine(x2d, y, slot_t, aff, seq, alpha, final_g, final_norm):
    t, d = x2d.shape
    bs, ne, cap, _ = y.shape
    tq = min(512, seq)
    nq = seq // tq
    mi = (lambda i: (0, 0, 0)) if alpha.shape[0] == 1 else (lambda i: (i // nq, 0, 0))
    return pl.pallas_call(
        functools.partial(_combine_kernel, cap=cap, final_norm=final_norm),
        out_shape=jax.ShapeDtypeStruct((t, d), F32),
        grid=(t // tq,),
        in_specs=[pl.BlockSpec((tq, d), lambda i: (i, 0)),
                  pl.BlockSpec((None, ne, cap, d), lambda i: (i // nq, 0, 0, 0)),
                  pl.BlockSpec((tq, LANE), lambda i: (i, 0)),
                  pl.BlockSpec((tq, LANE), lambda i: (i, 0)),
                  pl.BlockSpec((None, 1, d), mi),
                  pl.BlockSpec((1, d), lambda i: (0, 0))],
        out_specs=pl.BlockSpec((tq, d), lambda i: (i, 0)),
        compiler_params=_cparams(("arbitrary",)),
        name="moe_combine",
    )(x2d, y, slot_t, aff, alpha, final_g.reshape(1, d))


def _moe(x1, h2, aff, bs, seq, alpha, moe_w, layer, final_g, final_norm):
    d = x1.shape[1]
    cap = CAPACITY * seq // N_EXPERTS
    nblk = seq // LANE
    aff_t = aff[:, :N_EXPERTS].reshape(bs, nblk, LANE, N_EXPERTS).transpose(0, 1, 3, 2)
    slot, start = _topk_slots(aff_t, cap)
    slot_e = slot.transpose(0, 2, 1, 3)
    slot_t = slot.transpose(0, 1, 3, 2).reshape(bs * seq, N_EXPERTS)
    slot_t = jnp.pad(slot_t, ((0, 0), (0, LANE - N_EXPERTS)), constant_values=-1)
    starts = start[:, :, :, 0].transpose(0, 2, 1).reshape(-1)
    y = _moe_ffn(h2.reshape(bs, seq, d), slot_e, starts, *moe_w, layer, cap)
    return _combine(x1, y, slot_t, aff, seq, alpha, final_g, final_norm)


def _layout(d):
    s5w, inner, lruw = d // 2, d, d // 2
    gn = M2_GROUPS * M2_STATE
    heads = inner // M2_HEAD_DIM
    o_u = 0
    o_z = o_u + s5w
    o_xbc = o_z + inner
    o_dt = o_xbc + inner + 2 * gn
    o_xl = o_dt + 2 * heads
    o_gl = o_xl + lruw
    o_g = o_gl + lruw
    d_in = o_g + N_BRANCH * d
    p_z = 0
    p_g = p_z + inner
    p_xm = p_g + N_BRANCH * d
    p_bc = p_xm + inner
    p_u = p_bc + M2_GROUPS * LANE
    p_xl = p_u + s5w
    p_gl = p_xl + lruw
    p_dt = p_gl + lruw
    npad = p_dt + M2_GROUPS * LANE
    hpg = heads // M2_GROUPS
    assert hpg == 4 and M2_STATE == 64 and M2_HEAD_DIM == 64, "SSD kernel packs two 64-wide heads per lane tile"
    perm = np.full((npad,), d_in, np.int32)
    perm[p_z:p_z + inner] = o_z + np.arange(inner)
    perm[p_g:p_g + N_BRANCH * d] = o_g + np.arange(N_BRANCH * d)
    perm[p_xm:p_xm + inner] = o_xbc + np.arange(inner)
    bcp = np.zeros((M2_GROUPS * LANE,), np.int32)
    for g in range(M2_GROUPS):
        bcp[g * LANE:g * LANE + M2_STATE] = inner + gn + g * M2_STATE + np.arange(M2_STATE)
        bcp[g * LANE + M2_STATE:(g + 1) * LANE] = inner + g * M2_STATE + np.arange(M2_STATE)
    perm[p_bc:p_bc + M2_GROUPS * LANE] = o_xbc + bcp
    perm[p_u:p_u + s5w] = o_u + np.arange(s5w)
    perm[p_xl:p_xl + lruw] = o_xl + np.arange(lruw)
    perm[p_gl:p_gl + lruw] = o_gl + np.arange(lruw)
    for g in range(M2_GROUPS):
        for dd in range(2):
            for j in range(hpg):
                perm[p_dt + g * LANE + dd * hpg + j] = o_dt + dd * heads + g * hpg + j
    return dict(s5w=s5w, inner=inner, lruw=lruw, heads=heads, hpg=hpg, perm=perm, bcp=bcp, npad=npad,
                p_z=p_z, p_g=p_g, p_xm=p_xm, p_bc=p_bc, p_u=p_u, p_xl=p_xl, p_gl=p_gl, p_dt=p_dt)


def _rows8(rows, width):
    out = jnp.zeros((SUBLANE, width), F32)
    for i, r in enumerate(rows):
        out = out.at[i, :r.shape[0]].set(r.astype(F32))
    return out


def _pack_layer(lay, p, q_lat, q_ctx):
    d = p["w_in"].shape[0]
    inner, lruw, hpg, heads = lay["inner"], lay["lruw"], lay["hpg"], lay["heads"]
    lw = {}
    w_ext = jnp.concatenate([p["w_in"], jnp.zeros((d, 1), p["w_in"].dtype)], axis=1)
    lw["w_in"] = w_ext[:, lay["perm"]].astype(BF16)
    lw["s5_lat"] = _s5_weights(p["s5_lam_re"], p["s5_lam_im"], p["s5_log_step"], p["s5_b_re"], p["s5_b_im"],
                               p["s5_c_re"], p["s5_c_im"], q_lat)
    lw["s5_ctx"] = lw["s5_lat"] if q_ctx == q_lat else _s5_weights(
        p["s5_lam_re"], p["s5_lam_im"], p["s5_log_step"], p["s5_b_re"], p["s5_b_im"], p["s5_c_re"], p["s5_c_im"], q_ctx)
    gw = inner // M2_GROUPS
    cw, cb = p["m2_conv_w"], p["m2_conv_b"]
    lw["m2_cwx"] = jnp.stack([_rows8([cw[k, g * gw:(g + 1) * gw] for k in range(CONV_W)] + [cb[g * gw:(g + 1) * gw]], gw)
                              for g in range(M2_GROUPS)])
    cwb, cbb = cw[:, lay["bcp"]], cb[lay["bcp"]]
    lw["m2_cwb"] = jnp.stack([_rows8([cwb[k, g * LANE:(g + 1) * LANE] for k in range(CONV_W)] + [cbb[g * LANE:(g + 1) * LANE]], LANE)
                              for g in range(M2_GROUPS)])
    dtb = p["m2_dt_bias"].reshape(2, M2_GROUPS, hpg)
    alog = p["m2_a_log"].reshape(2, M2_GROUPS, hpg)
    dsk = p["m2_d"].reshape(M2_GROUPS, hpg)
    lw["m2_pp"] = jnp.stack([_rows8([dtb[:, g].reshape(-1), alog[:, g].reshape(-1), dsk[g]], LANE) for g in range(M2_GROUPS)])
    blk = lruw // LRU_BLOCKS
    eye = jnp.eye(LRU_BLOCKS, dtype=F32)

    def dense(w):
        return jnp.einsum('hij,hk->hikj', w.astype(F32), eye).reshape(lruw, lruw)

    ncb = lruw // LANE
    assert LANE % blk == 0
    mats = [dense(p["lru_w_a"][0]), dense(p["lru_w_x"][0]), dense(p["lru_w_a"][1]), dense(p["lru_w_x"][1])]
    lw["lru_w"] = jnp.stack([jnp.concatenate([m[c * LANE:(c + 1) * LANE, c * LANE:(c + 1) * LANE] for m in mats], axis=1)
                             for c in range(ncb)]).astype(BF16)
    sl = lambda v, c: v[c * LANE:(c + 1) * LANE]
    pps = []
    for c in range(ncb):
        bias = jnp.concatenate([sl(p["lru_b_a"][0], c), sl(p["lru_b_x"][0], c), sl(p["lru_b_a"][1], c), sl(p["lru_b_x"][1], c)])
        lam = jnp.concatenate([sl(p["lru_lam"][0], c), sl(p["lru_lam"][1], c)])
        pps.append(_rows8([bias, lam, sl(p["lru_conv_b"], c)] + [sl(p["lru_conv_w"][k], c) for k in range(CONV_W)], 4 * LANE))
    lw["lru_pp"] = jnp.stack(pps)
    lw["s5_d"] = p["s5_d"].reshape(1, -1).astype(F32)
    lw["s5_w_glu"] = p["s5_w_glu"].astype(BF16)
    lw["m2_norm_g"] = p["m2_norm_g"].reshape(1, -1).astype(F32)
    lw["m2_w_out"] = p["m2_w_out"].astype(BF16)
    lw["lru_w_out"] = p["lru_w_out"].astype(BF16)
    lw["w_o"] = p["w_o"].astype(BF16)
    lw["norm2_g"] = p["norm2_g"].reshape(1, -1).astype(F32)
    lw["w_router"] = jnp.pad(p["moe_w_router"].astype(F32), ((0, 0), (0, LANE - N_EXPERTS)))
    return lw


def _mixer(x2d, bs, seq, colmajor, norm_g, shift, scale, lw, lay, h0, s5w):
    proj = _inproj(x2d, seq, norm_g, shift, scale, lw["w_in"])
    y5, f5 = _s5_branch(proj, bs, seq, colmajor, s5w, h0[0], (lay["p_u"], lay["s5w"]))
    ys, fm = _ssd_branch(proj, bs, seq, lw, h0[1], (lay["p_xm"], lay["p_bc"], lay["p_dt"], lay["inner"]))
    hy, fl = _lru_branch(proj, bs, seq, lw, h0[2], (lay["p_xl"], lay["p_gl"], lay["lruw"]))
    return proj, y5, ys, hy, (f5, fm, fl)


def kernel(x, c, ctx, c_ctx, w_mod, b_mod, norm1_g, norm2_g, w_in, s5_lam_re, s5_lam_im, s5_log_step, s5_b_re, s5_b_im, s5_c_re, s5_c_im, s5_d, s5_w_glu, m2_conv_w, m2_conv_b, m2_dt_bias, m2_a_log, m2_d, m2_norm_g, m2_w_out, lru_conv_w, lru_conv_b, lru_w_a, lru_b_a, lru_w_x, lru_b_x, lru_lam, lru_w_out, w_o, moe_w_router, moe_w1, moe_w3, moe_w2, final_norm_g):
    bsz, seq, d = x.shape
    cl = ctx.shape[1]
    depth = w_mod.shape[0]
    lay = _layout(d)
    stacked = dict(norm2_g=norm2_g, w_in=w_in, s5_lam_re=s5_lam_re, s5_lam_im=s5_lam_im, s5_log_step=s5_log_step,
                   s5_b_re=s5_b_re, s5_b_im=s5_b_im, s5_c_re=s5_c_re, s5_c_im=s5_c_im, s5_d=s5_d, s5_w_glu=s5_w_glu,
                   m2_conv_w=m2_conv_w, m2_conv_b=m2_conv_b, m2_dt_bias=m2_dt_bias, m2_a_log=m2_a_log, m2_d=m2_d,
                   m2_norm_g=m2_norm_g, m2_w_out=m2_w_out, lru_conv_w=lru_conv_w, lru_conv_b=lru_conv_b,
                   lru_w_a=lru_w_a, lru_b_a=lru_b_a, lru_w_x=lru_w_x, lru_b_x=lru_b_x, lru_lam=lru_lam,
                   lru_w_out=lru_w_out, w_o=w_o, moe_w_router=moe_w_router)
    moe_w = (moe_w1.astype(BF16), moe_w3.astype(BF16), moe_w2.astype(BF16))
    rm = -(-(bsz + 1) // SUBLANE) * SUBLANE
    c_rows = jnp.zeros((rm, d), F32).at[:bsz].set(c.astype(F32)).at[bsz].set(c_ctx.astype(F32))
    mods = _modulation(c_rows, w_mod.astype(F32), b_mod.astype(F32))

    groups = lay["s5w"] // S5_GROUP
    ncb = lay["lruw"] // LANE
    npair = lay["inner"] // M2_GROUPS // LANE
    zero_h0 = (jnp.zeros((groups, 2, bsz, LANE), F32),
               jnp.zeros((bsz, M2_GROUPS, 2, npair, M2_STATE, LANE), F32),
               jnp.zeros((bsz, ncb, 2, LANE), F32))

    xs = x.reshape(bsz * seq, d).astype(F32)
    cs = ctx.reshape(bsz * cl, d).astype(F32)
    q_lat, q_ctx = min(S5_CHUNK, seq), min(S5_CHUNK, cl)
    packed = jax.vmap(lambda p: _pack_layer(lay, p, q_lat, q_ctx))(stacked)
    for i in range(depth):
        lw = jax.tree_util.tree_map(lambda v: v[i], packed)
        mx = [mods[i, :bsz, k * d:(k + 1) * d].reshape(bsz, 1, d) for k in range(6)]
        mc = [mods[i, bsz:bsz + 1, k * d:(k + 1) * d].reshape(1, 1, d) for k in range(6)]
        fcols = (lay["p_z"], lay["p_g"], lay["p_u"], lay["s5w"])
        cproj, cy5, cys, chy, cstates = _mixer(cs, bsz, cl, False, norm1_g[i], mc[0], mc[1], lw, lay, zero_h0, lw["s5_ctx"])
        if i < depth - 1:
            c1, ch2, caff = _final(cs, cproj, cy5, cys, chy, cl, (mc[2], mc[3], mc[4]), lw, fcols)
            cs = _moe(c1, ch2, caff, bsz, cl, mc[5], moe_w, i, final_norm_g, False)
        xproj, y5, ys, hy, _ = _mixer(xs, bsz, seq, True, norm1_g[i], mx[0], mx[1], lw, lay, cstates, lw["s5_lat"])
        x1, h2, aff = _final(xs, xproj, y5, ys, hy, seq, (mx[2], mx[3], mx[4]), lw, fcols)
        xs = _moe(x1, h2, aff, bsz, seq, mx[5], moe_w, i, final_norm_g, i == depth - 1)
    return xs.reshape(bsz, seq, d).astype(x.dtype)
```

```python
import functools
import math

import numpy as np
import jax
import jax.numpy as jnp
from jax import lax
from jax.experimental import pallas as pl
from jax.experimental.pallas import tpu as pltpu

F32 = jnp.float32
BF16 = jnp.bfloat16
I32 = jnp.int32

GRID_W = 64
EPS = 1e-6
CONV_W = 4
S5_GROUP = 16
S5_STATE = 64
M2_HEAD_DIM = 64
M2_GROUPS = 4
M2_STATE = 64
M2_CHUNK = 128
LRU_BLOCKS = 8
LRU_C = 8.0
N_EXPERTS = 16
CAPACITY = 2
N_BRANCH = 3

LANE = 128
SUBLANE = 8
S5_CHUNK = 128
LRU_TILE = 64
VMEM_LIMIT = 56 * 1024 * 1024


def _cparams(sem):
    return pltpu.CompilerParams(dimension_semantics=sem, vmem_limit_bytes=VMEM_LIMIT)


def _dot(a, b):
    return jnp.dot(a.astype(BF16), b.astype(BF16), preferred_element_type=F32)


def _dot_nt(a, b):
    return lax.dot_general(a.astype(BF16), b.astype(BF16), (((1,), (1,)), ((), ())), preferred_element_type=F32)


def _split2(a):
    hi = a.astype(BF16)
    lo = (a - hi.astype(F32)).astype(BF16)
    return hi, lo


def _split3(a):
    hi = a.astype(BF16)
    r = a - hi.astype(F32)
    mid = r.astype(BF16)
    lo = (r - mid.astype(F32)).astype(BF16)
    return hi, mid, lo


def _dot3(a, b):
    ah, al = _split2(a)
    bh, bl = _split2(b)
    d = lambda x, y: jnp.dot(x, y, preferred_element_type=F32)
    return d(ah, bh) + (d(ah, bl) + d(al, bh))


def _dot_exact_lhs(m01, x):
    hi, mid, lo = _split3(x)
    d = lambda y: jnp.dot(m01, y, preferred_element_type=F32)
    return d(hi) + (d(mid) + d(lo))


def _silu(x):
    return x * jax.nn.sigmoid(x)


def _softplus(x):
    return jnp.maximum(x, 0.0) + jnp.log(1.0 + jnp.exp(-jnp.abs(x)))


def _rms_scale(x):
    return x * lax.rsqrt(jnp.mean(x * x, axis=-1, keepdims=True) + EPS)


def _mod_kernel(c_ref, w_ref, b_ref, o_ref):
    o_ref[...] = _dot3(_silu(c_ref[...]), w_ref[...]) + b_ref[...]


def _modulation(c_rows, w_mod, b_mod):
    depth, d, n6 = w_mod.shape
    rm = c_rows.shape[0]
    tn = min(1024, n6)
    return pl.pallas_call(
        _mod_kernel,
        out_shape=jax.ShapeDtypeStruct((depth, rm, n6), F32),
        grid=(depth, n6 // tn),
        in_specs=[pl.BlockSpec((rm, d), lambda l, j: (0, 0)),
                  pl.BlockSpec((None, d, tn), lambda l, j: (l, 0, j)),
                  pl.BlockSpec((None, 1, tn), lambda l, j: (l, 0, j))],
        out_specs=pl.BlockSpec((None, rm, tn), lambda l, j: (l, 0, j)),
        compiler_params=_cparams(("arbitrary", "arbitrary")),
        name="modulation",
    )(c_rows, w_mod, b_mod.reshape(depth, 1, n6))


def _inproj_kernel(x_ref, g_ref, sh_ref, sc_ref, w_ref, o_ref, *, nchunk):
    h = _rms_scale(x_ref[...]) * g_ref[...] * (1.0 + sc_ref[...]) + sh_ref[...]
    hb = h.astype(BF16)
    npad = o_ref.shape[1]
    for n0 in range(0, npad, nchunk):
        o_ref[:, n0:n0 + nchunk] = jnp.dot(hb, w_ref[:, n0:n0 + nchunk], preferred_element_type=F32)


def _mod_index(bm, tm, seq):
    if bm == 1:
        return lambda i: (0, 0, 0)
    return lambda i: ((i * tm) // seq, 0, 0)


def _inproj(x2d, seq, g, shift, scale, w):
    t, d = x2d.shape
    npad = w.shape[1]
    tm = min(256, seq)
    mi = _mod_index(shift.shape[0], tm, seq)
    return pl.pallas_call(
        functools.partial(_inproj_kernel, nchunk=512),
        out_shape=jax.ShapeDtypeStruct((t, npad), F32),
        grid=(t // tm,),
        in_specs=[pl.BlockSpec((tm, d), lambda i: (i, 0)),
                  pl.BlockSpec((1, d), lambda i: (0, 0)),
                  pl.BlockSpec((None, 1, d), mi),
                  pl.BlockSpec((None, 1, d), mi),
                  pl.BlockSpec((d, npad), lambda i: (0, 0), pipeline_mode=pl.Buffered(1))],
        out_specs=pl.BlockSpec((tm, npad), lambda i: (i, 0)),
        compiler_params=_cparams(("arbitrary",)),
        name="inproj",
    )(x2d, g.reshape(1, d), shift, scale, w)


def _build_toeplitz(cp_ref, cn_ref, t_ref, q):
    trow = lax.broadcasted_iota(I32, (q, q), 0)
    tcol = lax.broadcasted_iota(I32, (q, q), 1)
    causal = tcol >= trow

    def body(j, carry):
        cp = cp_ref[j]
        cn = cn_ref[j]
        r0 = pl.multiple_of(j * q, q)
        for i in range(S5_GROUP):
            a = pltpu.roll(jnp.broadcast_to(cp[i:i + 1, :], (q, q)), 0, 1, stride=1, stride_axis=0)
            b = pltpu.roll(jnp.broadcast_to(cn[i:i + 1, :], (q, q)), 0, 1, stride=1, stride_axis=0)
            t_ref[pl.ds(r0, q), i * q:(i + 1) * q] = jnp.where(causal, a, b).astype(BF16)
        return carry

    lax.fori_loop(0, S5_GROUP, body, 0)


def _s5_kernel(u_ref, cp_ref, cn_ref, wst_ref, wout_ref, dec_ref, h0_ref, y_ref, fin_ref, ef_ref, eb_ref, t_ref, *, nc, bs):
    q = u_ref.shape[2]
    _build_toeplitz(cp_ref, cn_ref, t_ref, q)
    a = jnp.concatenate([u_ref[j] for j in range(S5_GROUP)], axis=-1).astype(BF16)
    loc = jnp.dot(a, wst_ref[...], preferred_element_type=F32)
    dec = dec_ref[...]
    half = S5_STATE

    def step(cur, da, db, add):
        return da * cur + db * pltpu.roll(cur, half, 1) + add

    cur = h0_ref[0]
    for c in range(nc):
        ef_ref[c * bs:(c + 1) * bs, :] = cur
        cur = step(cur, dec[0:1], dec[1:2], loc[c * bs:(c + 1) * bs, 0:LANE])
    fin_ref[0] = cur
    cur = h0_ref[1]
    for c in reversed(range(nc)):
        eb_ref[c * bs:(c + 1) * bs, :] = cur
        cur = step(cur, dec[2:3], dec[3:4], loc[c * bs:(c + 1) * bs, LANE:2 * LANE])
    fin_ref[1] = cur
    e = jnp.concatenate([ef_ref[...], eb_ref[...]], axis=-1).astype(BF16)
    kq = S5_GROUP * q
    t_ref[kq:kq + 2 * LANE, :] = wout_ref[...]
    acc = jnp.dot(jnp.concatenate([a, e], axis=-1), t_ref[...], preferred_element_type=F32)
    for i in range(S5_GROUP):
        y_ref[i] = acc[:, i * q:(i + 1) * q]


def _s5_scan(ut, s5w, h0, nc, bs):
    cpos, cneg, wst, wout, dec = s5w
    groups = cpos.shape[0]
    r, q = ut.shape[1], ut.shape[2]
    assert q == LANE, "the Toeplitz builder rotates one 128-lane tile per block"
    kq = S5_GROUP * q
    lagspec = pl.BlockSpec((None, S5_GROUP, S5_GROUP, q), lambda g: (g, 0, 0, 0))
    return pl.pallas_call(
        functools.partial(_s5_kernel, nc=nc, bs=bs),
        out_shape=(jax.ShapeDtypeStruct(ut.shape, F32), jax.ShapeDtypeStruct((groups, 2, bs, LANE), F32)),
        grid=(groups,),
        in_specs=[pl.BlockSpec((S5_GROUP, r, q), lambda g: (g, 0, 0)),
                  lagspec, lagspec,
                  pl.BlockSpec((None, kq, 2 * LANE), lambda g: (g, 0, 0)),
                  pl.BlockSpec((None, 2 * LANE, kq), lambda g: (g, 0, 0)),
                  pl.BlockSpec((None, SUBLANE, LANE), lambda g: (g, 0, 0)),
                  pl.BlockSpec((None, 2, bs, LANE), lambda g: (g, 0, 0, 0))],
        out_specs=(pl.BlockSpec((S5_GROUP, r, q), lambda g: (g, 0, 0)),
                   pl.BlockSpec((None, 2, bs, LANE), lambda g: (g, 0, 0, 0))),
        scratch_shapes=[pltpu.VMEM((r, LANE), F32), pltpu.VMEM((r, LANE), F32), pltpu.VMEM((kq + 2 * LANE, kq), BF16)],
        compiler_params=_cparams(("arbitrary",)),
        name="s5_scan",
    )(ut, cpos, cneg, wst, wout, dec, h0)


def _s5_weights(lam_re, lam_im, log_step, b_re, b_im, c_re, c_im, q):
    hp = lax.Precision.HIGHEST
    g, p = lam_re.shape[1], lam_re.shape[2]
    ii = S5_GROUP
    tau = jnp.arange(q + 1, dtype=F32)[:, None, None]
    ks, wsts, wouts, decs = [], [], [], []
    for d in range(2):
        lr, li = lam_re[d].astype(F32), lam_im[d].astype(F32)
        step = jnp.exp(log_step[d].astype(F32))[:, None]
        ar, ai = lr * step, li * step
        mag = jnp.exp(tau * ar)
        pre, pim = mag * jnp.cos(tau * ai), mag * jnp.sin(tau * ai)
        bar_re, bar_im = pre[1], pim[1]
        den = lr * lr + li * li
        nr, ni = bar_re - 1.0, bar_im
        coef_re = (nr * lr + ni * li) / den
        coef_im = (ni * lr - nr * li) / den
        bre, bim = b_re[d].astype(F32), b_im[d].astype(F32)
        bb_re = coef_re[..., None] * bre - coef_im[..., None] * bim
        bb_im = coef_re[..., None] * bim + coef_im[..., None] * bre
        cre, cim = c_re[d].astype(F32), c_im[d].astype(F32)
        cp_re = cre[None] * pre[:, :, None, :] - cim[None] * pim[:, :, None, :]
        cp_im = cre[None] * pim[:, :, None, :] + cim[None] * pre[:, :, None, :]
        k = (jnp.einsum('tgip,gpj->tgij', cp_re[:q], bb_re, precision=hp)
             - jnp.einsum('tgip,gpj->tgij', cp_im[:q], bb_im, precision=hp))
        ks.append(k)
        pw_re = pre[q - 1::-1][:q] if d == 0 else pre[:q]
        pw_im = pim[q - 1::-1][:q] if d == 0 else pim[:q]
        w_re = pw_re[:, :, :, None] * bb_re[None] - pw_im[:, :, :, None] * bb_im[None]
        w_im = pw_re[:, :, :, None] * bb_im[None] + pw_im[:, :, :, None] * bb_re[None]
        w = jnp.concatenate([w_re, w_im], axis=2)
        wsts.append(jnp.transpose(w, (1, 3, 0, 2)).reshape(g, ii * q, 2 * p))
        if d == 0:
            o_re, o_im = cp_re[1:q + 1], cp_im[1:q + 1]
        else:
            o_re, o_im = cp_re[q:0:-1], cp_im[q:0:-1]
        o = jnp.concatenate([o_re, -o_im], axis=3)
        wouts.append(jnp.transpose(o, (1, 3, 2, 0)).reshape(g, 2 * p, ii * q))
        dr, di = pre[q], pim[q]
        decs.append(jnp.concatenate([dr, dr], axis=-1))
        decs.append(jnp.concatenate([-di, di], axis=-1))
    cpos = jnp.transpose(ks[0].at[0].add(ks[1][0]), (1, 3, 2, 0))
    kb_rev = ks[1][::-1]
    cneg = jnp.concatenate([jnp.zeros_like(kb_rev[:1]), kb_rev[:q - 1]], axis=0)
    cneg = jnp.transpose(cneg, (1, 3, 2, 0))
    wst = jnp.concatenate(wsts, axis=-1).astype(BF16)
    wout = jnp.concatenate(wouts, axis=1).astype(BF16)
    dec = jnp.stack(decs + decs, axis=1)
    return cpos, cneg, wst, wout, dec


def _s5_in_kernel(x_ref, o_ref, *, cpc, colmajor):
    q = o_ref.shape[3]
    for b in range(x_ref.shape[0]):
        for cc in range(o_ref.shape[1]):
            if colmajor:
                x = jnp.concatenate([x_ref[b, :, cc * cpc + w, :] for w in range(cpc)], axis=0)
            else:
                x = x_ref[b, cc * q:(cc + 1) * q, :]
            o_ref[:, cc, b, :] = x.T


def _s5_out_kernel(y_ref, o_ref, *, cpc, colmajor):
    q = y_ref.shape[3]
    for b in range(o_ref.shape[0]):
        for cc in range(y_ref.shape[1]):
            t = y_ref[:, cc, b, :].T
            if colmajor:
                rows = q // cpc
                for w in range(cpc):
                    o_ref[b, :, cc * cpc + w, :] = t[w * rows:(w + 1) * rows]
            else:
                o_ref[b, cc * q:(cc + 1) * q, :] = t


def _s5_layout(bs, seq, colmajor, q):
    bb = min(SUBLANE, bs)
    while bs % bb:
        bb -= 1
    nc = seq // q
    if not colmajor:
        return bb, nc, 1, 1, None
    rows = seq // GRID_W
    assert q % rows == 0, "a scan chunk must cover whole grid columns"
    cpc = q // rows
    wb = max(SUBLANE, cpc)
    return bb, wb // cpc, GRID_W // wb, cpc, (rows, wb)


def _s5_branch(proj, bs, seq, colmajor, s5w, h0, cols):
    p_u, width = cols
    q = min(S5_CHUNK, seq)
    nc = seq // q
    bb, nch, nj, cpc, cm = _s5_layout(bs, seq, colmajor, q)
    cb = p_u // width
    cmaj = jax.ShapeDtypeStruct((width, nc, bs, q), F32)
    cspec = pl.BlockSpec((width, nch, bb, q), lambda i, j: (0, j, i, 0))
    if colmajor:
        rows, wb = cm
        tok_in = pl.BlockSpec((bb, rows, wb, width), lambda i, j: (i, 0, j, cb))
        tok_out = pl.BlockSpec((bb, rows, wb, width), lambda i, j: (i, 0, j, 0))
        x_in = proj.reshape(bs, rows, GRID_W, proj.shape[1])
        tok_shape = jax.ShapeDtypeStruct((bs, rows, GRID_W, width), F32)
    else:
        tok_in = pl.BlockSpec((bb, seq, width), lambda i, j: (i, 0, cb))
        tok_out = pl.BlockSpec((bb, seq, width), lambda i, j: (i, 0, 0))
        x_in = proj.reshape(bs, seq, proj.shape[1])
        tok_shape = jax.ShapeDtypeStruct((bs, seq, width), F32)
    ut = pl.pallas_call(
        functools.partial(_s5_in_kernel, cpc=cpc, colmajor=colmajor), out_shape=cmaj, grid=(bs // bb, nj),
        in_specs=[tok_in], out_specs=cspec, compiler_params=_cparams(("arbitrary", "arbitrary")), name="s5_to_channel_major",
    )(x_in)
    yt, fin = _s5_scan(ut.reshape(width, nc * bs, q), s5w, h0, nc, bs)
    y = pl.pallas_call(
        functools.partial(_s5_out_kernel, cpc=cpc, colmajor=colmajor), out_shape=tok_shape, grid=(bs // bb, nj),
        in_specs=[cspec], out_specs=tok_out, compiler_params=_cparams(("arbitrary", "arbitrary")), name="s5_to_token_major",
    )(yt.reshape(width, nc, bs, q))
    return y.reshape(bs * seq, width), fin


def _conv_rows(pad_ref, base, n, taps, bias):
    w = pad_ref[pl.ds(base, n + 2 * SUBLANE), :]
    tot = n + 2 * SUBLANE
    xm2 = pltpu.roll(w, 2, 0)[SUBLANE:SUBLANE + n]
    xm1 = pltpu.roll(w, 1, 0)[SUBLANE:SUBLANE + n]
    x0 = w[SUBLANE:SUBLANE + n]
    xp1 = pltpu.roll(w, tot - 1, 0)[SUBLANE:SUBLANE + n]
    return taps[0:1] * xm2 + taps[1:2] * xm1 + taps[2:3] * x0 + taps[3:4] * xp1 + bias


def _fill_padded(pad_ref, x_ref, seq):
    zeros = jnp.zeros((SUBLANE, pad_ref.shape[1]), F32)
    pad_ref[0:SUBLANE, :] = zeros
    pad_ref[SUBLANE + seq:2 * SUBLANE + seq, :] = zeros
    pad_ref[SUBLANE:SUBLANE + seq, :] = x_ref[...]


def _scan_tile(a, v, h, reverse):
    s = a.shape[0]
    row = lax.broadcasted_iota(I32, a.shape, 0)
    k = 1
    while k < s:
        sh = s - k if reverse else k
        ok = (row < s - k) if reverse else (row >= k)
        a_sh = pltpu.roll(a, sh, 0)
        v_sh = pltpu.roll(v, sh, 0)
        v = v + a * jnp.where(ok, v_sh, 0.0)
        a = a * jnp.where(ok, a_sh, 1.0)
        k *= 2
    return v + a * h


def _lru_kernel(x_ref, g_ref, w_ref, pp_ref, h0_ref, o_ref, fin_ref, pad_ref, a0_ref, v0_ref, a1_ref, v1_ref, *, seq, ch):
    pp = pp_ref[...]
    bias = pp[0:1]
    lam = pp[1:2]
    cb = pp[2:3, 0:LANE]
    taps = pp[3:7, 0:LANE]
    _fill_padded(pad_ref, x_ref, seq)
    av = ((a0_ref, v0_ref), (a1_ref, v1_ref))

    def gates(i, carry):
        base = pl.multiple_of(i * ch, ch)
        xc = _conv_rows(pad_ref, base, ch, taps, cb)
        gt = _dot(xc, w_ref[...]) + bias
        for d in range(2):
            r = jax.nn.sigmoid(gt[:, 2 * d * LANE:(2 * d + 1) * LANE])
            ig = jax.nn.sigmoid(gt[:, (2 * d + 1) * LANE:(2 * d + 2) * LANE])
            log_a = -LRU_C * r * _softplus(-lam[:, d * LANE:(d + 1) * LANE])
            a = jnp.exp(log_a)
            av[d][0][pl.ds(base, ch), :] = a
            av[d][1][pl.ds(base, ch), :] = jnp.sqrt(jnp.maximum(1.0 - a * a, EPS)) * (ig * xc)
        return carry

    lax.fori_loop(0, seq // ch, gates, 0, unroll=2 if (seq // ch) % 2 == 0 else 1)
    tile = min(LRU_TILE, seq)
    nt = seq // tile

    def fwd(i, h):
        r0 = pl.multiple_of(i * tile, tile)
        hall = _scan_tile(a0_ref[pl.ds(r0, tile), :], v0_ref[pl.ds(r0, tile), :], h, False)
        o_ref[pl.ds(r0, tile), :] = hall
        return hall[tile - 1:tile, :]

    h0 = h0_ref[...]
    hf = lax.fori_loop(0, nt, fwd, h0[0:1])

    def bwd(i, h):
        r0 = pl.multiple_of((nt - 1 - i) * tile, tile)
        hall = _scan_tile(a1_ref[pl.ds(r0, tile), :], v1_ref[pl.ds(r0, tile), :], h, True)
        o_ref[pl.ds(r0, tile), :] = (o_ref[pl.ds(r0, tile), :] + hall) * jax.nn.gelu(g_ref[pl.ds(r0, tile), :])
        return hall[0:1, :]

    hb = lax.fori_loop(0, nt, bwd, h0[1:2])
    fin_ref[...] = jnp.concatenate([hf, hb], axis=0)


def _lru_branch(proj, bs, seq, lw, h0, cols):
    p_xl, p_gl, width = cols
    ncb = width // LANE
    bx, bg = p_xl // LANE, p_gl // LANE
    ch = min(256, seq)
    sc = pltpu.VMEM((seq, LANE), F32)
    hy, fin = pl.pallas_call(
        functools.partial(_lru_kernel, seq=seq, ch=ch),
        out_shape=(jax.ShapeDtypeStruct((bs * seq, width), F32), jax.ShapeDtypeStruct((bs, ncb, 2, LANE), F32)),
        grid=(bs, ncb),
        in_specs=[pl.BlockSpec((seq, LANE), lambda b, c: (b, bx + c)),
                  pl.BlockSpec((seq, LANE), lambda b, c: (b, bg + c)),
                  pl.BlockSpec((None, LANE, 4 * LANE), lambda b, c: (c, 0, 0)),
                  pl.BlockSpec((None, SUBLANE, 4 * LANE), lambda b, c: (c, 0, 0)),
                  pl.BlockSpec((None, None, 2, LANE), lambda b, c: (b, c, 0, 0))],
        out_specs=(pl.BlockSpec((seq, LANE), lambda b, c: (b, c)),
                   pl.BlockSpec((None, None, 2, LANE), lambda b, c: (b, c, 0, 0))),
        scratch_shapes=[pltpu.VMEM((seq + 2 * SUBLANE, LANE), F32), sc, sc, sc, sc],
        compiler_params=_cparams(("arbitrary", "arbitrary")),
        name="rglru",
    )(proj, proj, lw["lru_w"], lw["lru_pp"], h0)
    return hy, fin


def _ssd_kernel(xm_ref, bc_ref, dt_ref, cwx_ref, cwb_ref, pp_ref, h0_ref, y_ref, fin_ref,
                xpad_ref, bpad_ref, xa_ref, ba_ref, ex_ref, cb_ref, ext_ref, tot_ref,
                sf_ref, sb_ref, df_ref, db_ref, *, seq):
    cq = M2_CHUNK
    nc = seq // cq
    hd = M2_HEAD_DIM
    ns = M2_STATE
    npair = xm_ref.shape[1] // LANE
    hpg = 2 * npair
    pp = pp_ref[...]
    dt_bias, a_neg, dskip = pp[0:1], -jnp.exp(pp[1:2]), pp[2:3]
    cwx, cwb = cwx_ref[...], cwb_ref[...]
    _fill_padded(xpad_ref, xm_ref, seq)
    _fill_padded(bpad_ref, bc_ref, seq)

    rowi = lax.broadcasted_iota(I32, (cq, cq), 0)
    coli = lax.broadcasted_iota(I32, (cq, cq), 1)
    tri = (coli <= rowi).astype(BF16)
    lane = lax.broadcasted_iota(I32, (cq, LANE), 1)
    first_half = lane < hd
    first_half_s = lax.broadcasted_iota(I32, (ns, LANE), 1) < hd
    first_half1 = lax.broadcasted_iota(I32, (1, LANE), 1) < hd
    srefs, drefs = (sf_ref, sb_ref), (df_ref, db_ref)

    def conv_act(c):
        r0 = pl.multiple_of(c * cq, cq)
        xa_ref[pl.ds(r0, cq), :] = _silu(_conv_rows(xpad_ref, r0, cq, cwx[0:4], cwx[4:5]))
        ba_ref[pl.ds(r0, cq), :] = _silu(_conv_rows(bpad_ref, r0, cq, cwb[0:4], cwb[4:5]))

    def activations(c):
        r0 = pl.multiple_of(c * cq, cq)
        r8 = pl.multiple_of(c * SUBLANE, SUBLANE)
        dtv = _softplus(dt_ref[pl.ds(r0, cq), :] + dt_bias)
        la = dtv * a_neg
        cum = _dot_exact_lhs(tri, la)
        tot = cum[cq - 1:cq, :]
        ex = jnp.where(lane < hpg, cum, cum - la)
        ex_ref[pl.ds(r0, cq), :] = ex
        ldt = jnp.log(dtv.T[0:SUBLANE, :])
        ext = ex.T[0:SUBLANE, :]
        fwd_rows = lax.broadcasted_iota(I32, (SUBLANE, LANE), 0) < hpg
        ext_ref[pl.ds(r8, SUBLANE), :] = jnp.where(fwd_rows, ext - ldt, ext + ldt)
        tot_ref[pl.ds(r8, SUBLANE), :] = jnp.broadcast_to(tot, (SUBLANE, LANE))

    def local_states(c):
        r0 = pl.multiple_of(c * cq, cq)
        r8 = pl.multiple_of(c * SUBLANE, SUBLANE)
        bc = ba_ref[pl.ds(r0, cq), :]
        ext = ext_ref[pl.ds(r8, SUBLANE), :]
        tot = tot_ref[pl.ds(r8, 1), :]
        bt = bc.T[ns:2 * ns, :]
        cb_ref[pl.ds(r0, cq), :] = _dot(bc[:, 0:ns], bt)
        for d in range(2):
            for pr in range(npair):
                xa = xa_ref[pl.ds(r0, cq), pr * LANE:(pr + 1) * LANE]
                rs, ts = [], []
                for hh in range(2):
                    col = d * hpg + 2 * pr + hh
                    th = tot[:, col:col + 1]
                    erow = ext[col:col + 1, :]
                    din_dt = jnp.exp(th - erow) if d == 0 else jnp.exp(erow)
                    rs.append(_dot(bt * din_dt, xa))
                    ts.append(jnp.exp(th))
                srefs[d][c, pr] = jnp.where(first_half_s, rs[0], rs[1])
                r8p = pl.multiple_of((c * npair + pr) * SUBLANE, SUBLANE)
                drefs[d][pl.ds(r8p, SUBLANE), :] = jnp.broadcast_to(jnp.where(first_half1, ts[0], ts[1]), (SUBLANE, LANE))

    lax.fori_loop(0, nc, lambda i, carry: (conv_act(i), carry)[1], 0, unroll=2 if nc % 2 == 0 else 1)
    lax.fori_loop(0, nc, lambda i, carry: (activations(i), carry)[1], 0, unroll=8 if nc % 8 == 0 else 1)
    lax.fori_loop(0, nc, lambda i, carry: (local_states(i), carry)[1], 0, unroll=8 if nc % 8 == 0 else 1)

    def recur(d):
        def step(i, cur):
            c = i if d == 0 else nc - 1 - i
            out = []
            for pr in range(npair):
                r8p = pl.multiple_of((c * npair + pr) * SUBLANE, SUBLANE)
                loc = srefs[d][c, pr]
                srefs[d][c, pr] = cur[pr]
                out.append(drefs[d][pl.ds(r8p, 1), :] * cur[pr] + loc)
            return tuple(out)
        return lax.fori_loop(0, nc, step, tuple(h0_ref[d, pr] for pr in range(npair)))

    for d in range(2):
        fin = recur(d)
        for pr in range(npair):
            fin_ref[d, pr] = fin[pr]

    lower = coli <= rowi
    upper = coli >= rowi

    def output(c, carry):
        r0 = pl.multiple_of(c * cq, cq)
        r8 = pl.multiple_of(c * SUBLANE, SUBLANE)
        ex = ex_ref[pl.ds(r0, cq), :]
        ext = ext_ref[pl.ds(r8, SUBLANE), :]
        tot = tot_ref[pl.ds(r8, 1), :]
        cb = cb_ref[pl.ds(r0, cq), :]
        cm = ba_ref[pl.ds(r0, cq), 0:ns]
        for pr in range(npair):
            lanes = slice(pr * LANE, (pr + 1) * LANE)
            xa = xa_ref[pl.ds(r0, cq), lanes]
            so = _dot(cm, jnp.concatenate([sf_ref[c, pr], sb_ref[c, pr]], axis=1))
            dsk = jnp.where(first_half1, dskip[:, 2 * pr:2 * pr + 1], dskip[:, 2 * pr + 1:2 * pr + 2])
            ydiag, ecfs, ecbs = [], [], []
            for hh in range(2):
                cf, cbk = 2 * pr + hh, hpg + 2 * pr + hh
                ecf = jnp.broadcast_to(ex[:, cf:cf + 1], (cq, cq))
                ecb = jnp.broadcast_to(ex[:, cbk:cbk + 1], (cq, cq))
                lf = jnp.where(lower, jnp.exp(ecf - ext[cf:cf + 1, :]), 0.0)
                lb = jnp.where(upper, jnp.exp(ext[cbk:cbk + 1, :] - ecb), 0.0)
                ydiag.append(_dot(cb * (lf + lb), xa))
                ecfs.append(ecf)
                ecbs.append(tot[:, cbk:cbk + 1] - ecb)
            dout_f = jnp.exp(jnp.where(first_half, ecfs[0], ecfs[1]))
            dout_b = jnp.exp(jnp.where(first_half, ecbs[0], ecbs[1]))
            yoff = so[:, 0:LANE] * dout_f + so[:, LANE:2 * LANE] * dout_b
            y_ref[pl.ds(r0, cq), lanes] = jnp.where(first_half, ydiag[0], ydiag[1]) + yoff + dsk * xa
        return carry

    lax.fori_loop(0, nc, output, 0, unroll=2 if nc % 2 == 0 else 1)


def _ssd_branch(proj, bs, seq, lw, h0, cols):
    p_xm, p_bc, p_dt, inner = cols
    gw = inner // M2_GROUPS
    npair = gw // LANE
    nc = seq // M2_CHUNK
    bxm, bbc, bdt = p_xm // gw, p_bc // LANE, p_dt // LANE
    y, fin = pl.pallas_call(
        functools.partial(_ssd_kernel, seq=seq),
        out_shape=(jax.ShapeDtypeStruct((bs * seq, inner), F32),
                   jax.ShapeDtypeStruct((bs, M2_GROUPS, 2, npair, M2_STATE, LANE), F32)),
        grid=(bs, M2_GROUPS),
        in_specs=[pl.BlockSpec((seq, gw), lambda b, g: (b, bxm + g)),
                  pl.BlockSpec((seq, LANE), lambda b, g: (b, bbc + g)),
                  pl.BlockSpec((seq, LANE), lambda b, g: (b, bdt + g)),
                  pl.BlockSpec((None, SUBLANE, gw), lambda b, g: (g, 0, 0)),
                  pl.BlockSpec((None, SUBLANE, LANE), lambda b, g: (g, 0, 0)),
                  pl.BlockSpec((None, SUBLANE, LANE), lambda b, g: (g, 0, 0)),
                  pl.BlockSpec((None, None, 2, npair, M2_STATE, LANE), lambda b, g: (b, g, 0, 0, 0, 0))],
        out_specs=(pl.BlockSpec((seq, gw), lambda b, g: (b, g)),
                   pl.BlockSpec((None, None, 2, npair, M2_STATE, LANE), lambda b, g: (b, g, 0, 0, 0, 0))),
        scratch_shapes=[pltpu.VMEM((seq + 2 * SUBLANE, gw), F32), pltpu.VMEM((seq + 2 * SUBLANE, LANE), F32),
                        pltpu.VMEM((seq, gw), F32), pltpu.VMEM((seq, LANE), F32),
                        pltpu.VMEM((seq, LANE), F32), pltpu.VMEM((seq, LANE), F32),
                        pltpu.VMEM((nc * SUBLANE, LANE), F32), pltpu.VMEM((nc * SUBLANE, LANE), F32),
                        pltpu.VMEM((nc, npair, M2_STATE, LANE), F32), pltpu.VMEM((nc, npair, M2_STATE, LANE), F32),
                        pltpu.VMEM((nc * npair * SUBLANE, LANE), F32), pltpu.VMEM((nc * npair * SUBLANE, LANE), F32)],
        compiler_params=_cparams(("arbitrary", "arbitrary")),
        name="ssd",
    )(proj, proj, proj, lw["m2_cwx"], lw["m2_cwb"], lw["m2_pp"], h0)
    return y, fin


def _final_kernel(x_ref, y5_ref, u_ref, ys_ref, z_ref, hy_ref, g0_ref, g1_ref, g2_ref,
                  al_ref, sh_ref, sc_ref, d5_ref, wglu_ref, ng_ref, wm2_ref, wlru_ref, wo_ref, n2g_ref, wr_ref,
                  x1_ref, h2_ref, aff_ref):
    d = x_ref.shape[1]
    t5 = jax.nn.gelu(y5_ref[...] + d5_ref[...] * u_ref[...])
    vg = _dot(t5, wglu_ref[...])
    ya = vg[:, :d] * jax.nn.sigmoid(vg[:, d:])
    tb = _rms_scale(ys_ref[...] * _silu(z_ref[...])) * ng_ref[...]
    yb = _dot(tb, wm2_ref[...])
    yc = _dot(hy_ref[...], wlru_ref[...])
    merged = (jax.nn.sigmoid(g0_ref[...]) * ya + jax.nn.sigmoid(g1_ref[...]) * yb) + jax.nn.sigmoid(g2_ref[...]) * yc
    x1 = x_ref[...] + al_ref[...] * _dot(merged, wo_ref[...])
    x1_ref[...] = x1
    h2 = _rms_scale(x1) * n2g_ref[...] * (1.0 + sc_ref[...]) + sh_ref[...]
    h2_ref[...] = h2.astype(BF16)
    logits = _dot3(h2, wr_ref[...])
    valid = lax.broadcasted_iota(I32, logits.shape, 1) < N_EXPERTS
    logits = jnp.where(valid, logits, -jnp.inf)
    m = jnp.max(logits, axis=-1, keepdims=True)
    e = jnp.where(valid, jnp.exp(logits - m), 0.0)
    aff_ref[...] = e / jnp.sum(e, axis=-1, keepdims=True)


def _final(x2d, proj, y5, yssd, hy, seq, mods, lw, cols):
    t, d = x2d.shape
    tm = min(256, seq)
    alpha, shift2, scale2 = mods
    mi = _mod_index(alpha.shape[0], tm, seq)
    p_z, p_g, p_u, s5w = cols
    row = lambda w: pl.BlockSpec((tm, w), lambda i: (i, 0))
    pcol = lambda w, off: pl.BlockSpec((tm, w), lambda i: (i, off // w))
    full = lambda a: pl.BlockSpec(a.shape, lambda i: (0,) * a.ndim, pipeline_mode=pl.Buffered(1))
    mspec = pl.BlockSpec((None, 1, d), mi)
    weights = [lw["s5_d"], lw["s5_w_glu"], lw["m2_norm_g"], lw["m2_w_out"], lw["lru_w_out"], lw["w_o"],
               lw["norm2_g"], lw["w_router"]]
    return pl.pallas_call(
        _final_kernel,
        out_shape=(jax.ShapeDtypeStruct((t, d), F32), jax.ShapeDtypeStruct((t, d), BF16),
                   jax.ShapeDtypeStruct((t, LANE), F32)),
        grid=(t // tm,),
        in_specs=[row(d), row(s5w), pcol(s5w, p_u), row(d), pcol(d, p_z), row(hy.shape[1]),
                  pcol(d, p_g), pcol(d, p_g + d), pcol(d, p_g + 2 * d), mspec, mspec, mspec]
                 + [full(w) for w in weights],
        out_specs=(row(d), row(d), row(LANE)),
        compiler_params=_cparams(("arbitrary",)),
        name="merge_out",
    )(x2d, y5, proj, yssd, proj, hy, proj, proj, proj, alpha, shift2, scale2, *weights)


def _topk_kernel(aff_ref, slot_ref, start_ref, *, cap):
    a = aff_ref[...]
    nblk, ne, _ = a.shape
    key = lax.bitcast_convert_type(a, I32)

    def count(m):
        return jnp.sum(jnp.sum(m.astype(I32), axis=0, keepdims=True), axis=2, keepdims=True)

    def body(i, lo):
        cand = lo | (jnp.int32(1) << (30 - i))
        return jnp.where(count(key >= cand) >= cap, cand, lo)

    kth = lax.fori_loop(0, 31, body, jnp.zeros((1, ne, 1), I32))
    gt = key > kth
    eq = key == kth
    need = cap - count(gt)
    rowi = lax.broadcasted_iota(I32, (LANE, LANE), 0)
    coli = lax.broadcasted_iota(I32, (LANE, LANE), 1)
    upper = (rowi <= coli).astype(BF16)

    def exclusive_rank(m):
        mf = m.astype(F32)
        incl = jnp.dot(mf.reshape(nblk * ne, LANE).astype(BF16), upper, preferred_element_type=F32).reshape(nblk, ne, LANE)
        offs, run = [], jnp.zeros((1, ne, 1), F32)
        for k in range(nblk):
            offs.append(run)
            run = run + incl[k:k + 1, :, LANE - 1:LANE]
        offs = jnp.concatenate(offs, axis=0)
        return (incl - mf + offs).astype(I32), offs.astype(I32)

    sel = gt | (eq & (exclusive_rank(eq)[0] < need))
    rank, offs = exclusive_rank(sel)
    slot_ref[...] = jnp.where(sel, rank, -1)
    start_ref[...] = jnp.broadcast_to(offs, start_ref.shape)


def _topk_slots(aff_t, cap):
    bs, nblk, ne, _ = aff_t.shape
    spec = pl.BlockSpec((None, nblk, ne, LANE), lambda b: (b, 0, 0, 0))
    return pl.pallas_call(
        functools.partial(_topk_kernel, cap=cap),
        out_shape=(jax.ShapeDtypeStruct(aff_t.shape, I32), jax.ShapeDtypeStruct(aff_t.shape, I32)),
        grid=(bs,),
        in_specs=[spec],
        out_specs=(spec, spec),
        compiler_params=_cparams(("arbitrary",)),
        name="route_topk",
    )(aff_t)


def _moe_ffn_kernel(start_ref, h_ref, slot_ref, w1_ref, w3_ref, w2_ref, y_ref, xs_ref, *, cap, tw, sb):
    e, bstep = pl.program_id(0), pl.program_id(1)
    bb, nblk = slot_ref.shape[0], slot_ref.shape[1]
    bpw = tw // LANE
    nw = nblk // bpw
    sidx = lax.broadcasted_iota(I32, (sb, LANE), 0)
    for bi in range(bb):
        base = ((bstep * bb + bi) * pl.num_programs(0) + e) * nblk
        begins = [start_ref[base + k * bpw] for k in range(nw)]
        ends = begins[1:] + [cap]
        for j in range(cap // sb):
            lo, hi = j * sb, (j + 1) * sb
            rows = slice(bi * cap + lo, bi * cap + hi)
            k_lo = sum(jnp.asarray(en <= lo, I32) for en in ends)
            k_hi = sum(jnp.asarray(bg < hi, I32) for bg in begins)
            xs_ref[rows, :] = jnp.zeros((sb, xs_ref.shape[1]), F32)

            def body(k, carry, lo=lo, rows=rows, bi=bi):
                t0 = pl.multiple_of(k * tw, tw)
                p = jnp.concatenate([(slot_ref[bi, pl.ds(k * bpw + i, 1), :] == sidx + lo).astype(BF16)
                                     for i in range(bpw)], axis=1)
                xs_ref[rows, :] += jnp.dot(p, h_ref[bi, pl.ds(t0, tw), :], preferred_element_type=F32)
                return carry

            lax.fori_loop(k_lo, k_hi, body, 0)
    xs = xs_ref[...].astype(BF16)
    hid = _silu(jnp.dot(xs, w1_ref[...], preferred_element_type=F32)) * jnp.dot(xs, w3_ref[...], preferred_element_type=F32)
    y = jnp.dot(hid.astype(BF16), w2_ref[...], preferred_element_type=F32).astype(BF16)
    for bi in range(bb):
        y_ref[bi] = y[bi * cap:(bi + 1) * cap]


def _moe_ffn(h2, slot_e, starts, w1, w3, w2, layer, cap):
    bs, n, d = h2.shape
    ne, nblk = slot_e.shape[1], slot_e.shape[2]
    tw = min(512, n)
    sb = min(LANE, cap)
    bb = max(1, min(bs, 512 // cap))
    while bs % bb:
        bb -= 1
    wspec = lambda w: pl.BlockSpec((None, None) + w.shape[2:], lambda e, b, s: (layer, e, 0, 0))
    return pl.pallas_call(
        functools.partial(_moe_ffn_kernel, cap=cap, tw=tw, sb=sb),
        out_shape=jax.ShapeDtypeStruct((bs, ne, cap, d), BF16),
        grid_spec=pltpu.PrefetchScalarGridSpec(
            num_scalar_prefetch=1,
            grid=(ne, bs // bb),
            in_specs=[pl.BlockSpec((bb, n, d), lambda e, b, s: (b, 0, 0)),
                      pl.BlockSpec((bb, None, nblk, LANE), lambda e, b, s: (b, e, 0, 0)),
                      wspec(w1), wspec(w3), wspec(w2)],
            out_specs=pl.BlockSpec((bb, None, cap, d), lambda e, b, s: (b, e, 0, 0)),
            scratch_shapes=[pltpu.VMEM((bb * cap, d), F32)]),
        compiler_params=_cparams(("arbitrary", "arbitrary")),
        name="moe_ffn",
    )(starts, h2, slot_e, w1, w3, w2)


def _combine_kernel(x_ref, y_ref, slot_ref, aff_ref, al_ref, fg_ref, o_ref, *, cap, final_norm):
    slot = slot_ref[...]
    aff = aff_ref[...]
    tq = slot.shape[0]
    sidx = lax.broadcasted_iota(I32, (tq, cap), 1)
    acc = jnp.zeros(x_ref.shape, F32)
    for e in range(N_EXPERTS):
        pt = (slot[:, e:e + 1] == sidx).astype(BF16)
        acc = acc + aff[:, e:e + 1] * jnp.dot(pt, y_ref[e], preferred_element_type=F32)
    x2 = x_ref[...] + al_ref[...] * acc
    if final_norm:
        x2 = _rms_scale(x2) * fg_ref[...]
    o_ref[...] = x2


def _combine(x2d, y, slot_t, aff, seq, alpha, final_g, final_norm):
    t, d = x2d.shape
    bs, ne, cap, _ = y.shape
    tq = min(512, seq)
    nq = seq // tq
    mi = (lambda i: (0, 0, 0)) if alpha.shape[0] == 1 else (lambda i: (i // nq, 0, 0))
    return pl.pallas_call(
        functools.partial(_combine_kernel, cap=cap, final_norm=final_norm),
        out_shape=jax.ShapeDtypeStruct((t, d), F32),
        grid=(t // tq,),
        in_specs=[pl.BlockSpec((tq, d), lambda i: (i, 0)),
                  pl.BlockSpec((None, ne, cap, d), lambda i: (i // nq, 0, 0, 0)),
                  pl.BlockSpec((tq, LANE), lambda i: (i, 0)),
                  pl.BlockSpec((tq, LANE), lambda i: (i, 0)),
                  pl.BlockSpec((None, 1, d), mi),
                  pl.BlockSpec((1, d), lambda i: (0, 0))],
        out_specs=pl.BlockSpec((tq, d), lambda i: (i, 0)),
        compiler_params=_cparams(("arbitrary",)),
        name="moe_combine",
    )(x2d, y, slot_t, aff, alpha, final_g.reshape(1, d))


def _moe(x1, h2, aff, bs, seq, alpha, moe_w, layer, final_g, final_norm):
    d = x1.shape[1]
    cap = CAPACITY * seq // N_EXPERTS
    nblk = seq // LANE
    aff_t = aff[:, :N_EXPERTS].reshape(bs, nblk, LANE, N_EXPERTS).transpose(0, 1, 3, 2)
    slot, start = _topk_slots(aff_t, cap)
    slot_e = slot.transpose(0, 2, 1, 3)
    slot_t = slot.transpose(0, 1, 3, 2).reshape(bs * seq, N_EXPERTS)
    slot_t = jnp.pad(slot_t, ((0, 0), (0, LANE - N_EXPERTS)), constant_values=-1)
    starts = start[:, :, :, 0].transpose(0, 2, 1).reshape(-1)
    y = _moe_ffn(h2.reshape(bs, seq, d), slot_e, starts, *moe_w, layer, cap)
    return _combine(x1, y, slot_t, aff, seq, alpha, final_g, final_norm)


def _layout(d):
    s5w, inner, lruw = d // 2, d, d // 2
    gn = M2_GROUPS * M2_STATE
    heads = inner // M2_HEAD_DIM
    o_u = 0
    o_z = o_u + s5w
    o_xbc = o_z + inner
    o_dt = o_xbc + inner + 2 * gn
    o_xl = o_dt + 2 * heads
    o_gl = o_xl + lruw
    o_g = o_gl + lruw
    d_in = o_g + N_BRANCH * d
    p_z = 0
    p_g = p_z + inner
    p_xm = p_g + N_BRANCH * d
    p_bc = p_xm + inner
    p_u = p_bc + M2_GROUPS * LANE
    p_xl = p_u + s5w
    p_gl = p_xl + lruw
    p_dt = p_gl + lruw
    npad = p_dt + M2_GROUPS * LANE
    hpg = heads // M2_GROUPS
    assert hpg == 4 and M2_STATE == 64 and M2_HEAD_DIM == 64, "SSD kernel packs two 64-wide heads per lane tile"
    perm = np.full((npad,), d_in, np.int32)
    perm[p_z:p_z + inner] = o_z + np.arange(inner)
    perm[p_g:p_g + N_BRANCH * d] = o_g + np.arange(N_BRANCH * d)
    perm[p_xm:p_xm + inner] = o_xbc + np.arange(inner)
    bcp = np.zeros((M2_GROUPS * LANE,), np.int32)
    for g in range(M2_GROUPS):
        bcp[g * LANE:g * LANE + M2_STATE] = inner + gn + g * M2_STATE + np.arange(M2_STATE)
        bcp[g * LANE + M2_STATE:(g + 1) * LANE] = inner + g * M2_STATE + np.arange(M2_STATE)
    perm[p_bc:p_bc + M2_GROUPS * LANE] = o_xbc + bcp
    perm[p_u:p_u + s5w] = o_u + np.arange(s5w)
    perm[p_xl:p_xl + lruw] = o_xl + np.arange(lruw)
    perm[p_gl:p_gl + lruw] = o_gl + np.arange(lruw)
    for g in range(M2_GROUPS):
        for dd in range(2):
            for j in range(hpg):
                perm[p_dt + g * LANE + dd * hpg + j] = o_dt + dd * heads + g * hpg + j
    return dict(s5w=s5w, inner=inner, lruw=lruw, heads=heads, hpg=hpg, perm=perm, bcp=bcp, npad=npad,
                p_z=p_z, p_g=p_g, p_xm=p_xm, p_bc=p_bc, p_u=p_u, p_xl=p_xl, p_gl=p_gl, p_dt=p_dt)


def _rows8(rows, width):
    out = jnp.zeros((SUBLANE, width), F32)
    for i, r in enumerate(rows):
        out = out.at[i, :r.shape[0]].set(r.astype(F32))
    return out


def _pack_layer(lay, p, q_lat, q_ctx):
    d = p["w_in"].shape[0]
    inner, lruw, hpg, heads = lay["inner"], lay["lruw"], lay["hpg"], lay["heads"]
    lw = {}
    w_ext = jnp.concatenate([p["w_in"], jnp.zeros((d, 1), p["w_in"].dtype)], axis=1)
    lw["w_in"] = w_ext[:, lay["perm"]].astype(BF16)
    lw["s5_lat"] = _s5_weights(p["s5_lam_re"], p["s5_lam_im"], p["s5_log_step"], p["s5_b_re"], p["s5_b_im"],
                               p["s5_c_re"], p["s5_c_im"], q_lat)
    lw["s5_ctx"] = lw["s5_lat"] if q_ctx == q_lat else _s5_weights(
        p["s5_lam_re"], p["s5_lam_im"], p["s5_log_step"], p["s5_b_re"], p["s5_b_im"], p["s5_c_re"], p["s5_c_im"], q_ctx)
    gw = inner // M2_GROUPS
    cw, cb = p["m2_conv_w"], p["m2_conv_b"]
    lw["m2_cwx"] = jnp.stack([_rows8([cw[k, g * gw:(g + 1) * gw] for k in range(CONV_W)] + [cb[g * gw:(g + 1) * gw]], gw)
                              for g in range(M2_GROUPS)])
    cwb, cbb = cw[:, lay["bcp"]], cb[lay["bcp"]]
    lw["m2_cwb"] = jnp.stack([_rows8([cwb[k, g * LANE:(g + 1) * LANE] for k in range(CONV_W)] + [cbb[g * LANE:(g + 1) * LANE]], LANE)
                              for g in range(M2_GROUPS)])
    dtb = p["m2_dt_bias"].reshape(2, M2_GROUPS, hpg)
    alog = p["m2_a_log"].reshape(2, M2_GROUPS, hpg)
    dsk = p["m2_d"].reshape(M2_GROUPS, hpg)
    lw["m2_pp"] = jnp.stack([_rows8([dtb[:, g].reshape(-1), alog[:, g].reshape(-1), dsk[g]], LANE) for g in range(M2_GROUPS)])
    blk = lruw // LRU_BLOCKS
    eye = jnp.eye(LRU_BLOCKS, dtype=F32)

    def dense(w):
        return jnp.einsum('hij,hk->hikj', w.astype(F32), eye).reshape(lruw, lruw)

    ncb = lruw // LANE
    assert LANE % blk == 0
    mats = [dense(p["lru_w_a"][0]), dense(p["lru_w_x"][0]), dense(p["lru_w_a"][1]), dense(p["lru_w_x"][1])]
    lw["lru_w"] = jnp.stack([jnp.concatenate([m[c * LANE:(c + 1) * LANE, c * LANE:(c + 1) * LANE] for m in mats], axis=1)
                             for c in range(ncb)]).astype(BF16)
    sl = lambda v, c: v[c * LANE:(c + 1) * LANE]
    pps = []
    for c in range(ncb):
        bias = jnp.concatenate([sl(p["lru_b_a"][0], c), sl(p["lru_b_x"][0], c), sl(p["lru_b_a"][1], c), sl(p["lru_b_x"][1], c)])
        lam = jnp.concatenate([sl(p["lru_lam"][0], c), sl(p["lru_lam"][1], c)])
        pps.append(_rows8([bias, lam, sl(p["lru_conv_b"], c)] + [sl(p["lru_conv_w"][k], c) for k in range(CONV_W)], 4 * LANE))
    lw["lru_pp"] = jnp.stack(pps)
    lw["s5_d"] = p["s5_d"].reshape(1, -1).astype(F32)
    lw["s5_w_glu"] = p["s5_w_glu"].astype(BF16)
    lw["m2_norm_g"] = p["m2_norm_g"].reshape(1, -1).astype(F32)
    lw["m2_w_out"] = p["m2_w_out"].astype(BF16)
    lw["lru_w_out"] = p["lru_w_out"].astype(BF16)
    lw["w_o"] = p["w_o"].astype(BF16)
    lw["norm2_g"] = p["norm2_g"].reshape(1, -1).astype(F32)
    lw["w_router"] = jnp.pad(p["moe_w_router"].astype(F32), ((0, 0), (0, LANE - N_EXPERTS)))
    return lw


def _mixer(x2d, bs, seq, colmajor, norm_g, shift, scale, lw, lay, h0, s5w):
    proj = _inproj(x2d, seq, norm_g, shift, scale, lw["w_in"])
    y5, f5 = _s5_branch(proj, bs, seq, colmajor, s5w, h0[0], (lay["p_u"], lay["s5w"]))
    ys, fm = _ssd_branch(proj, bs, seq, lw, h0[1], (lay["p_xm"], lay["p_bc"], lay["p_dt"], lay["inner"]))
    hy, fl = _lru_branch(proj, bs, seq, lw, h0[2], (lay["p_xl"], lay["p_gl"], lay["lruw"]))
    return proj, y5, ys, hy, (f5, fm, fl)


def kernel(x, c, ctx, c_ctx, w_mod, b_mod, norm1_g, norm2_g, w_in, s5_lam_re, s5_lam_im, s5_log_step, s5_b_re, s5_b_im, s5_c_re, s5_c_im, s5_d, s5_w_glu, m2_conv_w, m2_conv_b, m2_dt_bias, m2_a_log, m2_d, m2_norm_g, m2_w_out, lru_conv_w, lru_conv_b, lru_w_a, lru_b_a, lru_w_x, lru_b_x, lru_lam, lru_w_out, w_o, moe_w_router, moe_w1, moe_w3, moe_w2, final_norm_g):
    bsz, seq, d = x.shape
    cl = ctx.shape[1]
    depth = w_mod.shape[0]
    lay = _layout(d)
    stacked = dict(norm2_g=norm2_g, w_in=w_in, s5_lam_re=s5_lam_re, s5_lam_im=s5_lam_im, s5_log_step=s5_log_step,
                   s5_b_re=s5_b_re, s5_b_im=s5_b_im, s5_c_re=s5_c_re, s5_c_im=s5_c_im, s5_d=s5_d, s5_w_glu=s5_w_glu,
                   m2_conv_w=m2_conv_w, m2_conv_b=m2_conv_b, m2_dt_bias=m2_dt_bias, m2_a_log=m2_a_log, m2_d=m2_d,
                   m2_norm_g=m2_norm_g, m2_w_out=m2_w_out, lru_conv_w=lru_conv_w, lru_conv_b=lru_conv_b,
                   lru_w_a=lru_w_a, lru_b_a=lru_b_a, lru_w_x=lru_w_x, lru_b_x=lru_b_x, lru_lam=lru_lam,
                   lru_w_out=lru_w_out, w_o=w_o, moe_w_router=moe_w_router)
    moe_w = (moe_w1.astype(BF16), moe_w3.astype(BF16), moe_w2.astype(BF16))
    rm = -(-(bsz + 1) // SUBLANE) * SUBLANE
    c_rows = jnp.zeros((rm, d), F32).at[:bsz].set(c.astype(F32)).at[bsz].set(c_ctx.astype(F32))
    mods = _modulation(c_rows, w_mod.astype(F32), b_mod.astype(F32))

    groups = lay["s5w"] // S5_GROUP
    ncb = lay["lruw"] // LANE
    npair = lay["inner"] // M2_GROUPS // LANE
    zero_h0 = (jnp.zeros((groups, 2, bsz, LANE), F32),
               jnp.zeros((bsz, M2_GROUPS, 2, npair, M2_STATE, LANE), F32),
               jnp.zeros((bsz, ncb, 2, LANE), F32))

    xs = x.reshape(bsz * seq, d).astype(F32)
    cs = ctx.reshape(bsz * cl, d).astype(F32)
    q_lat, q_ctx = min(S5_CHUNK, seq), min(S5_CHUNK, cl)
    packed = jax.vmap(lambda p: _pack_layer(lay, p, q_lat, q_ctx))(stacked)
    for i in range(depth):
        lw = jax.tree_util.tree_map(lambda v: v[i], packed)
        mx = [mods[i, :bsz, k * d:(k + 1) * d].reshape(bsz, 1, d) for k in range(6)]
        mc = [mods[i, bsz:bsz + 1, k * d:(k + 1) * d].reshape(1, 1, d) for k in range(6)]
        fcols = (lay["p_z"], lay["p_g"], lay["p_u"], lay["s5w"])
        cproj, cy5, cys, chy, cstates = _mixer(cs, bsz, cl, False, norm1_g[i], mc[0], mc[1], lw, lay, zero_h0, lw["s5_ctx"])
        if i < depth - 1:
            c1, ch2, caff = _final(cs, cproj, cy5, cys, chy, cl, (mc[2], mc[3], mc[4]), lw, fcols)
            cs = _moe(c1, ch2, caff, bsz, cl, mc[5], moe_w, i, final_norm_g, False)
        xproj, y5, ys, hy, _ = _mixer(xs, bsz, seq, True, norm1_g[i], mx[0], mx[1], lw, lay, cstates, lw["s5_lat"])
        x1, h2, aff = _final(xs, xproj, y5, ys, hy, seq, (mx[2], mx[3], mx[4]), lw, fcols)
        xs = _moe(x1, h2, aff, bsz, seq, mx[5], moe_w, i, final_norm_g, i == depth - 1)
    return xs.reshape(bsz, seq, d).astype(x.dtype)
```

```python
import functools
import math

import numpy as np
import jax
import jax.numpy as jnp
from jax import lax
from jax.experimental import pallas as pl
from jax.experimental.pallas import tpu as pltpu

F32 = jnp.float32
BF16 = jnp.bfloat16
I32 = jnp.int32

GRID_W = 64
EPS = 1e-6
CONV_W = 4
S5_GROUP = 16
S5_STATE = 64
M2_HEAD_DIM = 64
M2_GROUPS = 4
M2_STATE = 64
M2_CHUNK = 128
LRU_BLOCKS = 8
LRU_C = 8.0
N_EXPERTS = 16
CAPACITY = 2
N_BRANCH = 3

LANE = 128
SUBLANE = 8
S5_CHUNK = 128
LRU_TILE = 64
VMEM_LIMIT = 56 * 1024 * 1024


def _cparams(sem):
    return pltpu.CompilerParams(dimension_semantics=sem, vmem_limit_bytes=VMEM_LIMIT)


def _dot(a, b):
    return jnp.dot(a.astype(BF16), b.astype(BF16), preferred_element_type=F32)


def _dot_nt(a, b):
    return lax.dot_general(a.astype(BF16), b.astype(BF16), (((1,), (1,)), ((), ())), preferred_element_type=F32)


def _split2(a):
    hi = a.astype(BF16)
    lo = (a - hi.astype(F32)).astype(BF16)
    return hi, lo


def _split3(a):
    hi = a.astype(BF16)
    r = a - hi.astype(F32)
    mid = r.astype(BF16)
    lo = (r - mid.astype(F32)).astype(BF16)
    return hi, mid, lo


def _dot3(a, b):
    ah, al = _split2(a)
    bh, bl = _split2(b)
    d = lambda x, y: jnp.dot(x, y, preferred_element_type=F32)
    return d(ah, bh) + (d(ah, bl) + d(al, bh))


def _dot_exact_lhs(m01, x):
    hi, mid, lo = _split3(x)
    d = lambda y: jnp.dot(m01, y, preferred_element_type=F32)
    return d(hi) + (d(mid) + d(lo))


def _silu(x):
    return x * jax.nn.sigmoid(x)


def _softplus(x):
    return jnp.maximum(x, 0.0) + jnp.log(1.0 + jnp.exp(-jnp.abs(x)))


def _rms_scale(x):
    return x * lax.rsqrt(jnp.mean(x * x, axis=-1, keepdims=True) + EPS)


def _mod_kernel(c_ref, w_ref, b_ref, o_ref):
    o_ref[...] = _dot3(_silu(c_ref[...]), w_ref[...]) + b_ref[...]


def _modulation(c_rows, w_mod, b_mod):
    depth, d, n6 = w_mod.shape
    rm = c_rows.shape[0]
    tn = min(1024, n6)
    return pl.pallas_call(
        _mod_kernel,
        out_shape=jax.ShapeDtypeStruct((depth, rm, n6), F32),
        grid=(depth, n6 // tn),
        in_specs=[pl.BlockSpec((rm, d), lambda l, j: (0, 0)),
                  pl.BlockSpec((None, d, tn), lambda l, j: (l, 0, j)),
                  pl.BlockSpec((None, 1, tn), lambda l, j: (l, 0, j))],
        out_specs=pl.BlockSpec((None, rm, tn), lambda l, j: (l, 0, j)),
        compiler_params=_cparams(("arbitrary", "arbitrary")),
        name="modulation",
    )(c_rows, w_mod, b_mod.reshape(depth, 1, n6))


def _inproj_kernel(x_ref, g_ref, sh_ref, sc_ref, w_ref, o_ref, *, nchunk):
    h = _rms_scale(x_ref[...]) * g_ref[...] * (1.0 + sc_ref[...]) + sh_ref[...]
    hb = h.astype(BF16)
    npad = o_ref.shape[1]
    for n0 in range(0, npad, nchunk):
        o_ref[:, n0:n0 + nchunk] = jnp.dot(hb, w_ref[:, n0:n0 + nchunk], preferred_element_type=F32)


def _mod_index(bm, tm, seq):
    if bm == 1:
        return lambda i: (0, 0, 0)
    return lambda i: ((i * tm) // seq, 0, 0)


def _inproj(x2d, seq, g, shift, scale, w):
    t, d = x2d.shape
    npad = w.shape[1]
    tm = min(256, seq)
    mi = _mod_index(shift.shape[0], tm, seq)
    return pl.pallas_call(
        functools.partial(_inproj_kernel, nchunk=512),
        out_shape=jax.ShapeDtypeStruct((t, npad), F32),
        grid=(t // tm,),
        in_specs=[pl.BlockSpec((tm, d), lambda i: (i, 0)),
                  pl.BlockSpec((1, d), lambda i: (0, 0)),
                  pl.BlockSpec((None, 1, d), mi),
                  pl.BlockSpec((None, 1, d), mi),
                  pl.BlockSpec((d, npad), lambda i: (0, 0), pipeline_mode=pl.Buffered(1))],
        out_specs=pl.BlockSpec((tm, npad), lambda i: (i, 0)),
        compiler_params=_cparams(("arbitrary",)),
        name="inproj",
    )(x2d, g.reshape(1, d), shift, scale, w)


def _build_toeplitz(cp_ref, cn_ref, t_ref, q):
    trow = lax.broadcasted_iota(I32, (q, q), 0)
    tcol = lax.broadcasted_iota(I32, (q, q), 1)
    causal = tcol >= trow

    def body(j, carry):
        cp = cp_ref[j]
        cn = cn_ref[j]
        r0 = pl.multiple_of(j * q, q)
        for i in range(S5_GROUP):
            a = pltpu.roll(jnp.broadcast_to(cp[i:i + 1, :], (q, q)), 0, 1, stride=1, stride_axis=0)
            b = pltpu.roll(jnp.broadcast_to(cn[i:i + 1, :], (q, q)), 0, 1, stride=1, stride_axis=0)
            t_ref[pl.ds(r0, q), i * q:(i + 1) * q] = jnp.where(causal, a, b).astype(BF16)
        return carry

    lax.fori_loop(0, S5_GROUP, body, 0)


def _s5_kernel(u_ref, cp_ref, cn_ref, wst_ref, wout_ref, dec_ref, h0_ref, y_ref, fin_ref, ef_ref, eb_ref, t_ref, *, nc, bs):
    q = u_ref.shape[2]
    _build_toeplitz(cp_ref, cn_ref, t_ref, q)
    a = jnp.concatenate([u_ref[j] for j in range(S5_GROUP)], axis=-1).astype(BF16)
    loc = jnp.dot(a, wst_ref[...], preferred_element_type=F32)
    dec = dec_ref[...]
    half = S5_STATE

    def step(cur, da, db, add):
        return da * cur + db * pltpu.roll(cur, half, 1) + add

    cur = h0_ref[0]
    for c in range(nc):
        ef_ref[c * bs:(c + 1) * bs, :] = cur
        cur = step(cur, dec[0:1], dec[1:2], loc[c * bs:(c + 1) * bs, 0:LANE])
    fin_ref[0] = cur
    cur = h0_ref[1]
    for c in reversed(range(nc)):
        eb_ref[c * bs:(c + 1) * bs, :] = cur
        cur = step(cur, dec[2:3], dec[3:4], loc[c * bs:(c + 1) * bs, LANE:2 * LANE])
    fin_ref[1] = cur
    e = jnp.concatenate([ef_ref[...], eb_ref[...]], axis=-1).astype(BF16)
    kq = S5_GROUP * q
    t_ref[kq:kq + 2 * LANE, :] = wout_ref[...]
    acc = jnp.dot(jnp.concatenate([a, e], axis=-1), t_ref[...], preferred_element_type=F32)
    for i in range(S5_GROUP):
        y_ref[i] = acc[:, i * q:(i + 1) * q]


def _s5_scan(ut, s5w, h0, nc, bs):
    cpos, cneg, wst, wout, dec = s5w
    groups = cpos.shape[0]
    r, q = ut.shape[1], ut.shape[2]
    assert q == LANE, "the Toeplitz builder rotates one 128-lane tile per block"
    kq = S5_GROUP * q
    lagspec = pl.BlockSpec((None, S5_GROUP, S5_GROUP, q), lambda g: (g, 0, 0, 0))
    return pl.pallas_call(
        functools.partial(_s5_kernel, nc=nc, bs=bs),
        out_shape=(jax.ShapeDtypeStruct(ut.shape, F32), jax.ShapeDtypeStruct((groups, 2, bs, LANE), F32)),
        grid=(groups,),
        in_specs=[pl.BlockSpec((S5_GROUP, r, q), lambda g: (g, 0, 0)),
                  lagspec, lagspec,
                  pl.BlockSpec((None, kq, 2 * LANE), lambda g: (g, 0, 0)),
                  pl.BlockSpec((None, 2 * LANE, kq), lambda g: (g, 0, 0)),
                  pl.BlockSpec((None, SUBLANE, LANE), lambda g: (g, 0, 0)),
                  pl.BlockSpec((None, 2, bs, LANE), lambda g: (g, 0, 0, 0))],
        out_specs=(pl.BlockSpec((S5_GROUP, r, q), lambda g: (g, 0, 0)),
                   pl.BlockSpec((None, 2, bs, LANE), lambda g: (g, 0, 0, 0))),
        scratch_shapes=[pltpu.VMEM((r, LANE), F32), pltpu.VMEM((r, LANE), F32), pltpu.VMEM((kq + 2 * LANE, kq), BF16)],
        compiler_params=_cparams(("arbitrary",)),
        name="s5_scan",
    )(ut, cpos, cneg, wst, wout, dec, h0)


def _s5_weights(lam_re, lam_im, log_step, b_re, b_im, c_re, c_im, q):
    hp = lax.Precision.HIGHEST
    g, p = lam_re.shape[1], lam_re.shape[2]
    ii = S5_GROUP
    tau = jnp.arange(q + 1, dtype=F32)[:, None, None]
    ks, wsts, wouts, decs = [], [], [], []
    for d in range(2):
        lr, li = lam_re[d].astype(F32), lam_im[d].astype(F32)
        step = jnp.exp(log_step[d].astype(F32))[:, None]
        ar, ai = lr * step, li * step
        mag = jnp.exp(tau * ar)
        pre, pim = mag * jnp.cos(tau * ai), mag * jnp.sin(tau * ai)
        bar_re, bar_im = pre[1], pim[1]
        den = lr * lr + li * li
        nr, ni = bar_re - 1.0, bar_im
        coef_re = (nr * lr + ni * li) / den
        coef_im = (ni * lr - nr * li) / den
        bre, bim = b_re[d].astype(F32), b_im[d].astype(F32)
        bb_re = coef_re[..., None] * bre - coef_im[..., None] * bim
        bb_im = coef_re[..., None] * bim + coef_im[..., None] * bre
        cre, cim = c_re[d].astype(F32), c_im[d].astype(F32)
        cp_re = cre[None] * pre[:, :, None, :] - cim[None] * pim[:, :, None, :]
        cp_im = cre[None] * pim[:, :, None, :] + cim[None] * pre[:, :, None, :]
        k = (jnp.einsum('tgip,gpj->tgij', cp_re[:q], bb_re, precision=hp)
             - jnp.einsum('tgip,gpj->tgij', cp_im[:q], bb_im, precision=hp))
        ks.append(k)
        pw_re = pre[q - 1::-1][:q] if d == 0 else pre[:q]
        pw_im = pim[q - 1::-1][:q] if d == 0 else pim[:q]
        w_re = pw_re[:, :, :, None] * bb_re[None] - pw_im[:, :, :, None] * bb_im[None]
        w_im = pw_re[:, :, :, None] * bb_im[None] + pw_im[:, :, :, None] * bb_re[None]
        w = jnp.concatenate([w_re, w_im], axis=2)
        wsts.append(jnp.transpose(w, (1, 3, 0, 2)).reshape(g, ii * q, 2 * p))
        if d == 0:
            o_re, o_im = cp_re[1:q + 1], cp_im[1:q + 1]
        else:
            o_re, o_im = cp_re[q:0:-1], cp_im[q:0:-1]
        o = jnp.concatenate([o_re, -o_im], axis=3)
        wouts.append(jnp.transpose(o, (1, 3, 2, 0)).reshape(g, 2 * p, ii * q))
        dr, di = pre[q], pim[q]
        decs.append(jnp.concatenate([dr, dr], axis=-1))
        decs.append(jnp.concatenate([-di, di], axis=-1))
    cpos = jnp.transpose(ks[0].at[0].add(ks[1][0]), (1, 3, 2, 0))
    kb_rev = ks[1][::-1]
    cneg = jnp.concatenate([jnp.zeros_like(kb_rev[:1]), kb_rev[:q - 1]], axis=0)
    cneg = jnp.transpose(cneg, (1, 3, 2, 0))
    wst = jnp.concatenate(wsts, axis=-1).astype(BF16)
    wout = jnp.concatenate(wouts, axis=1).astype(BF16)
    dec = jnp.stack(decs + decs, axis=1)
    return cpos, cneg, wst, wout, dec


def _s5_in_kernel(x_ref, o_ref, *, cpc, colmajor):
    q = o_ref.shape[3]
    for b in range(x_ref.shape[0]):
        for cc in range(o_ref.shape[1]):
            if colmajor:
                x = jnp.concatenate([x_ref[b, :, cc * cpc + w, :] for w in range(cpc)], axis=0)
            else:
                x = x_ref[b, cc * q:(cc + 1) * q, :]
            o_ref[:, cc, b, :] = x.T


def _s5_out_kernel(y_ref, o_ref, *, cpc, colmajor):
    q = y_ref.shape[3]
    for b in range(o_ref.shape[0]):
        for cc in range(y_ref.shape[1]):
            t = y_ref[:, cc, b, :].T
            if colmajor:
                rows = q // cpc
                for w in range(cpc):
                    o_ref[b, :, cc * cpc + w, :] = t[w * rows:(w + 1) * rows]
            else:
                o_ref[b, cc * q:(cc + 1) * q, :] = t


def _s5_layout(bs, seq, colmajor, q):
    bb = min(SUBLANE, bs)
    while bs % bb:
        bb -= 1
    nc = seq // q
    if not colmajor:
        return bb, nc, 1, 1, None
    rows = seq // GRID_W
    assert q % rows == 0, "a scan chunk must cover whole grid columns"
    cpc = q // rows
    wb = max(SUBLANE, cpc)
    return bb, wb // cpc, GRID_W // wb, cpc, (rows, wb)


def _s5_branch(proj, bs, seq, colmajor, s5w, h0, cols):
    p_u, width = cols
    q = min(S5_CHUNK, seq)
    nc = seq // q
    bb, nch, nj, cpc, cm = _s5_layout(bs, seq, colmajor, q)
    cb = p_u // width
    cmaj = jax.ShapeDtypeStruct((width, nc, bs, q), F32)
    cspec = pl.BlockSpec((width, nch, bb, q), lambda i, j: (0, j, i, 0))
    if colmajor:
        rows, wb = cm
        tok_in = pl.BlockSpec((bb, rows, wb, width), lambda i, j: (i, 0, j, cb))
        tok_out = pl.BlockSpec((bb, rows, wb, width), lambda i, j: (i, 0, j, 0))
        x_in = proj.reshape(bs, rows, GRID_W, proj.shape[1])
        tok_shape = jax.ShapeDtypeStruct((bs, rows, GRID_W, width), F32)
    else:
        tok_in = pl.BlockSpec((bb, seq, width), lambda i, j: (i, 0, cb))
        tok_out = pl.BlockSpec((bb, seq, width), lambda i, j: (i, 0, 0))
        x_in = proj.reshape(bs, seq, proj.shape[1])
        tok_shape = jax.ShapeDtypeStruct((bs, seq, width), F32)
    ut = pl.pallas_call(
        functools.partial(_s5_in_kernel, cpc=cpc, colmajor=colmajor), out_shape=cmaj, grid=(bs // bb, nj),
        in_specs=[tok_in], out_specs=cspec, compiler_params=_cparams(("arbitrary", "arbitrary")), name="s5_to_channel_major",
    )(x_in)
    yt, fin = _s5_scan(ut.reshape(width, nc * bs, q), s5w, h0, nc, bs)
    y = pl.pallas_call(
        functools.partial(_s5_out_kernel, cpc=cpc, colmajor=colmajor), out_shape=tok_shape, grid=(bs // bb, nj),
        in_specs=[cspec], out_specs=tok_out, compiler_params=_cparams(("arbitrary", "arbitrary")), name="s5_to_token_major",
    )(yt.reshape(width, nc, bs, q))
    return y.reshape(bs * seq, width), fin


def _conv_rows(pad_ref, base, n, taps, bias):
    w = pad_ref[pl.ds(base, n + 2 * SUBLANE), :]
    tot = n + 2 * SUBLANE
    xm2 = pltpu.roll(w, 2, 0)[SUBLANE:SUBLANE + n]
    xm1 = pltpu.roll(w, 1, 0)[SUBLANE:SUBLANE + n]
    x0 = w[SUBLANE:SUBLANE + n]
    xp1 = pltpu.roll(w, tot - 1, 0)[SUBLANE:SUBLANE + n]
    return taps[0:1] * xm2 + taps[1:2] * xm1 + taps[2:3] * x0 + taps[3:4] * xp1 + bias


def _fill_padded(pad_ref, x_ref, seq):
    zeros = jnp.zeros((SUBLANE, pad_ref.shape[1]), F32)
    pad_ref[0:SUBLANE, :] = zeros
    pad_ref[SUBLANE + seq:2 * SUBLANE + seq, :] = zeros
    pad_ref[SUBLANE:SUBLANE + seq, :] = x_ref[...]


def _scan_tile(a, v, h, reverse):
    s = a.shape[0]
    row = lax.broadcasted_iota(I32, a.shape, 0)
    k = 1
    while k < s:
        sh = s - k if reverse else k
        ok = (row < s - k) if reverse else (row >= k)
        a_sh = pltpu.roll(a, sh, 0)
        v_sh = pltpu.roll(v, sh, 0)
        v = v + a * jnp.where(ok, v_sh, 0.0)
        a = a * jnp.where(ok, a_sh, 1.0)
        k *= 2
    return v + a * h


def _lru_kernel(x_ref, g_ref, w_ref, pp_ref, h0_ref, o_ref, fin_ref, pad_ref, a0_ref, v0_ref, a1_ref, v1_ref, *, seq, ch):
    pp = pp_ref[...]
    bias = pp[0:1]
    lam = pp[1:2]
    cb = pp[2:3, 0:LANE]
    taps = pp[3:7, 0:LANE]
    _fill_padded(pad_ref, x_ref, seq)
    av = ((a0_ref, v0_ref), (a1_ref, v1_ref))

    def gates(i, carry):
        base = pl.multiple_of(i * ch, ch)
        xc = _conv_rows(pad_ref, base, ch, taps, cb)
        gt = _dot(xc, w_ref[...]) + bias
        for d in range(2):
            r = jax.nn.sigmoid(gt[:, 2 * d * LANE:(2 * d + 1) * LANE])
            ig = jax.nn.sigmoid(gt[:, (2 * d + 1) * LANE:(2 * d + 2) * LANE])
            log_a = -LRU_C * r * _softplus(-lam[:, d * LANE:(d + 1) * LANE])
            a = jnp.exp(log_a)
            av[d][0][pl.ds(base, ch), :] = a
            av[d][1][pl.ds(base, ch), :] = jnp.sqrt(jnp.maximum(1.0 - a * a, EPS)) * (ig * xc)
        return carry

    lax.fori_loop(0, seq // ch, gates, 0, unroll=2 if (seq // ch) % 2 == 0 else 1)
    tile = min(LRU_TILE, seq)
    nt = seq // tile

    def fwd(i, h):
        r0 = pl.multiple_of(i * tile, tile)
        hall = _scan_tile(a0_ref[pl.ds(r0, tile), :], v0_ref[pl.ds(r0, tile), :], h, False)
        o_ref[pl.ds(r0, tile), :] = hall
        return hall[tile - 1:tile, :]

    h0 = h0_ref[...]
    hf = lax.fori_loop(0, nt, fwd, h0[0:1])

    def bwd(i, h):
        r0 = pl.multiple_of((nt - 1 - i) * tile, tile)
        hall = _scan_tile(a1_ref[pl.ds(r0, tile), :], v1_ref[pl.ds(r0, tile), :], h, True)
        o_ref[pl.ds(r0, tile), :] = (o_ref[pl.ds(r0, tile), :] + hall) * jax.nn.gelu(g_ref[pl.ds(r0, tile), :])
        return hall[0:1, :]

    hb = lax.fori_loop(0, nt, bwd, h0[1:2])
    fin_ref[...] = jnp.concatenate([hf, hb], axis=0)


def _lru_branch(proj, bs, seq, lw, h0, cols):
    p_xl, p_gl, width = cols
    ncb = width // LANE
    bx, bg = p_xl // LANE, p_gl // LANE
    ch = min(256, seq)
    sc = pltpu.VMEM((seq, LANE), F32)
    hy, fin = pl.pallas_call(
        functools.partial(_lru_kernel, seq=seq, ch=ch),
        out_shape=(jax.ShapeDtypeStruct((bs * seq, width), F32), jax.ShapeDtypeStruct((bs, ncb, 2, LANE), F32)),
        grid=(bs, ncb),
        in_specs=[pl.BlockSpec((seq, LANE), lambda b, c: (b, bx + c)),
                  pl.BlockSpec((seq, LANE), lambda b, c: (b, bg + c)),
                  pl.BlockSpec((None, LANE, 4 * LANE), lambda b, c: (c, 0, 0)),
                  pl.BlockSpec((None, SUBLANE, 4 * LANE), lambda b, c: (c, 0, 0)),
                  pl.BlockSpec((None, None, 2, LANE), lambda b, c: (b, c, 0, 0))],
        out_specs=(pl.BlockSpec((seq, LANE), lambda b, c: (b, c)),
                   pl.BlockSpec((None, None, 2, LANE), lambda b, c: (b, c, 0, 0))),
        scratch_shapes=[pltpu.VMEM((seq + 2 * SUBLANE, LANE), F32), sc, sc, sc, sc],
        compiler_params=_cparams(("arbitrary", "arbitrary")),
        name="rglru",
    )(proj, proj, lw["lru_w"], lw["lru_pp"], h0)
    return hy, fin


def _ssd_kernel(xm_ref, bc_ref, dt_ref, cwx_ref, cwb_ref, pp_ref, h0_ref, y_ref, fin_ref,
                xpad_ref, bpad_ref, xa_ref, ba_ref, ex_ref, cb_ref, ext_ref, tot_ref,
                sf_ref, sb_ref, df_ref, db_ref, *, seq):
    cq = M2_CHUNK
    nc = seq // cq
    hd = M2_HEAD_DIM
    ns = M2_STATE
    npair = xm_ref.shape[1] // LANE
    hpg = 2 * npair
    pp = pp_ref[...]
    dt_bias, a_neg, dskip = pp[0:1], -jnp.exp(pp[1:2]), pp[2:3]
    cwx, cwb = cwx_ref[...], cwb_ref[...]
    _fill_padded(xpad_ref, xm_ref, seq)
    _fill_padded(bpad_ref, bc_ref, seq)

    rowi = lax.broadcasted_iota(I32, (cq, cq), 0)
    coli = lax.broadcasted_iota(I32, (cq, cq), 1)
    tri = (coli <= rowi).astype(BF16)
    lane = lax.broadcasted_iota(I32, (cq, LANE), 1)
    first_half = lane < hd
    first_half_s = lax.broadcasted_iota(I32, (ns, LANE), 1) < hd
    first_half1 = lax.broadcasted_iota(I32, (1, LANE), 1) < hd
    srefs, drefs = (sf_ref, sb_ref), (df_ref, db_ref)

    def conv_act(c):
        r0 = pl.multiple_of(c * cq, cq)
        xa_ref[pl.ds(r0, cq), :] = _silu(_conv_rows(xpad_ref, r0, cq, cwx[0:4], cwx[4:5]))
        ba_ref[pl.ds(r0, cq), :] = _silu(_conv_rows(bpad_ref, r0, cq, cwb[0:4], cwb[4:5]))

    def activations(c):
        r0 = pl.multiple_of(c * cq, cq)
        r8 = pl.multiple_of(c * SUBLANE, SUBLANE)
        dtv = _softplus(dt_ref[pl.ds(r0, cq), :] + dt_bias)
        la = dtv * a_neg
        cum = _dot_exact_lhs(tri, la)
        tot = cum[cq - 1:cq, :]
        ex = jnp.where(lane < hpg, cum, cum - la)
        ex_ref[pl.ds(r0, cq), :] = ex
        ldt = jnp.log(dtv.T[0:SUBLANE, :])
        ext = ex.T[0:SUBLANE, :]
        fwd_rows = lax.broadcasted_iota(I32, (SUBLANE, LANE), 0) < hpg
        ext_ref[pl.ds(r8, SUBLANE), :] = jnp.where(fwd_rows, ext - ldt, ext + ldt)
        tot_ref[pl.ds(r8, SUBLANE), :] = jnp.broadcast_to(tot, (SUBLANE, LANE))

    def local_states(c):
        r0 = pl.multiple_of(c * cq, cq)
        r8 = pl.multiple_of(c * SUBLANE, SUBLANE)
        bc = ba_ref[pl.ds(r0, cq), :]
        ext = ext_ref[pl.ds(r8, SUBLANE), :]
        tot = tot_ref[pl.ds(r8, 1), :]
        bt = bc.T[ns:2 * ns, :]
        cb_ref[pl.ds(r0, cq), :] = _dot(bc[:, 0:ns], bt)
        for d in range(2):
            for pr in range(npair):
                xa = xa_ref[pl.ds(r0, cq), pr * LANE:(pr + 1) * LANE]
                rs, ts = [], []
                for hh in range(2):
                    col = d * hpg + 2 * pr + hh
                    th = tot[:, col:col + 1]
                    erow = ext[col:col + 1, :]
                    din_dt = jnp.exp(th - erow) if d == 0 else jnp.exp(erow)
                    rs.append(_dot(bt * din_dt, xa))
                    ts.append(jnp.exp(th))
                srefs[d][c, pr] = jnp.where(first_half_s, rs[0], rs[1])
                r8p = pl.multiple_of((c * npair + pr) * SUBLANE, SUBLANE)
                drefs[d][pl.ds(r8p, SUBLANE), :] = jnp.broadcast_to(jnp.where(first_half1, ts[0], ts[1]), (SUBLANE, LANE))

    lax.fori_loop(0, nc, lambda i, carry: (conv_act(i), carry)[1], 0, unroll=2 if nc % 2 == 0 else 1)
    lax.fori_loop(0, nc, lambda i, carry: (activations(i), carry)[1], 0, unroll=8 if nc % 8 == 0 else 1)
    lax.fori_loop(0, nc, lambda i, carry: (local_states(i), carry)[1], 0, unroll=8 if nc % 8 == 0 else 1)

    def recur(d):
        def step(i, cur):
            c = i if d == 0 else nc - 1 - i
            out = []
            for pr in range(npair):
                r8p = pl.multiple_of((c * npair + pr) * SUBLANE, SUBLANE)
                loc = srefs[d][c, pr]
                srefs[d][c, pr] = cur[pr]
                out.append(drefs[d][pl.ds(r8p, 1), :] * cur[pr] + loc)
            return tuple(out)
        return lax.fori_loop(0, nc, step, tuple(h0_ref[d, pr] for pr in range(npair)))

    for d in range(2):
        fin = recur(d)
        for pr in range(npair):
            fin_ref[d, pr] = fin[pr]

    lower = coli <= rowi
    upper = coli >= rowi

    def output(c, carry):
        r0 = pl.multiple_of(c * cq, cq)
        r8 = pl.multiple_of(c * SUBLANE, SUBLANE)
        ex = ex_ref[pl.ds(r0, cq), :]
        ext = ext_ref[pl.ds(r8, SUBLANE), :]
        tot = tot_ref[pl.ds(r8, 1), :]
        cb = cb_ref[pl.ds(r0, cq), :]
        cm = ba_ref[pl.ds(r0, cq), 0:ns]
        for pr in range(npair):
            lanes = slice(pr * LANE, (pr + 1) * LANE)
            xa = xa_ref[pl.ds(r0, cq), lanes]
            so = _dot(cm, jnp.concatenate([sf_ref[c, pr], sb_ref[c, pr]], axis=1))
            dsk = jnp.where(first_half1, dskip[:, 2 * pr:2 * pr + 1], dskip[:, 2 * pr + 1:2 * pr + 2])
            ydiag, ecfs, ecbs = [], [], []
            for hh in range(2):
                cf, cbk = 2 * pr + hh, hpg + 2 * pr + hh
                ecf = jnp.broadcast_to(ex[:, cf:cf + 1], (cq, cq))
                ecb = jnp.broadcast_to(ex[:, cbk:cbk + 1], (cq, cq))
                lf = jnp.where(lower, jnp.exp(ecf - ext[cf:cf + 1, :]), 0.0)
                lb = jnp.where(upper, jnp.exp(ext[cbk:cbk + 1, :] - ecb), 0.0)
                ydiag.append(_dot(cb * (lf + lb), xa))
                ecfs.append(ecf)
                ecbs.append(tot[:, cbk:cbk + 1] - ecb)
            dout_f = jnp.exp(jnp.where(first_half, ecfs[0], ecfs[1]))
            dout_b = jnp.exp(jnp.where(first_half, ecbs[0], ecbs[1]))
            yoff = so[:, 0:LANE] * dout_f + so[:, LANE:2 * LANE] * dout_b
            y_ref[pl.ds(r0, cq), lanes] = jnp.where(first_half, ydiag[0], ydiag[1]) + yoff + dsk * xa
        return carry

    lax.fori_loop(0, nc, output, 0, unroll=4 if nc % 4 == 0 else 1)


def _ssd_branch(proj, bs, seq, lw, h0, cols):
    p_xm, p_bc, p_dt, inner = cols
    gw = inner // M2_GROUPS
    npair = gw // LANE
    nc = seq // M2_CHUNK
    bxm, bbc, bdt = p_xm // gw, p_bc // LANE, p_dt // LANE
    y, fin = pl.pallas_call(
        functools.partial(_ssd_kernel, seq=seq),
        out_shape=(jax.ShapeDtypeStruct((bs * seq, inner), F32),
                   jax.ShapeDtypeStruct((bs, M2_GROUPS, 2, npair, M2_STATE, LANE), F32)),
        grid=(bs, M2_GROUPS),
        in_specs=[pl.BlockSpec((seq, gw), lambda b, g: (b, bxm + g)),
                  pl.BlockSpec((seq, LANE), lambda b, g: (b, bbc + g)),
                  pl.BlockSpec((seq, LANE), lambda b, g: (b, bdt + g)),
                  pl.BlockSpec((None, SUBLANE, gw), lambda b, g: (g, 0, 0)),
                  pl.BlockSpec((None, SUBLANE, LANE), lambda b, g: (g, 0, 0)),
                  pl.BlockSpec((None, SUBLANE, LANE), lambda b, g: (g, 0, 0)),
                  pl.BlockSpec((None, None, 2, npair, M2_STATE, LANE), lambda b, g: (b, g, 0, 0, 0, 0))],
        out_specs=(pl.BlockSpec((seq, gw), lambda b, g: (b, g)),
                   pl.BlockSpec((None, None, 2, npair, M2_STATE, LANE), lambda b, g: (b, g, 0, 0, 0, 0))),
        scratch_shapes=[pltpu.VMEM((seq + 2 * SUBLANE, gw), F32), pltpu.VMEM((seq + 2 * SUBLANE, LANE), F32),
                        pltpu.VMEM((seq, gw), F32), pltpu.VMEM((seq, LANE), F32),
                        pltpu.VMEM((seq, LANE), F32), pltpu.VMEM((seq, LANE), F32),
                        pltpu.VMEM((nc * SUBLANE, LANE), F32), pltpu.VMEM((nc * SUBLANE, LANE), F32),
                        pltpu.VMEM((nc, npair, M2_STATE, LANE), F32), pltpu.VMEM((nc, npair, M2_STATE, LANE), F32),
                        pltpu.VMEM((nc * npair * SUBLANE, LANE), F32), pltpu.VMEM((nc * npair * SUBLANE, LANE), F32)],
        compiler_params=_cparams(("arbitrary", "arbitrary")),
        name="ssd",
    )(proj, proj, proj, lw["m2_cwx"], lw["m2_cwb"], lw["m2_pp"], h0)
    return y, fin


def _final_kernel(x_ref, y5_ref, u_ref, ys_ref, z_ref, hy_ref, g0_ref, g1_ref, g2_ref,
                  al_ref, sh_ref, sc_ref, d5_ref, wglu_ref, ng_ref, wm2_ref, wlru_ref, wo_ref, n2g_ref, wr_ref,
                  x1_ref, h2_ref, aff_ref):
    d = x_ref.shape[1]
    t5 = jax.nn.gelu(y5_ref[...] + d5_ref[...] * u_ref[...])
    vg = _dot(t5, wglu_ref[...])
    ya = vg[:, :d] * jax.nn.sigmoid(vg[:, d:])
    tb = _rms_scale(ys_ref[...] * _silu(z_ref[...])) * ng_ref[...]
    yb = _dot(tb, wm2_ref[...])
    yc = _dot(hy_ref[...], wlru_ref[...])
    merged = (jax.nn.sigmoid(g0_ref[...]) * ya + jax.nn.sigmoid(g1_ref[...]) * yb) + jax.nn.sigmoid(g2_ref[...]) * yc
    x1 = x_ref[...] + al_ref[...] * _dot(merged, wo_ref[...])
    x1_ref[...] = x1
    h2 = _rms_scale(x1) * n2g_ref[...] * (1.0 + sc_ref[...]) + sh_ref[...]
    h2_ref[...] = h2.astype(BF16)
    logits = _dot3(h2, wr_ref[...])
    valid = lax.broadcasted_iota(I32, logits.shape, 1) < N_EXPERTS
    logits = jnp.where(valid, logits, -jnp.inf)
    m = jnp.max(logits, axis=-1, keepdims=True)
    e = jnp.where(valid, jnp.exp(logits - m), 0.0)
    aff_ref[...] = e / jnp.sum(e, axis=-1, keepdims=True)


def _final(x2d, proj, y5, yssd, hy, seq, mods, lw, cols):
    t, d = x2d.shape
    tm = min(256, seq)
    alpha, shift2, scale2 = mods
    mi = _mod_index(alpha.shape[0], tm, seq)
    p_z, p_g, p_u, s5w = cols
    row = lambda w: pl.BlockSpec((tm, w), lambda i: (i, 0))
    pcol = lambda w, off: pl.BlockSpec((tm, w), lambda i: (i, off // w))
    full = lambda a: pl.BlockSpec(a.shape, lambda i: (0,) * a.ndim, pipeline_mode=pl.Buffered(1))
    mspec = pl.BlockSpec((None, 1, d), mi)
    weights = [lw["s5_d"], lw["s5_w_glu"], lw["m2_norm_g"], lw["m2_w_out"], lw["lru_w_out"], lw["w_o"],
               lw["norm2_g"], lw["w_router"]]
    return pl.pallas_call(
        _final_kernel,
        out_shape=(jax.ShapeDtypeStruct((t, d), F32), jax.ShapeDtypeStruct((t, d), BF16),
                   jax.ShapeDtypeStruct((t, LANE), F32)),
        grid=(t // tm,),
        in_specs=[row(d), row(s5w), pcol(s5w, p_u), row(d), pcol(d, p_z), row(hy.shape[1]),
                  pcol(d, p_g), pcol(d, p_g + d), pcol(d, p_g + 2 * d), mspec, mspec, mspec]
                 + [full(w) for w in weights],
        out_specs=(row(d), row(d), row(LANE)),
        compiler_params=_cparams(("arbitrary",)),
        name="merge_out",
    )(x2d, y5, proj, yssd, proj, hy, proj, proj, proj, alpha, shift2, scale2, *weights)


def _topk_kernel(aff_ref, slot_ref, start_ref, *, cap):
    a = aff_ref[...]
    nblk, ne, _ = a.shape
    key = lax.bitcast_convert_type(a, I32)

    def count(m):
        return jnp.sum(jnp.sum(m.astype(I32), axis=0, keepdims=True), axis=2, keepdims=True)

    def body(i, lo):
        cand = lo | (jnp.int32(1) << (30 - i))
        return jnp.where(count(key >= cand) >= cap, cand, lo)

    kth = lax.fori_loop(0, 31, body, jnp.zeros((1, ne, 1), I32))
    gt = key > kth
    eq = key == kth
    need = cap - count(gt)
    rowi = lax.broadcasted_iota(I32, (LANE, LANE), 0)
    coli = lax.broadcasted_iota(I32, (LANE, LANE), 1)
    upper = (rowi <= coli).astype(BF16)

    def exclusive_rank(m):
        mf = m.astype(F32)
        incl = jnp.dot(mf.reshape(nblk * ne, LANE).astype(BF16), upper, preferred_element_type=F32).reshape(nblk, ne, LANE)
        offs, run = [], jnp.zeros((1, ne, 1), F32)
        for k in range(nblk):
            offs.append(run)
            run = run + incl[k:k + 1, :, LANE - 1:LANE]
        offs = jnp.concatenate(offs, axis=0)
        return (incl - mf + offs).astype(I32), offs.astype(I32)

    sel = gt | (eq & (exclusive_rank(eq)[0] < need))
    rank, offs = exclusive_rank(sel)
    slot_ref[...] = jnp.where(sel, rank, -1)
    start_ref[...] = jnp.broadcast_to(offs, start_ref.shape)


def _topk_slots(aff_t, cap):
    bs, nblk, ne, _ = aff_t.shape
    spec = pl.BlockSpec((None, nblk, ne, LANE), lambda b: (b, 0, 0, 0))
    return pl.pallas_call(
        functools.partial(_topk_kernel, cap=cap),
        out_shape=(jax.ShapeDtypeStruct(aff_t.shape, I32), jax.ShapeDtypeStruct(aff_t.shape, I32)),
        grid=(bs,),
        in_specs=[spec],
        out_specs=(spec, spec),
        compiler_params=_cparams(("arbitrary",)),
        name="route_topk",
    )(aff_t)


def _moe_ffn_kernel(start_ref, h_ref, slot_ref, w1_ref, w3_ref, w2_ref, y_ref, xs_ref, *, cap, tw, sb):
    e, bstep = pl.program_id(0), pl.program_id(1)
    bb, nblk = slot_ref.shape[0], slot_ref.shape[1]
    bpw = tw // LANE
    nw = nblk // bpw
    sidx = lax.broadcasted_iota(I32, (sb, LANE), 0)
    for bi in range(bb):
        base = ((bstep * bb + bi) * pl.num_programs(0) + e) * nblk
        begins = [start_ref[base + k * bpw] for k in range(nw)]
        ends = begins[1:] + [cap]
        for j in range(cap // sb):
            lo, hi = j * sb, (j + 1) * sb
            rows = slice(bi * cap + lo, bi * cap + hi)
            k_lo = sum(jnp.asarray(en <= lo, I32) for en in ends)
            k_hi = sum(jnp.asarray(bg < hi, I32) for bg in begins)
            xs_ref[rows, :] = jnp.zeros((sb, xs_ref.shape[1]), F32)

            def body(k, carry, lo=lo, rows=rows, bi=bi):
                t0 = pl.multiple_of(k * tw, tw)
                p = jnp.concatenate([(slot_ref[bi, pl.ds(k * bpw + i, 1), :] == sidx + lo).astype(BF16)
                                     for i in range(bpw)], axis=1)
                xs_ref[rows, :] += jnp.dot(p, h_ref[bi, pl.ds(t0, tw), :], preferred_element_type=F32)
                return carry

            lax.fori_loop(k_lo, k_hi, body, 0)
    xs = xs_ref[...].astype(BF16)
    hid = _silu(jnp.dot(xs, w1_ref[...], preferred_element_type=F32)) * jnp.dot(xs, w3_ref[...], preferred_element_type=F32)
    y = jnp.dot(hid.astype(BF16), w2_ref[...], preferred_element_type=F32).astype(BF16)
    for bi in range(bb):
        y_ref[bi] = y[bi * cap:(bi + 1) * cap]


def _moe_ffn(h2, slot_e, starts, w1, w3, w2, layer, cap):
    bs, n, d = h2.shape
    ne, nblk = slot_e.shape[1], slot_e.shape[2]
    tw = min(512, n)
    sb = min(LANE, cap)
    bb = max(1, min(bs, 512 // cap))
    while bs % bb:
        bb -= 1
    wspec = lambda w: pl.BlockSpec((None, None) + w.shape[2:], lambda e, b, s: (layer, e, 0, 0))
    return pl.pallas_call(
        functools.partial(_moe_ffn_kernel, cap=cap, tw=tw, sb=sb),
        out_shape=jax.ShapeDtypeStruct((bs, ne, cap, d), BF16),
        grid_spec=pltpu.PrefetchScalarGridSpec(
            num_scalar_prefetch=1,
            grid=(ne, bs // bb),
            in_specs=[pl.BlockSpec((bb, n, d), lambda e, b, s: (b, 0, 0)),
                      pl.BlockSpec((bb, None, nblk, LANE), lambda e, b, s: (b, e, 0, 0)),
                      wspec(w1), wspec(w3), wspec(w2)],
            out_specs=pl.BlockSpec((bb, None, cap, d), lambda e, b, s: (b, e, 0, 0)),
            scratch_shapes=[pltpu.VMEM((bb * cap, d), F32)]),
        compiler_params=_cparams(("arbitrary", "arbitrary")),
        name="moe_ffn",
    )(starts, h2, slot_e, w1, w3, w2)


def _combine_kernel(x_ref, y_ref, slot_ref, aff_ref, al_ref, fg_ref, o_ref, *, cap, final_norm):
    slot = slot_ref[...]
    aff = aff_ref[...]
    tq = slot.shape[0]
    sidx = lax.broadcasted_iota(I32, (tq, cap), 1)
    acc = jnp.zeros(x_ref.shape, F32)
    for e in range(N_EXPERTS):
        pt = (slot[:, e:e + 1] == sidx).astype(BF16)
        acc = acc + aff[:, e:e + 1] * jnp.dot(pt, y_ref[e], preferred_element_type=F32)
    x2 = x_ref[...] + al_ref[...] * acc
    if final_norm:
        x2 = _rms_scale(x2) * fg_ref[...]
    o_ref[...] = x2


def _combine(x2d, y, slot_t, aff, seq, alpha, final_g, final_norm):
    t, d = x2d.shape
    bs, ne, cap, _ = y.shape
    tq = min(512, seq)
    nq = seq // tq
    mi = (lambda i: (0, 0, 0)) if alpha.shape[0] == 1 else (lambda i: (i // nq, 0, 0))
    return pl.pallas_call(
        functools.partial(_combine_kernel, cap=cap, final_norm=final_norm),
        out_shape=jax.ShapeDtypeStruct((t, d), F32),
        grid=(t // tq,),
        in_specs=[pl.BlockSpec((tq, d), lambda i: (i, 0)),
                  pl.BlockSpec((None, ne, cap, d), lambda i: (i // nq, 0, 0, 0)),
                  pl.BlockSpec((tq, LANE), lambda i: (i, 0)),
                  pl.BlockSpec((tq, LANE), lambda i: (i, 0)),
                  pl.BlockSpec((None, 1, d), mi),
                  pl.BlockSpec((1, d), lambda i: (0, 0))],
        out_specs=pl.BlockSpec((tq, d), lambda i: (i, 0)),
        compiler_params=_cparams(("arbitrary",)),
        name="moe_combine",
    )(x2d, y, slot_t, aff, alpha, final_g.reshape(1, d))


def _moe(x1, h2, aff, bs, seq, alpha, moe_w, layer, final_g, final_norm):
    d = x1.shape[1]
    cap = CAPACITY * seq // N_EXPERTS
    nblk = seq // LANE
    aff_t = aff[:, :N_EXPERTS].reshape(bs, nblk, LANE, N_EXPERTS).transpose(0, 1, 3, 2)
    slot, start = _topk_slots(aff_t, cap)
    slot_e = slot.transpose(0, 2, 1, 3)
    slot_t = slot.transpose(0, 1, 3, 2).reshape(bs * seq, N_EXPERTS)
    slot_t = jnp.pad(slot_t, ((0, 0), (0, LANE - N_EXPERTS)), constant_values=-1)
    starts = start[:, :, :, 0].transpose(0, 2, 1).reshape(-1)
    y = _moe_ffn(h2.reshape(bs, seq, d), slot_e, starts, *moe_w, layer, cap)
    return _combine(x1, y, slot_t, aff, seq, alpha, final_g, final_norm)


def _layout(d):
    s5w, inner, lruw = d // 2, d, d // 2
    gn = M2_GROUPS * M2_STATE
    heads = inner // M2_HEAD_DIM
    o_u = 0
    o_z = o_u + s5w
    o_xbc = o_z + inner
    o_dt = o_xbc + inner + 2 * gn
    o_xl = o_dt + 2 * heads
    o_gl = o_xl + lruw
    o_g = o_gl + lruw
    d_in = o_g + N_BRANCH * d
    p_z = 0
    p_g = p_z + inner
    p_xm = p_g + N_BRANCH * d
    p_bc = p_xm + inner
    p_u = p_bc + M2_GROUPS * LANE
    p_xl = p_u + s5w
    p_gl = p_xl + lruw
    p_dt = p_gl + lruw
    npad = p_dt + M2_GROUPS * LANE
    hpg = heads // M2_GROUPS
    assert hpg == 4 and M2_STATE == 64 and M2_HEAD_DIM == 64, "SSD kernel packs two 64-wide heads per lane tile"
    perm = np.full((npad,), d_in, np.int32)
    perm[p_z:p_z + inner] = o_z + np.arange(inner)
    perm[p_g:p_g + N_BRANCH * d] = o_g + np.arange(N_BRANCH * d)
    perm[p_xm:p_xm + inner] = o_xbc + np.arange(inner)
    bcp = np.zeros((M2_GROUPS * LANE,), np.int32)
    for g in range(M2_GROUPS):
        bcp[g * LANE:g * LANE + M2_STATE] = inner + gn + g * M2_STATE + np.arange(M2_STATE)
        bcp[g * LANE + M2_STATE:(g + 1) * LANE] = inner + g * M2_STATE + np.arange(M2_STATE)
    perm[p_bc:p_bc + M2_GROUPS * LANE] = o_xbc + bcp
    perm[p_u:p_u + s5w] = o_u + np.arange(s5w)
    perm[p_xl:p_xl + lruw] = o_xl + np.arange(lruw)
    perm[p_gl:p_gl + lruw] = o_gl + np.arange(lruw)
    for g in range(M2_GROUPS):
        for dd in range(2):
            for j in range(hpg):
                perm[p_dt + g * LANE + dd * hpg + j] = o_dt + dd * heads + g * hpg + j
    return dict(s5w=s5w, inner=inner, lruw=lruw, heads=heads, hpg=hpg, perm=perm, bcp=bcp, npad=npad,
                p_z=p_z, p_g=p_g, p_xm=p_xm, p_bc=p_bc, p_u=p_u, p_xl=p_xl, p_gl=p_gl, p_dt=p_dt)


def _rows8(rows, width):
    out = jnp.zeros((SUBLANE, width), F32)
    for i, r in enumerate(rows):
        out = out.at[i, :r.shape[0]].set(r.astype(F32))
    return out


def _pack_layer(lay, p, q_lat, q_ctx):
    d = p["w_in"].shape[0]
    inner, lruw, hpg, heads = lay["inner"], lay["lruw"], lay["hpg"], lay["heads"]
    lw = {}
    w_ext = jnp.concatenate([p["w_in"], jnp.zeros((d, 1), p["w_in"].dtype)], axis=1)
    lw["w_in"] = w_ext[:, lay["perm"]].astype(BF16)
    lw["s5_lat"] = _s5_weights(p["s5_lam_re"], p["s5_lam_im"], p["s5_log_step"], p["s5_b_re"], p["s5_b_im"],
                               p["s5_c_re"], p["s5_c_im"], q_lat)
    lw["s5_ctx"] = lw["s5_lat"] if q_ctx == q_lat else _s5_weights(
        p["s5_lam_re"], p["s5_lam_im"], p["s5_log_step"], p["s5_b_re"], p["s5_b_im"], p["s5_c_re"], p["s5_c_im"], q_ctx)
    gw = inner // M2_GROUPS
    cw, cb = p["m2_conv_w"], p["m2_conv_b"]
    lw["m2_cwx"] = jnp.stack([_rows8([cw[k, g * gw:(g + 1) * gw] for k in range(CONV_W)] + [cb[g * gw:(g + 1) * gw]], gw)
                              for g in range(M2_GROUPS)])
    cwb, cbb = cw[:, lay["bcp"]], cb[lay["bcp"]]
    lw["m2_cwb"] = jnp.stack([_rows8([cwb[k, g * LANE:(g + 1) * LANE] for k in range(CONV_W)] + [cbb[g * LANE:(g + 1) * LANE]], LANE)
                              for g in range(M2_GROUPS)])
    dtb = p["m2_dt_bias"].reshape(2, M2_GROUPS, hpg)
    alog = p["m2_a_log"].reshape(2, M2_GROUPS, hpg)
    dsk = p["m2_d"].reshape(M2_GROUPS, hpg)
    lw["m2_pp"] = jnp.stack([_rows8([dtb[:, g].reshape(-1), alog[:, g].reshape(-1), dsk[g]], LANE) for g in range(M2_GROUPS)])
    blk = lruw // LRU_BLOCKS
    eye = jnp.eye(LRU_BLOCKS, dtype=F32)

    def dense(w):
        return jnp.einsum('hij,hk->hikj', w.astype(F32), eye).reshape(lruw, lruw)

    ncb = lruw // LANE
    assert LANE % blk == 0
    mats = [dense(p["lru_w_a"][0]), dense(p["lru_w_x"][0]), dense(p["lru_w_a"][1]), dense(p["lru_w_x"][1])]
    lw["lru_w"] = jnp.stack([jnp.concatenate([m[c * LANE:(c + 1) * LANE, c * LANE:(c + 1) * LANE] for m in mats], axis=1)
                             for c in range(ncb)]).astype(BF16)
    sl = lambda v, c: v[c * LANE:(c + 1) * LANE]
    pps = []
    for c in range(ncb):
        bias = jnp.concatenate([sl(p["lru_b_a"][0], c), sl(p["lru_b_x"][0], c), sl(p["lru_b_a"][1], c), sl(p["lru_b_x"][1], c)])
        lam = jnp.concatenate([sl(p["lru_lam"][0], c), sl(p["lru_lam"][1], c)])
        pps.append(_rows8([bias, lam, sl(p["lru_conv_b"], c)] + [sl(p["lru_conv_w"][k], c) for k in range(CONV_W)], 4 * LANE))
    lw["lru_pp"] = jnp.stack(pps)
    lw["s5_d"] = p["s5_d"].reshape(1, -1).astype(F32)
    lw["s5_w_glu"] = p["s5_w_glu"].astype(BF16)
    lw["m2_norm_g"] = p["m2_norm_g"].reshape(1, -1).astype(F32)
    lw["m2_w_out"] = p["m2_w_out"].astype(BF16)
    lw["lru_w_out"] = p["lru_w_out"].astype(BF16)
    lw["w_o"] = p["w_o"].astype(BF16)
    lw["norm2_g"] = p["norm2_g"].reshape(1, -1).astype(F32)
    lw["w_router"] = jnp.pad(p["moe_w_router"].astype(F32), ((0, 0), (0, LANE - N_EXPERTS)))
    return lw


def _mixer(x2d, bs, seq, colmajor, norm_g, shift, scale, lw, lay, h0, s5w):
    proj = _inproj(x2d, seq, norm_g, shift, scale, lw["w_in"])
    y5, f5 = _s5_branch(proj, bs, seq, colmajor, s5w, h0[0], (lay["p_u"], lay["s5w"]))
    ys, fm = _ssd_branch(proj, bs, seq, lw, h0[1], (lay["p_xm"], lay["p_bc"], lay["p_dt"], lay["inner"]))
    hy, fl = _lru_branch(proj, bs, seq, lw, h0[2], (lay["p_xl"], lay["p_gl"], lay["lruw"]))
    return proj, y5, ys, hy, (f5, fm, fl)


def kernel(x, c, ctx, c_ctx, w_mod, b_mod, norm1_g, norm2_g, w_in, s5_lam_re, s5_lam_im, s5_log_step, s5_b_re, s5_b_im, s5_c_re, s5_c_im, s5_d, s5_w_glu, m2_conv_w, m2_conv_b, m2_dt_bias, m2_a_log, m2_d, m2_norm_g, m2_w_out, lru_conv_w, lru_conv_b, lru_w_a, lru_b_a, lru_w_x, lru_b_x, lru_lam, lru_w_out, w_o, moe_w_router, moe_w1, moe_w3, moe_w2, final_norm_g):
    bsz, seq, d = x.shape
    cl = ctx.shape[1]
    depth = w_mod.shape[0]
    lay = _layout(d)
    stacked = dict(norm2_g=norm2_g, w_in=w_in, s5_lam_re=s5_lam_re, s5_lam_im=s5_lam_im, s5_log_step=s5_log_step,
                   s5_b_re=s5_b_re, s5_b_im=s5_b_im, s5_c_re=s5_c_re, s5_c_im=s5_c_im, s5_d=s5_d, s5_w_glu=s5_w_glu,
                   m2_conv_w=m2_conv_w, m2_conv_b=m2_conv_b, m2_dt_bias=m2_dt_bias, m2_a_log=m2_a_log, m2_d=m2_d,
                   m2_norm_g=m2_norm_g, m2_w_out=m2_w_out, lru_conv_w=lru_conv_w, lru_conv_b=lru_conv_b,
                   lru_w_a=lru_w_a, lru_b_a=lru_b_a, lru_w_x=lru_w_x, lru_b_x=lru_b_x, lru_lam=lru_lam,
                   lru_w_out=lru_w_out, w_o=w_o, moe_w_router=moe_w_router)
    moe_w = (moe_w1.astype(BF16), moe_w3.astype(BF16), moe_w2.astype(BF16))
    rm = -(-(bsz + 1) // SUBLANE) * SUBLANE
    c_rows = jnp.zeros((rm, d), F32).at[:bsz].set(c.astype(F32)).at[bsz].set(c_ctx.astype(F32))
    mods = _modulation(c_rows, w_mod.astype(F32), b_mod.astype(F32))

    groups = lay["s5w"] // S5_GROUP
    ncb = lay["lruw"] // LANE
    npair = lay["inner"] // M2_GROUPS // LANE
    zero_h0 = (jnp.zeros((groups, 2, bsz, LANE), F32),
               jnp.zeros((bsz, M2_GROUPS, 2, npair, M2_STATE, LANE), F32),
               jnp.zeros((bsz, ncb, 2, LANE), F32))

    xs = x.reshape(bsz * seq, d).astype(F32)
    cs = ctx.reshape(bsz * cl, d).astype(F32)
    q_lat, q_ctx = min(S5_CHUNK, seq), min(S5_CHUNK, cl)
    packed = jax.vmap(lambda p: _pack_layer(lay, p, q_lat, q_ctx))(stacked)
    for i in range(depth):
        lw = jax.tree_util.tree_map(lambda v: v[i], packed)
        mx = [mods[i, :bsz, k * d:(k + 1) * d].reshape(bsz, 1, d) for k in range(6)]
        mc = [mods[i, bsz:bsz + 1, k * d:(k + 1) * d].reshape(1, 1, d) for k in range(6)]
        fcols = (lay["p_z"], lay["p_g"], lay["p_u"], lay["s5w"])
        cproj, cy5, cys, chy, cstates = _mixer(cs, bsz, cl, False, norm1_g[i], mc[0], mc[1], lw, lay, zero_h0, lw["s5_ctx"])
        if i < depth - 1:
            c1, ch2, caff = _final(cs, cproj, cy5, cys, chy, cl, (mc[2], mc[3], mc[4]), lw, fcols)
            cs = _moe(c1, ch2, caff, bsz, cl, mc[5], moe_w, i, final_norm_g, False)
        xproj, y5, ys, hy, _ = _mixer(xs, bsz, seq, True, norm1_g[i], mx[0], mx[1], lw, lay, cstates, lw["s5_lat"])
        x1, h2, aff = _final(xs, xproj, y5, ys, hy, seq, (mx[2], mx[3], mx[4]), lw, fcols)
        xs = _moe(x1, h2, aff, bsz, seq, mx[5], moe_w, i, final_norm_g, i == depth - 1)
    return xs.reshape(bsz, seq, d).astype(x.dtype)
```

```python
import functools
import math

import numpy as np
import jax
import jax.numpy as jnp
from jax import lax
from jax.experimental import pallas as pl
from jax.experimental.pallas import tpu as pltpu

F32 = jnp.float32
BF16 = jnp.bfloat16
I32 = jnp.int32

GRID_W = 64
EPS = 1e-6
CONV_W = 4
S5_GROUP = 16
S5_STATE = 64
M2_HEAD_DIM = 64
M2_GROUPS = 4
M2_STATE = 64
M2_CHUNK = 128
LRU_BLOCKS = 8
LRU_C = 8.0
N_EXPERTS = 16
CAPACITY = 2
N_BRANCH = 3

LANE = 128
SUBLANE = 8
S5_CHUNK = 128
LRU_TILE = 64
VMEM_LIMIT = 56 * 1024 * 1024


def _cparams(sem):
    return pltpu.CompilerParams(dimension_semantics=sem, vmem_limit_bytes=VMEM_LIMIT)


def _dot(a, b):
    return jnp.dot(a.astype(BF16), b.astype(BF16), preferred_element_type=F32)


def _dot_nt(a, b):
    return lax.dot_general(a.astype(BF16), b.astype(BF16), (((1,), (1,)), ((), ())), preferred_element_type=F32)


def _split2(a):
    hi = a.astype(BF16)
    lo = (a - hi.astype(F32)).astype(BF16)
    return hi, lo


def _split3(a):
    hi = a.astype(BF16)
    r = a - hi.astype(F32)
    mid = r.astype(BF16)
    lo = (r - mid.astype(F32)).astype(BF16)
    return hi, mid, lo


def _dot3(a, b):
    ah, al = _split2(a)
    bh, bl = _split2(b)
    d = lambda x, y: jnp.dot(x, y, preferred_element_type=F32)
    return d(ah, bh) + (d(ah, bl) + d(al, bh))


def _dot_exact_lhs(m01, x):
    hi, mid, lo = _split3(x)
    d = lambda y: jnp.dot(m01, y, preferred_element_type=F32)
    return d(hi) + (d(mid) + d(lo))


def _silu(x):
    return x * jax.nn.sigmoid(x)


def _softplus(x):
    return jnp.maximum(x, 0.0) + jnp.log(1.0 + jnp.exp(-jnp.abs(x)))


def _rms_scale(x):
    return x * lax.rsqrt(jnp.mean(x * x, axis=-1, keepdims=True) + EPS)


def _mod_kernel(c_ref, w_ref, b_ref, o_ref):
    o_ref[...] = _dot3(_silu(c_ref[...]), w_ref[...]) + b_ref[...]


def _modulation(c_rows, w_mod, b_mod):
    depth, d, n6 = w_mod.shape
    rm = c_rows.shape[0]
    tn = min(1024, n6)
    return pl.pallas_call(
        _mod_kernel,
        out_shape=jax.ShapeDtypeStruct((depth, rm, n6), F32),
        grid=(depth, n6 // tn),
        in_specs=[pl.BlockSpec((rm, d), lambda l, j: (0, 0)),
                  pl.BlockSpec((None, d, tn), lambda l, j: (l, 0, j)),
                  pl.BlockSpec((None, 1, tn), lambda l, j: (l, 0, j))],
        out_specs=pl.BlockSpec((None, rm, tn), lambda l, j: (l, 0, j)),
        compiler_params=_cparams(("arbitrary", "arbitrary")),
        name="modulation",
    )(c_rows, w_mod, b_mod.reshape(depth, 1, n6))


def _inproj_kernel(x_ref, g_ref, sh_ref, sc_ref, w_ref, o_ref, *, nchunk):
    h = _rms_scale(x_ref[...]) * g_ref[...] * (1.0 + sc_ref[...]) + sh_ref[...]
    hb = h.astype(BF16)
    npad = o_ref.shape[1]
    for n0 in range(0, npad, nchunk):
        o_ref[:, n0:n0 + nchunk] = jnp.dot(hb, w_ref[:, n0:n0 + nchunk], preferred_element_type=F32)


def _mod_index(bm, tm, seq):
    if bm == 1:
        return lambda i: (0, 0, 0)
    return lambda i: ((i * tm) // seq, 0, 0)


def _inproj(x2d, seq, g, shift, scale, w):
    t, d = x2d.shape
    npad = w.shape[1]
    tm = min(256, seq)
    mi = _mod_index(shift.shape[0], tm, seq)
    return pl.pallas_call(
        functools.partial(_inproj_kernel, nchunk=512),
        out_shape=jax.ShapeDtypeStruct((t, npad), F32),
        grid=(t // tm,),
        in_specs=[pl.BlockSpec((tm, d), lambda i: (i, 0)),
                  pl.BlockSpec((1, d), lambda i: (0, 0)),
                  pl.BlockSpec((None, 1, d), mi),
                  pl.BlockSpec((None, 1, d), mi),
                  pl.BlockSpec((d, npad), lambda i: (0, 0), pipeline_mode=pl.Buffered(1))],
        out_specs=pl.BlockSpec((tm, npad), lambda i: (i, 0)),
        compiler_params=_cparams(("arbitrary",)),
        name="inproj",
    )(x2d, g.reshape(1, d), shift, scale, w)


def _build_toeplitz(cp_ref, cn_ref, t_ref, q):
    trow = lax.broadcasted_iota(I32, (q, q), 0)
    tcol = lax.broadcasted_iota(I32, (q, q), 1)
    causal = tcol >= trow

    def body(j, carry):
        cp = cp_ref[j]
        cn = cn_ref[j]
        r0 = pl.multiple_of(j * q, q)
        for i in range(S5_GROUP):
            a = pltpu.roll(jnp.broadcast_to(cp[i:i + 1, :], (q, q)), 0, 1, stride=1, stride_axis=0)
            b = pltpu.roll(jnp.broadcast_to(cn[i:i + 1, :], (q, q)), 0, 1, stride=1, stride_axis=0)
            t_ref[pl.ds(r0, q), i * q:(i + 1) * q] = jnp.where(causal, a, b).astype(BF16)
        return carry

    lax.fori_loop(0, S5_GROUP, body, 0)


def _s5_kernel(u_ref, cp_ref, cn_ref, wst_ref, wout_ref, dec_ref, h0_ref, y_ref, fin_ref, ef_ref, eb_ref, t_ref, *, nc, bs):
    q = u_ref.shape[2]
    _build_toeplitz(cp_ref, cn_ref, t_ref, q)
    a = jnp.concatenate([u_ref[j] for j in range(S5_GROUP)], axis=-1).astype(BF16)
    loc = jnp.dot(a, wst_ref[...], preferred_element_type=F32)
    dec = dec_ref[...]
    half = S5_STATE

    def step(cur, da, db, add):
        return da * cur + db * pltpu.roll(cur, half, 1) + add

    cur = h0_ref[0]
    for c in range(nc):
        ef_ref[c * bs:(c + 1) * bs, :] = cur
        cur = step(cur, dec[0:1], dec[1:2], loc[c * bs:(c + 1) * bs, 0:LANE])
    fin_ref[0] = cur
    cur = h0_ref[1]
    for c in reversed(range(nc)):
        eb_ref[c * bs:(c + 1) * bs, :] = cur
        cur = step(cur, dec[2:3], dec[3:4], loc[c * bs:(c + 1) * bs, LANE:2 * LANE])
    fin_ref[1] = cur
    e = jnp.concatenate([ef_ref[...], eb_ref[...]], axis=-1).astype(BF16)
    kq = S5_GROUP * q
    t_ref[kq:kq + 2 * LANE, :] = wout_ref[...]
    acc = jnp.dot(jnp.concatenate([a, e], axis=-1), t_ref[...], preferred_element_type=F32)
    for i in range(S5_GROUP):
        y_ref[i] = acc[:, i * q:(i + 1) * q]


def _s5_scan(ut, s5w, h0, nc, bs):
    cpos, cneg, wst, wout, dec = s5w
    groups = cpos.shape[0]
    r, q = ut.shape[1], ut.shape[2]
    assert q == LANE, "the Toeplitz builder rotates one 128-lane tile per block"
    kq = S5_GROUP * q
    lagspec = pl.BlockSpec((None, S5_GROUP, S5_GROUP, q), lambda g: (g, 0, 0, 0))
    return pl.pallas_call(
        functools.partial(_s5_kernel, nc=nc, bs=bs),
        out_shape=(jax.ShapeDtypeStruct(ut.shape, F32), jax.ShapeDtypeStruct((groups, 2, bs, LANE), F32)),
        grid=(groups,),
        in_specs=[pl.BlockSpec((S5_GROUP, r, q), lambda g: (g, 0, 0)),
                  lagspec, lagspec,
                  pl.BlockSpec((None, kq, 2 * LANE), lambda g: (g, 0, 0)),
                  pl.BlockSpec((None, 2 * LANE, kq), lambda g: (g, 0, 0)),
                  pl.BlockSpec((None, SUBLANE, LANE), lambda g: (g, 0, 0)),
                  pl.BlockSpec((None, 2, bs, LANE), lambda g: (g, 0, 0, 0))],
        out_specs=(pl.BlockSpec((S5_GROUP, r, q), lambda g: (g, 0, 0)),
                   pl.BlockSpec((None, 2, bs, LANE), lambda g: (g, 0, 0, 0))),
        scratch_shapes=[pltpu.VMEM((r, LANE), F32), pltpu.VMEM((r, LANE), F32), pltpu.VMEM((kq + 2 * LANE, kq), BF16)],
        compiler_params=_cparams(("arbitrary",)),
        name="s5_scan",
    )(ut, cpos, cneg, wst, wout, dec, h0)


def _s5_weights(lam_re, lam_im, log_step, b_re, b_im, c_re, c_im, q):
    hp = lax.Precision.HIGHEST
    g, p = lam_re.shape[1], lam_re.shape[2]
    ii = S5_GROUP
    tau = jnp.arange(q + 1, dtype=F32)[:, None, None]
    ks, wsts, wouts, decs = [], [], [], []
    for d in range(2):
        lr, li = lam_re[d].astype(F32), lam_im[d].astype(F32)
        step = jnp.exp(log_step[d].astype(F32))[:, None]
        ar, ai = lr * step, li * step
        mag = jnp.exp(tau * ar)
        pre, pim = mag * jnp.cos(tau * ai), mag * jnp.sin(tau * ai)
        bar_re, bar_im = pre[1], pim[1]
        den = lr * lr + li * li
        nr, ni = bar_re - 1.0, bar_im
        coef_re = (nr * lr + ni * li) / den
        coef_im = (ni * lr - nr * li) / den
        bre, bim = b_re[d].astype(F32), b_im[d].astype(F32)
        bb_re = coef_re[..., None] * bre - coef_im[..., None] * bim
        bb_im = coef_re[..., None] * bim + coef_im[..., None] * bre
        cre, cim = c_re[d].astype(F32), c_im[d].astype(F32)
        cp_re = cre[None] * pre[:, :, None, :] - cim[None] * pim[:, :, None, :]
        cp_im = cre[None] * pim[:, :, None, :] + cim[None] * pre[:, :, None, :]
        k = (jnp.einsum('tgip,gpj->tgij', cp_re[:q], bb_re, precision=hp)
             - jnp.einsum('tgip,gpj->tgij', cp_im[:q], bb_im, precision=hp))
        ks.append(k)
        pw_re = pre[q - 1::-1][:q] if d == 0 else pre[:q]
        pw_im = pim[q - 1::-1][:q] if d == 0 else pim[:q]
        w_re = pw_re[:, :, :, None] * bb_re[None] - pw_im[:, :, :, None] * bb_im[None]
        w_im = pw_re[:, :, :, None] * bb_im[None] + pw_im[:, :, :, None] * bb_re[None]
        w = jnp.concatenate([w_re, w_im], axis=2)
        wsts.append(jnp.transpose(w, (1, 3, 0, 2)).reshape(g, ii * q, 2 * p))
        if d == 0:
            o_re, o_im = cp_re[1:q + 1], cp_im[1:q + 1]
        else:
            o_re, o_im = cp_re[q:0:-1], cp_im[q:0:-1]
        o = jnp.concatenate([o_re, -o_im], axis=3)
        wouts.append(jnp.transpose(o, (1, 3, 2, 0)).reshape(g, 2 * p, ii * q))
        dr, di = pre[q], pim[q]
        decs.append(jnp.concatenate([dr, dr], axis=-1))
        decs.append(jnp.concatenate([-di, di], axis=-1))
    cpos = jnp.transpose(ks[0].at[0].add(ks[1][0]), (1, 3, 2, 0))
    kb_rev = ks[1][::-1]
    cneg = jnp.concatenate([jnp.zeros_like(kb_rev[:1]), kb_rev[:q - 1]], axis=0)
    cneg = jnp.transpose(cneg, (1, 3, 2, 0))
    wst = jnp.concatenate(wsts, axis=-1).astype(BF16)
    wout = jnp.concatenate(wouts, axis=1).astype(BF16)
    dec = jnp.stack(decs + decs, axis=1)
    return cpos, cneg, wst, wout, dec


def _s5_in_kernel(x_ref, o_ref, *, cpc, colmajor):
    q = o_ref.shape[3]
    for b in range(x_ref.shape[0]):
        for cc in range(o_ref.shape[1]):
            if colmajor:
                x = jnp.concatenate([x_ref[b, :, cc * cpc + w, :] for w in range(cpc)], axis=0)
            else:
                x = x_ref[b, cc * q:(cc + 1) * q, :]
            o_ref[:, cc, b, :] = x.T


def _s5_out_kernel(y_ref, o_ref, *, cpc, colmajor):
    q = y_ref.shape[3]
    for b in range(o_ref.shape[0]):
        for cc in range(y_ref.shape[1]):
            t = y_ref[:, cc, b, :].T
            if colmajor:
                rows = q // cpc
                for w in range(cpc):
                    o_ref[b, :, cc * cpc + w, :] = t[w * rows:(w + 1) * rows]
            else:
                o_ref[b, cc * q:(cc + 1) * q, :] = t


def _s5_layout(bs, seq, colmajor, q):
    bb = min(SUBLANE, bs)
    while bs % bb:
        bb -= 1
    nc = seq // q
    if not colmajor:
        return bb, nc, 1, 1, None
    rows = seq // GRID_W
    assert q % rows == 0, "a scan chunk must cover whole grid columns"
    cpc = q // rows
    wb = max(SUBLANE, cpc)
    return bb, wb // cpc, GRID_W // wb, cpc, (rows, wb)


def _s5_branch(proj, bs, seq, colmajor, s5w, h0, cols):
    p_u, width = cols
    q = min(S5_CHUNK, seq)
    nc = seq // q
    bb, nch, nj, cpc, cm = _s5_layout(bs, seq, colmajor, q)
    cb = p_u // width
    cmaj = jax.ShapeDtypeStruct((width, nc, bs, q), F32)
    cspec = pl.BlockSpec((width, nch, bb, q), lambda i, j: (0, j, i, 0))
    if colmajor:
        rows, wb = cm
        tok_in = pl.BlockSpec((bb, rows, wb, width), lambda i, j: (i, 0, j, cb))
        tok_out = pl.BlockSpec((bb, rows, wb, width), lambda i, j: (i, 0, j, 0))
        x_in = proj.reshape(bs, rows, GRID_W, proj.shape[1])
        tok_shape = jax.ShapeDtypeStruct((bs, rows, GRID_W, width), F32)
    else:
        tok_in = pl.BlockSpec((bb, seq, width), lambda i, j: (i, 0, cb))
        tok_out = pl.BlockSpec((bb, seq, width), lambda i, j: (i, 0, 0))
        x_in = proj.reshape(bs, seq, proj.shape[1])
        tok_shape = jax.ShapeDtypeStruct((bs, seq, width), F32)
    ut = pl.pallas_call(
        functools.partial(_s5_in_kernel, cpc=cpc, colmajor=colmajor), out_shape=cmaj, grid=(bs // bb, nj),
        in_specs=[tok_in], out_specs=cspec, compiler_params=_cparams(("arbitrary", "arbitrary")), name="s5_to_channel_major",
    )(x_in)
    yt, fin = _s5_scan(ut.reshape(width, nc * bs, q), s5w, h0, nc, bs)
    y = pl.pallas_call(
        functools.partial(_s5_out_kernel, cpc=cpc, colmajor=colmajor), out_shape=tok_shape, grid=(bs // bb, nj),
        in_specs=[cspec], out_specs=tok_out, compiler_params=_cparams(("arbitrary", "arbitrary")), name="s5_to_token_major",
    )(yt.reshape(width, nc, bs, q))
    return y.reshape(bs * seq, width), fin


def _conv_rows(pad_ref, base, n, taps, bias):
    w = pad_ref[pl.ds(base, n + 2 * SUBLANE), :]
    tot = n + 2 * SUBLANE
    xm2 = pltpu.roll(w, 2, 0)[SUBLANE:SUBLANE + n]
    xm1 = pltpu.roll(w, 1, 0)[SUBLANE:SUBLANE + n]
    x0 = w[SUBLANE:SUBLANE + n]
    xp1 = pltpu.roll(w, tot - 1, 0)[SUBLANE:SUBLANE + n]
    return taps[0:1] * xm2 + taps[1:2] * xm1 + taps[2:3] * x0 + taps[3:4] * xp1 + bias


def _fill_padded(pad_ref, x_ref, seq):
    zeros = jnp.zeros((SUBLANE, pad_ref.shape[1]), F32)
    pad_ref[0:SUBLANE, :] = zeros
    pad_ref[SUBLANE + seq:2 * SUBLANE + seq, :] = zeros
    pad_ref[SUBLANE:SUBLANE + seq, :] = x_ref[...]


def _scan_tile(a, v, h, reverse):
    s = a.shape[0]
    row = lax.broadcasted_iota(I32, a.shape, 0)
    k = 1
    while k < s:
        sh = s - k if reverse else k
        ok = (row < s - k) if reverse else (row >= k)
        a_sh = pltpu.roll(a, sh, 0)
        v_sh = pltpu.roll(v, sh, 0)
        v = v + a * jnp.where(ok, v_sh, 0.0)
        a = a * jnp.where(ok, a_sh, 1.0)
        k *= 2
    return v + a * h


def _lru_kernel(x_ref, g_ref, w_ref, pp_ref, h0_ref, o_ref, fin_ref, pad_ref, a0_ref, v0_ref, a1_ref, v1_ref, *, seq, ch):
    pp = pp_ref[...]
    bias = pp[0:1]
    lam = pp[1:2]
    cb = pp[2:3, 0:LANE]
    taps = pp[3:7, 0:LANE]
    _fill_padded(pad_ref, x_ref, seq)
    av = ((a0_ref, v0_ref), (a1_ref, v1_ref))

    def gates(i, carry):
        base = pl.multiple_of(i * ch, ch)
        xc = _conv_rows(pad_ref, base, ch, taps, cb)
        gt = _dot(xc, w_ref[...]) + bias
        for d in range(2):
            r = jax.nn.sigmoid(gt[:, 2 * d * LANE:(2 * d + 1) * LANE])
            ig = jax.nn.sigmoid(gt[:, (2 * d + 1) * LANE:(2 * d + 2) * LANE])
            log_a = -LRU_C * r * _softplus(-lam[:, d * LANE:(d + 1) * LANE])
            a = jnp.exp(log_a)
            av[d][0][pl.ds(base, ch), :] = a
            av[d][1][pl.ds(base, ch), :] = jnp.sqrt(jnp.maximum(1.0 - a * a, EPS)) * (ig * xc)
        return carry

    lax.fori_loop(0, seq // ch, gates, 0, unroll=2 if (seq // ch) % 2 == 0 else 1)
    tile = min(LRU_TILE, seq)
    nt = seq // tile

    def fwd(i, h):
        r0 = pl.multiple_of(i * tile, tile)
        hall = _scan_tile(a0_ref[pl.ds(r0, tile), :], v0_ref[pl.ds(r0, tile), :], h, False)
        o_ref[pl.ds(r0, tile), :] = hall
        return hall[tile - 1:tile, :]

    h0 = h0_ref[...]
    hf = lax.fori_loop(0, nt, fwd, h0[0:1])

    def bwd(i, h):
        r0 = pl.multiple_of((nt - 1 - i) * tile, tile)
        hall = _scan_tile(a1_ref[pl.ds(r0, tile), :], v1_ref[pl.ds(r0, tile), :], h, True)
        o_ref[pl.ds(r0, tile), :] = (o_ref[pl.ds(r0, tile), :] + hall) * jax.nn.gelu(g_ref[pl.ds(r0, tile), :])
        return hall[0:1, :]

    hb = lax.fori_loop(0, nt, bwd, h0[1:2])
    fin_ref[...] = jnp.concatenate([hf, hb], axis=0)


def _lru_branch(proj, bs, seq, lw, h0, cols):
    p_xl, p_gl, width = cols
    ncb = width // LANE
    bx, bg = p_xl // LANE, p_gl // LANE
    ch = min(256, seq)
    sc = pltpu.VMEM((seq, LANE), F32)
    hy, fin = pl.pallas_call(
        functools.partial(_lru_kernel, seq=seq, ch=ch),
        out_shape=(jax.ShapeDtypeStruct((bs * seq, width), F32), jax.ShapeDtypeStruct((bs, ncb, 2, LANE), F32)),
        grid=(bs, ncb),
        in_specs=[pl.BlockSpec((seq, LANE), lambda b, c: (b, bx + c)),
                  pl.BlockSpec((seq, LANE), lambda b, c: (b, bg + c)),
                  pl.BlockSpec((None, LANE, 4 * LANE), lambda b, c: (c, 0, 0)),
                  pl.BlockSpec((None, SUBLANE, 4 * LANE), lambda b, c: (c, 0, 0)),
                  pl.BlockSpec((None, None, 2, LANE), lambda b, c: (b, c, 0, 0))],
        out_specs=(pl.BlockSpec((seq, LANE), lambda b, c: (b, c)),
                   pl.BlockSpec((None, None, 2, LANE), lambda b, c: (b, c, 0, 0))),
        scratch_shapes=[pltpu.VMEM((seq + 2 * SUBLANE, LANE), F32), sc, sc, sc, sc],
        compiler_params=_cparams(("arbitrary", "arbitrary")),
        name="rglru",
    )(proj, proj, lw["lru_w"], lw["lru_pp"], h0)
    return hy, fin


def _ssd_kernel(xm_ref, bc_ref, dt_ref, cwx_ref, cwb_ref, pp_ref, h0_ref, y_ref, fin_ref,
                xpad_ref, bpad_ref, xa_ref, ba_ref, ex_ref, cb_ref, ext_ref, tot_ref,
                sf_ref, sb_ref, df_ref, db_ref, *, seq):
    cq = M2_CHUNK
    nc = seq // cq
    hd = M2_HEAD_DIM
    ns = M2_STATE
    npair = xm_ref.shape[1] // LANE
    hpg = 2 * npair
    pp = pp_ref[...]
    dt_bias, a_neg, dskip = pp[0:1], -jnp.exp(pp[1:2]), pp[2:3]
    cwx, cwb = cwx_ref[...], cwb_ref[...]
    _fill_padded(xpad_ref, xm_ref, seq)
    _fill_padded(bpad_ref, bc_ref, seq)

    rowi = lax.broadcasted_iota(I32, (cq, cq), 0)
    coli = lax.broadcasted_iota(I32, (cq, cq), 1)
    tri = (coli <= rowi).astype(BF16)
    lane = lax.broadcasted_iota(I32, (cq, LANE), 1)
    first_half = lane < hd
    first_half_s = lax.broadcasted_iota(I32, (ns, LANE), 1) < hd
    first_half1 = lax.broadcasted_iota(I32, (1, LANE), 1) < hd
    srefs, drefs = (sf_ref, sb_ref), (df_ref, db_ref)

    def conv_act(c):
        r0 = pl.multiple_of(c * cq, cq)
        xa_ref[pl.ds(r0, cq), :] = _silu(_conv_rows(xpad_ref, r0, cq, cwx[0:4], cwx[4:5]))
        ba_ref[pl.ds(r0, cq), :] = _silu(_conv_rows(bpad_ref, r0, cq, cwb[0:4], cwb[4:5]))

    def activations(c):
        r0 = pl.multiple_of(c * cq, cq)
        r8 = pl.multiple_of(c * SUBLANE, SUBLANE)
        dtv = _softplus(dt_ref[pl.ds(r0, cq), :] + dt_bias)
        la = dtv * a_neg
        cum = _dot_exact_lhs(tri, la)
        tot = cum[cq - 1:cq, :]
        ex = jnp.where(lane < hpg, cum, cum - la)
        ex_ref[pl.ds(r0, cq), :] = ex
        ldt = jnp.log(dtv.T[0:SUBLANE, :])
        ext = ex.T[0:SUBLANE, :]
        fwd_rows = lax.broadcasted_iota(I32, (SUBLANE, LANE), 0) < hpg
        ext_ref[pl.ds(r8, SUBLANE), :] = jnp.where(fwd_rows, ext - ldt, ext + ldt)
        tot_ref[pl.ds(r8, SUBLANE), :] = jnp.broadcast_to(tot, (SUBLANE, LANE))

    def local_states(c):
        r0 = pl.multiple_of(c * cq, cq)
        r8 = pl.multiple_of(c * SUBLANE, SUBLANE)
        bc = ba_ref[pl.ds(r0, cq), :]
        ext = ext_ref[pl.ds(r8, SUBLANE), :]
        tot = tot_ref[pl.ds(r8, 1), :]
        bt = bc.T[ns:2 * ns, :]
        cb_ref[pl.ds(r0, cq), :] = _dot(bc[:, 0:ns], bt)
        for d in range(2):
            for pr in range(npair):
                xa = xa_ref[pl.ds(r0, cq), pr * LANE:(pr + 1) * LANE]
                rs, ts = [], []
                for hh in range(2):
                    col = d * hpg + 2 * pr + hh
                    th = tot[:, col:col + 1]
                    erow = ext[col:col + 1, :]
                    din_dt = jnp.exp(th - erow) if d == 0 else jnp.exp(erow)
                    rs.append(_dot(bt * din_dt, xa))
                    ts.append(jnp.exp(th))
                srefs[d][c, pr] = jnp.where(first_half_s, rs[0], rs[1])
                r8p = pl.multiple_of((c * npair + pr) * SUBLANE, SUBLANE)
                drefs[d][pl.ds(r8p, SUBLANE), :] = jnp.broadcast_to(jnp.where(first_half1, ts[0], ts[1]), (SUBLANE, LANE))

    lax.fori_loop(0, nc, lambda i, carry: (conv_act(i), carry)[1], 0, unroll=2 if nc % 2 == 0 else 1)
    lax.fori_loop(0, nc, lambda i, carry: (activations(i), carry)[1], 0, unroll=8 if nc % 8 == 0 else 1)
    lax.fori_loop(0, nc, lambda i, carry: (local_states(i), carry)[1], 0, unroll=8 if nc % 8 == 0 else 1)

    def recur(d):
        def step(i, cur):
            c = i if d == 0 else nc - 1 - i
            out = []
            for pr in range(npair):
                r8p = pl.multiple_of((c * npair + pr) * SUBLANE, SUBLANE)
                loc = srefs[d][c, pr]
                srefs[d][c, pr] = cur[pr]
                out.append(drefs[d][pl.ds(r8p, 1), :] * cur[pr] + loc)
            return tuple(out)
        return lax.fori_loop(0, nc, step, tuple(h0_ref[d, pr] for pr in range(npair)))

    for d in range(2):
        fin = recur(d)
        for pr in range(npair):
            fin_ref[d, pr] = fin[pr]

    lower = coli <= rowi
    upper = coli >= rowi

    def output(c, carry):
        r0 = pl.multiple_of(c * cq, cq)
        r8 = pl.multiple_of(c * SUBLANE, SUBLANE)
        ex = ex_ref[pl.ds(r0, cq), :]
        ext = ext_ref[pl.ds(r8, SUBLANE), :]
        tot = tot_ref[pl.ds(r8, 1), :]
        cb = cb_ref[pl.ds(r0, cq), :]
        cm = ba_ref[pl.ds(r0, cq), 0:ns]
        for pr in range(npair):
            lanes = slice(pr * LANE, (pr + 1) * LANE)
            xa = xa_ref[pl.ds(r0, cq), lanes]
            so = _dot(cm, jnp.concatenate([sf_ref[c, pr], sb_ref[c, pr]], axis=1))
            dsk = jnp.where(first_half1, dskip[:, 2 * pr:2 * pr + 1], dskip[:, 2 * pr + 1:2 * pr + 2])
            ydiag, ecfs, ecbs = [], [], []
            for hh in range(2):
                cf, cbk = 2 * pr + hh, hpg + 2 * pr + hh
                ecf = jnp.broadcast_to(ex[:, cf:cf + 1], (cq, cq))
                ecb = jnp.broadcast_to(ex[:, cbk:cbk + 1], (cq, cq))
                lf = jnp.where(lower, jnp.exp(ecf - ext[cf:cf + 1, :]), 0.0)
                lb = jnp.where(upper, jnp.exp(ext[cbk:cbk + 1, :] - ecb), 0.0)
                ydiag.append(_dot(cb * (lf + lb), xa))
                ecfs.append(ecf)
                ecbs.append(tot[:, cbk:cbk + 1] - ecb)
            dout_f = jnp.exp(jnp.where(first_half, ecfs[0], ecfs[1]))
            dout_b = jnp.exp(jnp.where(first_half, ecbs[0], ecbs[1]))
            yoff = so[:, 0:LANE] * dout_f + so[:, LANE:2 * LANE] * dout_b
            y_ref[pl.ds(r0, cq), lanes] = jnp.where(first_half, ydiag[0], ydiag[1]) + yoff + dsk * xa
        return carry

    lax.fori_loop(0, nc, output, 0, unroll=4 if nc % 4 == 0 else 1)


def _ssd_branch(proj, bs, seq, lw, h0, cols):
    p_xm, p_bc, p_dt, inner = cols
    gw = inner // M2_GROUPS
    npair = gw // LANE
    nc = seq // M2_CHUNK
    bxm, bbc, bdt = p_xm // gw, p_bc // LANE, p_dt // LANE
    y, fin = pl.pallas_call(
        functools.partial(_ssd_kernel, seq=seq),
        out_shape=(jax.ShapeDtypeStruct((bs * seq, inner), F32),
                   jax.ShapeDtypeStruct((bs, M2_GROUPS, 2, npair, M2_STATE, LANE), F32)),
        grid=(bs, M2_GROUPS),
        in_specs=[pl.BlockSpec((seq, gw), lambda b, g: (b, bxm + g)),
                  pl.BlockSpec((seq, LANE), lambda b, g: (b, bbc + g)),
                  pl.BlockSpec((seq, LANE), lambda b, g: (b, bdt + g)),
                  pl.BlockSpec((None, SUBLANE, gw), lambda b, g: (g, 0, 0)),
                  pl.BlockSpec((None, SUBLANE, LANE), lambda b, g: (g, 0, 0)),
                  pl.BlockSpec((None, SUBLANE, LANE), lambda b, g: (g, 0, 0)),
                  pl.BlockSpec((None, None, 2, npair, M2_STATE, LANE), lambda b, g: (b, g, 0, 0, 0, 0))],
        out_specs=(pl.BlockSpec((seq, gw), lambda b, g: (b, g)),
                   pl.BlockSpec((None, None, 2, npair, M2_STATE, LANE), lambda b, g: (b, g, 0, 0, 0, 0))),
        scratch_shapes=[pltpu.VMEM((seq + 2 * SUBLANE, gw), F32), pltpu.VMEM((seq + 2 * SUBLANE, LANE), F32),
                        pltpu.VMEM((seq, gw), F32), pltpu.VMEM((seq, LANE), F32),
                        pltpu.VMEM((seq, LANE), F32), pltpu.VMEM((seq, LANE), F32),
                        pltpu.VMEM((nc * SUBLANE, LANE), F32), pltpu.VMEM((nc * SUBLANE, LANE), F32),
                        pltpu.VMEM((nc, npair, M2_STATE, LANE), F32), pltpu.VMEM((nc, npair, M2_STATE, LANE), F32),
                        pltpu.VMEM((nc * npair * SUBLANE, LANE), F32), pltpu.VMEM((nc * npair * SUBLANE, LANE), F32)],
        compiler_params=_cparams(("arbitrary", "arbitrary")),
        name="ssd",
    )(proj, proj, proj, lw["m2_cwx"], lw["m2_cwb"], lw["m2_pp"], h0)
    return y, fin


def _final_kernel(x_ref, y5_ref, u_ref, ys_ref, z_ref, hy_ref, g0_ref, g1_ref, g2_ref,
                  al_ref, sh_ref, sc_ref, d5_ref, wglu_ref, ng_ref, wm2_ref, wlru_ref, wo_ref, n2g_ref, wr_ref,
                  x1_ref, h2_ref, aff_ref):
    d = x_ref.shape[1]
    t5 = jax.nn.gelu(y5_ref[...] + d5_ref[...] * u_ref[...])
    vg = _dot(t5, wglu_ref[...])
    ya = vg[:, :d] * jax.nn.sigmoid(vg[:, d:])
    tb = _rms_scale(ys_ref[...] * _silu(z_ref[...])) * ng_ref[...]
    yb = _dot(tb, wm2_ref[...])
    yc = _dot(hy_ref[...], wlru_ref[...])
    merged = (jax.nn.sigmoid(g0_ref[...]) * ya + jax.nn.sigmoid(g1_ref[...]) * yb) + jax.nn.sigmoid(g2_ref[...]) * yc
    x1 = x_ref[...] + al_ref[...] * _dot(merged, wo_ref[...])
    x1_ref[...] = x1
    h2 = _rms_scale(x1) * n2g_ref[...] * (1.0 + sc_ref[...]) + sh_ref[...]
    h2_ref[...] = h2.astype(BF16)
    hh, hl = _split2(h2)
    r = jnp.dot(hh, wr_ref[...], preferred_element_type=F32)
    logits = r[:, :LANE] + (r[:, LANE:] + jnp.dot(hl, wr_ref[:, :LANE], preferred_element_type=F32))
    valid = lax.broadcasted_iota(I32, logits.shape, 1) < N_EXPERTS
    logits = jnp.where(valid, logits, -jnp.inf)
    m = jnp.max(logits, axis=-1, keepdims=True)
    e = jnp.where(valid, jnp.exp(logits - m), 0.0)
    aff_ref[...] = e / jnp.sum(e, axis=-1, keepdims=True)


def _final(x2d, proj, y5, yssd, hy, seq, mods, lw, cols):
    t, d = x2d.shape
    tm = min(256, seq)
    alpha, shift2, scale2 = mods
    mi = _mod_index(alpha.shape[0], tm, seq)
    p_z, p_g, p_u, s5w = cols
    row = lambda w: pl.BlockSpec((tm, w), lambda i: (i, 0))
    pcol = lambda w, off: pl.BlockSpec((tm, w), lambda i: (i, off // w))
    full = lambda a: pl.BlockSpec(a.shape, lambda i: (0,) * a.ndim, pipeline_mode=pl.Buffered(1))
    mspec = pl.BlockSpec((None, 1, d), mi)
    weights = [lw["s5_d"], lw["s5_w_glu"], lw["m2_norm_g"], lw["m2_w_out"], lw["lru_w_out"], lw["w_o"],
               lw["norm2_g"], lw["w_router"]]
    return pl.pallas_call(
        _final_kernel,
        out_shape=(jax.ShapeDtypeStruct((t, d), F32), jax.ShapeDtypeStruct((t, d), BF16),
                   jax.ShapeDtypeStruct((t, LANE), F32)),
        grid=(t // tm,),
        in_specs=[row(d), row(s5w), pcol(s5w, p_u), row(d), pcol(d, p_z), row(hy.shape[1]),
                  pcol(d, p_g), pcol(d, p_g + d), pcol(d, p_g + 2 * d), mspec, mspec, mspec]
                 + [full(w) for w in weights],
        out_specs=(row(d), row(d), row(LANE)),
        compiler_params=_cparams(("arbitrary",)),
        name="merge_out",
    )(x2d, y5, proj, yssd, proj, hy, proj, proj, proj, alpha, shift2, scale2, *weights)


def _topk_kernel(aff_ref, slot_ref, start_ref, *, cap):
    a = aff_ref[...]
    nblk, ne, _ = a.shape
    key = lax.bitcast_convert_type(a, I32)

    def count(m):
        return jnp.sum(jnp.sum(m.astype(I32), axis=0, keepdims=True), axis=2, keepdims=True)

    def body(i, lo):
        cand = lo | (jnp.int32(1) << (30 - i))
        return jnp.where(count(key >= cand) >= cap, cand, lo)

    kth = lax.fori_loop(0, 31, body, jnp.zeros((1, ne, 1), I32))
    gt = key > kth
    eq = key == kth
    need = cap - count(gt)
    rowi = lax.broadcasted_iota(I32, (LANE, LANE), 0)
    coli = lax.broadcasted_iota(I32, (LANE, LANE), 1)
    upper = (rowi <= coli).astype(BF16)

    def exclusive_rank(m):
        mf = m.astype(F32)
        incl = jnp.dot(mf.reshape(nblk * ne, LANE).astype(BF16), upper, preferred_element_type=F32).reshape(nblk, ne, LANE)
        offs, run = [], jnp.zeros((1, ne, 1), F32)
        for k in range(nblk):
            offs.append(run)
            run = run + incl[k:k + 1, :, LANE - 1:LANE]
        offs = jnp.concatenate(offs, axis=0)
        return (incl - mf + offs).astype(I32), offs.astype(I32)

    sel = gt | (eq & (exclusive_rank(eq)[0] < need))
    rank, offs = exclusive_rank(sel)
    slot_ref[...] = jnp.where(sel, rank, -1)
    start_ref[...] = jnp.broadcast_to(offs, start_ref.shape)


def _topk_slots(aff_t, cap):
    bs, nblk, ne, _ = aff_t.shape
    spec = pl.BlockSpec((None, nblk, ne, LANE), lambda b: (b, 0, 0, 0))
    return pl.pallas_call(
        functools.partial(_topk_kernel, cap=cap),
        out_shape=(jax.ShapeDtypeStruct(aff_t.shape, I32), jax.ShapeDtypeStruct(aff_t.shape, I32)),
        grid=(bs,),
        in_specs=[spec],
        out_specs=(spec, spec),
        compiler_params=_cparams(("arbitrary",)),
        name="route_topk",
    )(aff_t)


def _moe_ffn_kernel(start_ref, h_ref, slot_ref, w1_ref, w3_ref, w2_ref, y_ref, xs_ref, *, cap, tw, sb):
    e, bstep = pl.program_id(0), pl.program_id(1)
    bb, nblk = slot_ref.shape[0], slot_ref.shape[1]
    bpw = tw // LANE
    nw = nblk // bpw
    sidx = lax.broadcasted_iota(I32, (sb, LANE), 0)
    for bi in range(bb):
        base = ((bstep * bb + bi) * pl.num_programs(0) + e) * nblk
        begins = [start_ref[base + k * bpw] for k in range(nw)]
        ends = begins[1:] + [cap]
        for j in range(cap // sb):
            lo, hi = j * sb, (j + 1) * sb
            rows = slice(bi * cap + lo, bi * cap + hi)
            k_lo = sum(jnp.asarray(en <= lo, I32) for en in ends)
            k_hi = sum(jnp.asarray(bg < hi, I32) for bg in begins)
            xs_ref[rows, :] = jnp.zeros((sb, xs_ref.shape[1]), F32)

            def body(k, carry, lo=lo, rows=rows, bi=bi):
                t0 = pl.multiple_of(k * tw, tw)
                p = jnp.concatenate([(slot_ref[bi, pl.ds(k * bpw + i, 1), :] == sidx + lo).astype(BF16)
                                     for i in range(bpw)], axis=1)
                xs_ref[rows, :] += jnp.dot(p, h_ref[bi, pl.ds(t0, tw), :], preferred_element_type=F32)
                return carry

            lax.fori_loop(k_lo, k_hi, body, 0)
    xs = xs_ref[...].astype(BF16)
    hid = _silu(jnp.dot(xs, w1_ref[...], preferred_element_type=F32)) * jnp.dot(xs, w3_ref[...], preferred_element_type=F32)
    y = jnp.dot(hid.astype(BF16), w2_ref[...], preferred_element_type=F32).astype(BF16)
    for bi in range(bb):
        y_ref[bi] = y[bi * cap:(bi + 1) * cap]


def _moe_ffn(h2, slot_e, starts, w1, w3, w2, layer, cap):
    bs, n, d = h2.shape
    ne, nblk = slot_e.shape[1], slot_e.shape[2]
    tw = min(512, n)
    sb = min(LANE, cap)
    bb = max(1, min(bs, 512 // cap))
    while bs % bb:
        bb -= 1
    wspec = lambda w: pl.BlockSpec((None, None) + w.shape[2:], lambda e, b, s: (layer, e, 0, 0))
    return pl.pallas_call(
        functools.partial(_moe_ffn_kernel, cap=cap, tw=tw, sb=sb),
        out_shape=jax.ShapeDtypeStruct((bs, ne, cap, d), BF16),
        grid_spec=pltpu.PrefetchScalarGridSpec(
            num_scalar_prefetch=1,
            grid=(ne, bs // bb),
            in_specs=[pl.BlockSpec((bb, n, d), lambda e, b, s: (b, 0, 0)),
                      pl.BlockSpec((bb, None, nblk, LANE), lambda e, b, s: (b, e, 0, 0)),
                      wspec(w1), wspec(w3), wspec(w2)],
            out_specs=pl.BlockSpec((bb, None, cap, d), lambda e, b, s: (b, e, 0, 0)),
            scratch_shapes=[pltpu.VMEM((bb * cap, d), F32)]),
        compiler_params=_cparams(("arbitrary", "arbitrary")),
        name="moe_ffn",
    )(starts, h2, slot_e, w1, w3, w2)


def _combine_kernel(x_ref, y_ref, slot_ref, aff_ref, al_ref, fg_ref, o_ref, *, cap, final_norm):
    slot = slot_ref[...]
    aff = aff_ref[...]
    tq = slot.shape[0]
    sidx = lax.broadcasted_iota(I32, (tq, cap), 1)
    acc = jnp.zeros(x_ref.shape, F32)
    for e in range(N_EXPERTS):
        pt = (slot[:, e:e + 1] == sidx).astype(BF16)
        acc = acc + aff[:, e:e + 1] * jnp.dot(pt, y_ref[e], preferred_element_type=F32)
    x2 = x_ref[...] + al_ref[...] * acc
    if final_norm:
        x2 = _rms_scale(x2) * fg_ref[...]
    o_ref[...] = x2


def _combine(x2d, y, slot_t, aff, seq, alpha, final_g, final_norm):
    t, d = x2d.shape
    bs, ne, cap, _ = y.shape
    tq = min(512, seq)
    nq = seq // tq
    mi = (lambda i: (0, 0, 0)) if alpha.shape[0] == 1 else (lambda i: (i // nq, 0, 0))
    return pl.pallas_call(
        functools.partial(_combine_kernel, cap=cap, final_norm=final_norm),
        out_shape=jax.ShapeDtypeStruct((t, d), F32),
        grid=(t // tq,),
        in_specs=[pl.BlockSpec((tq, d), lambda i: (i, 0)),
                  pl.BlockSpec((None, ne, cap, d), lambda i: (i // nq, 0, 0, 0)),
                  pl.BlockSpec((tq, LANE), lambda i: (i, 0)),
                  pl.BlockSpec((tq, LANE), lambda i: (i, 0)),
                  pl.BlockSpec((None, 1, d), mi),
                  pl.BlockSpec((1, d), lambda i: (0, 0))],
        out_specs=pl.BlockSpec((tq, d), lambda i: (i, 0)),
        compiler_params=_cparams(("arbitrary",)),
        name="moe_combine",
    )(x2d, y, slot_t, aff, alpha, final_g.reshape(1, d))


def _moe(x1, h2, aff, bs, seq, alpha, moe_w, layer, final_g, final_norm):
    d = x1.shape[1]
    cap = CAPACITY * seq // N_EXPERTS
    nblk = seq // LANE
    aff_t = aff[:, :N_EXPERTS].reshape(bs, nblk, LANE, N_EXPERTS).transpose(0, 1, 3, 2)
    slot, start = _topk_slots(aff_t, cap)
    slot_e = slot.transpose(0, 2, 1, 3)
    slot_t = slot.transpose(0, 1, 3, 2).reshape(bs * seq, N_EXPERTS)
    slot_t = jnp.pad(slot_t, ((0, 0), (0, LANE - N_EXPERTS)), constant_values=-1)
    starts = start[:, :, :, 0].transpose(0, 2, 1).reshape(-1)
    y = _moe_ffn(h2.reshape(bs, seq, d), slot_e, starts, *moe_w, layer, cap)
    return _combine(x1, y, slot_t, aff, seq, alpha, final_g, final_norm)


def _layout(d):
    s5w, inner, lruw = d // 2, d, d // 2
    gn = M2_GROUPS * M2_STATE
    heads = inner // M2_HEAD_DIM
    o_u = 0
    o_z = o_u + s5w
    o_xbc = o_z + inner
    o_dt = o_xbc + inner + 2 * gn
    o_xl = o_dt + 2 * heads
    o_gl = o_xl + lruw
    o_g = o_gl + lruw
    d_in = o_g + N_BRANCH * d
    p_z = 0
    p_g = p_z + inner
    p_xm = p_g + N_BRANCH * d
    p_bc = p_xm + inner
    p_u = p_bc + M2_GROUPS * LANE
    p_xl = p_u + s5w
    p_gl = p_xl + lruw
    p_dt = p_gl + lruw
    npad = p_dt + M2_GROUPS * LANE
    hpg = heads // M2_GROUPS
    assert hpg == 4 and M2_STATE == 64 and M2_HEAD_DIM == 64, "SSD kernel packs two 64-wide heads per lane tile"
    perm = np.full((npad,), d_in, np.int32)
    perm[p_z:p_z + inner] = o_z + np.arange(inner)
    perm[p_g:p_g + N_BRANCH * d] = o_g + np.arange(N_BRANCH * d)
    perm[p_xm:p_xm + inner] = o_xbc + np.arange(inner)
    bcp = np.zeros((M2_GROUPS * LANE,), np.int32)
    for g in range(M2_GROUPS):
        bcp[g * LANE:g * LANE + M2_STATE] = inner + gn + g * M2_STATE + np.arange(M2_STATE)
        bcp[g * LANE + M2_STATE:(g + 1) * LANE] = inner + g * M2_STATE + np.arange(M2_STATE)
    perm[p_bc:p_bc + M2_GROUPS * LANE] = o_xbc + bcp
    perm[p_u:p_u + s5w] = o_u + np.arange(s5w)
    perm[p_xl:p_xl + lruw] = o_xl + np.arange(lruw)
    perm[p_gl:p_gl + lruw] = o_gl + np.arange(lruw)
    for g in range(M2_GROUPS):
        for dd in range(2):
            for j in range(hpg):
                perm[p_dt + g * LANE + dd * hpg + j] = o_dt + dd * heads + g * hpg + j
    return dict(s5w=s5w, inner=inner, lruw=lruw, heads=heads, hpg=hpg, perm=perm, bcp=bcp, npad=npad,
                p_z=p_z, p_g=p_g, p_xm=p_xm, p_bc=p_bc, p_u=p_u, p_xl=p_xl, p_gl=p_gl, p_dt=p_dt)


def _rows8(rows, width):
    out = jnp.zeros((SUBLANE, width), F32)
    for i, r in enumerate(rows):
        out = out.at[i, :r.shape[0]].set(r.astype(F32))
    return out


def _pack_layer(lay, p, q_lat, q_ctx):
    d = p["w_in"].shape[0]
    inner, lruw, hpg, heads = lay["inner"], lay["lruw"], lay["hpg"], lay["heads"]
    lw = {}
    w_ext = jnp.concatenate([p["w_in"], jnp.zeros((d, 1), p["w_in"].dtype)], axis=1)
    lw["w_in"] = w_ext[:, lay["perm"]].astype(BF16)
    lw["s5_lat"] = _s5_weights(p["s5_lam_re"], p["s5_lam_im"], p["s5_log_step"], p["s5_b_re"], p["s5_b_im"],
                               p["s5_c_re"], p["s5_c_im"], q_lat)
    lw["s5_ctx"] = lw["s5_lat"] if q_ctx == q_lat else _s5_weights(
        p["s5_lam_re"], p["s5_lam_im"], p["s5_log_step"], p["s5_b_re"], p["s5_b_im"], p["s5_c_re"], p["s5_c_im"], q_ctx)
    gw = inner // M2_GROUPS
    cw, cb = p["m2_conv_w"], p["m2_conv_b"]
    lw["m2_cwx"] = jnp.stack([_rows8([cw[k, g * gw:(g + 1) * gw] for k in range(CONV_W)] + [cb[g * gw:(g + 1) * gw]], gw)
                              for g in range(M2_GROUPS)])
    cwb, cbb = cw[:, lay["bcp"]], cb[lay["bcp"]]
    lw["m2_cwb"] = jnp.stack([_rows8([cwb[k, g * LANE:(g + 1) * LANE] for k in range(CONV_W)] + [cbb[g * LANE:(g + 1) * LANE]], LANE)
                              for g in range(M2_GROUPS)])
    dtb = p["m2_dt_bias"].reshape(2, M2_GROUPS, hpg)
    alog = p["m2_a_log"].reshape(2, M2_GROUPS, hpg)
    dsk = p["m2_d"].reshape(M2_GROUPS, hpg)
    lw["m2_pp"] = jnp.stack([_rows8([dtb[:, g].reshape(-1), alog[:, g].reshape(-1), dsk[g]], LANE) for g in range(M2_GROUPS)])
    blk = lruw // LRU_BLOCKS
    eye = jnp.eye(LRU_BLOCKS, dtype=F32)

    def dense(w):
        return jnp.einsum('hij,hk->hikj', w.astype(F32), eye).reshape(lruw, lruw)

    ncb = lruw // LANE
    assert LANE % blk == 0
    mats = [dense(p["lru_w_a"][0]), dense(p["lru_w_x"][0]), dense(p["lru_w_a"][1]), dense(p["lru_w_x"][1])]
    lw["lru_w"] = jnp.stack([jnp.concatenate([m[c * LANE:(c + 1) * LANE, c * LANE:(c + 1) * LANE] for m in mats], axis=1)
                             for c in range(ncb)]).astype(BF16)
    sl = lambda v, c: v[c * LANE:(c + 1) * LANE]
    pps = []
    for c in range(ncb):
        bias = jnp.concatenate([sl(p["lru_b_a"][0], c), sl(p["lru_b_x"][0], c), sl(p["lru_b_a"][1], c), sl(p["lru_b_x"][1], c)])
        lam = jnp.concatenate([sl(p["lru_lam"][0], c), sl(p["lru_lam"][1], c)])
        pps.append(_rows8([bias, lam, sl(p["lru_conv_b"], c)] + [sl(p["lru_conv_w"][k], c) for k in range(CONV_W)], 4 * LANE))
    lw["lru_pp"] = jnp.stack(pps)
    lw["s5_d"] = p["s5_d"].reshape(1, -1).astype(F32)
    lw["s5_w_glu"] = p["s5_w_glu"].astype(BF16)
    lw["m2_norm_g"] = p["m2_norm_g"].reshape(1, -1).astype(F32)
    lw["m2_w_out"] = p["m2_w_out"].astype(BF16)
    lw["lru_w_out"] = p["lru_w_out"].astype(BF16)
    lw["w_o"] = p["w_o"].astype(BF16)
    lw["norm2_g"] = p["norm2_g"].reshape(1, -1).astype(F32)
    lw["w_router"] = jnp.concatenate(_split2(jnp.pad(p["moe_w_router"].astype(F32), ((0, 0), (0, LANE - N_EXPERTS)))), axis=1)
    return lw


def _mixer(x2d, bs, seq, colmajor, norm_g, shift, scale, lw, lay, h0, s5w):
    proj = _inproj(x2d, seq, norm_g, shift, scale, lw["w_in"])
    y5, f5 = _s5_branch(proj, bs, seq, colmajor, s5w, h0[0], (lay["p_u"], lay["s5w"]))
    ys, fm = _ssd_branch(proj, bs, seq, lw, h0[1], (lay["p_xm"], lay["p_bc"], lay["p_dt"], lay["inner"]))
    hy, fl = _lru_branch(proj, bs, seq, lw, h0[2], (lay["p_xl"], lay["p_gl"], lay["lruw"]))
    return proj, y5, ys, hy, (f5, fm, fl)


def kernel(x, c, ctx, c_ctx, w_mod, b_mod, norm1_g, norm2_g, w_in, s5_lam_re, s5_lam_im, s5_log_step, s5_b_re, s5_b_im, s5_c_re, s5_c_im, s5_d, s5_w_glu, m2_conv_w, m2_conv_b, m2_dt_bias, m2_a_log, m2_d, m2_norm_g, m2_w_out, lru_conv_w, lru_conv_b, lru_w_a, lru_b_a, lru_w_x, lru_b_x, lru_lam, lru_w_out, w_o, moe_w_router, moe_w1, moe_w3, moe_w2, final_norm_g):
    bsz, seq, d = x.shape
    cl = ctx.shape[1]
    depth = w_mod.shape[0]
    lay = _layout(d)
    stacked = dict(norm2_g=norm2_g, w_in=w_in, s5_lam_re=s5_lam_re, s5_lam_im=s5_lam_im, s5_log_step=s5_log_step,
                   s5_b_re=s5_b_re, s5_b_im=s5_b_im, s5_c_re=s5_c_re, s5_c_im=s5_c_im, s5_d=s5_d, s5_w_glu=s5_w_glu,
                   m2_conv_w=m2_conv_w, m2_conv_b=m2_conv_b, m2_dt_bias=m2_dt_bias, m2_a_log=m2_a_log, m2_d=m2_d,
                   m2_norm_g=m2_norm_g, m2_w_out=m2_w_out, lru_conv_w=lru_conv_w, lru_conv_b=lru_conv_b,
                   lru_w_a=lru_w_a, lru_b_a=lru_b_a, lru_w_x=lru_w_x, lru_b_x=lru_b_x, lru_lam=lru_lam,
                   lru_w_out=lru_w_out, w_o=w_o, moe_w_router=moe_w_router)
    moe_w = (moe_w1.astype(BF16), moe_w3.astype(BF16), moe_w2.astype(BF16))
    rm = -(-(bsz + 1) // SUBLANE) * SUBLANE
    c_rows = jnp.zeros((rm, d), F32).at[:bsz].set(c.astype(F32)).at[bsz].set(c_ctx.astype(F32))
    mods = _modulation(c_rows, w_mod.astype(F32), b_mod.astype(F32))

    groups = lay["s5w"] // S5_GROUP
    ncb = lay["lruw"] // LANE
    npair = lay["inner"] // M2_GROUPS // LANE
    zero_h0 = (jnp.zeros((groups, 2, bsz, LANE), F32),
               jnp.zeros((bsz, M2_GROUPS, 2, npair, M2_STATE, LANE), F32),
               jnp.zeros((bsz, ncb, 2, LANE), F32))

    xs = x.reshape(bsz * seq, d).astype(F32)
    cs = ctx.reshape(bsz * cl, d).astype(F32)
    q_lat, q_ctx = min(S5_CHUNK, seq), min(S5_CHUNK, cl)
    packed = jax.vmap(lambda p: _pack_layer(lay, p, q_lat, q_ctx))(stacked)
    for i in range(depth):
        lw = jax.tree_util.tree_map(lambda v: v[i], packed)
        mx = [mods[i, :bsz, k * d:(k + 1) * d].reshape(bsz, 1, d) for k in range(6)]
        mc = [mods[i, bsz:bsz + 1, k * d:(k + 1) * d].reshape(1, 1, d) for k in range(6)]
        fcols = (lay["p_z"], lay["p_g"], lay["p_u"], lay["s5w"])
        cproj, cy5, cys, chy, cstates = _mixer(cs, bsz, cl, False, norm1_g[i], mc[0], mc[1], lw, lay, zero_h0, lw["s5_ctx"])
        if i < depth - 1:
            c1, ch2, caff = _final(cs, cproj, cy5, cys, chy, cl, (mc[2], mc[3], mc[4]), lw, fcols)
            cs = _moe(c1, ch2, caff, bsz, cl, mc[5], moe_w, i, final_norm_g, False)
        xproj, y5, ys, hy, _ = _mixer(xs, bsz, seq, True, norm1_g[i], mx[0], mx[1], lw, lay, cstates, lw["s5_lat"])
        x1, h2, aff = _final(xs, xproj, y5, ys, hy, seq, (mx[2], mx[3], mx[4]), lw, fcols)
        xs = _moe(x1, h2, aff, bsz, seq, mx[5], moe_w, i, final_norm_g, i == depth - 1)
    return xs.reshape(bsz, seq, d).astype(x.dtype)
```
